```python
import jax
import jax.numpy as jnp
from jax import lax
import numpy as np

D_MODEL = 1024
BATCH = 4
SEQ = 4096
DEPTH = 4

N_MIXERS = 3
N_RET = (DEPTH + 2) // 3
N_MLSTM = (DEPTH + 1) // 3
N_RWKV = DEPTH // 3
D_FF = -(-8 * D_MODEL // (3 * 256)) * 256
RMS_EPS = 1e-6

RET_HEADS = 4
RET_QK_DIM = D_MODEL // RET_HEADS
RET_V_DIM = 2 * D_MODEL // RET_HEADS
RET_CHUNK = 128
ROPE_BASE = 10000.0
RET_EPS = 1e-6

MLSTM_INNER = 2 * D_MODEL
MLSTM_HEADS = 4
MLSTM_HEAD_DIM = MLSTM_INNER // MLSTM_HEADS
MLSTM_CONV = 4
MLSTM_QKV_BLOCK = 4
MLSTM_CHUNK = 64
MLSTM_EPS = 1e-6

RWKV_HEAD_DIM = 64
RWKV_HEADS = D_MODEL // RWKV_HEAD_DIM
RWKV_DECAY_LORA = max(32, round(1.8 * D_MODEL ** 0.5 / 32) * 32)
RWKV_ICL_LORA = max(32, round(1.8 * D_MODEL ** 0.5 / 32) * 32)
RWKV_GATE_LORA = max(32, round(0.6 * D_MODEL ** 0.8 / 32) * 32)
RWKV_GN_EPS = 64e-5

kernel_name = 'hybrid_retention_mlstm_rwkv7_trunk'


def _rms_norm(x, g):
    xf = x.astype(jnp.float32)
    y = xf * lax.rsqrt(jnp.mean(xf * xf, axis=-1, keepdims=True) + RMS_EPS)
    return (y * g.astype(jnp.float32)).astype(x.dtype)


def _head_norm(h, eps, center):
    if center:
        h = h - jnp.mean(h, axis=-1, keepdims=True)
    y = h * lax.rsqrt(jnp.mean(h * h, axis=-1, keepdims=True) + eps)
    return y.reshape(h.shape[0], h.shape[1], -1)


def _to_chunks(t, L):
    B, S = t.shape[:2]
    t = t.reshape((B, S // L, L) + t.shape[2:])
    return jnp.swapaxes(jnp.moveaxis(t, 1, 0), 2, 3)


def _from_chunks(t):
    nC, B, H, L, d = t.shape
    return jnp.swapaxes(jnp.moveaxis(t, 0, 1), 2, 3).reshape(B, nC * L, H, d)


def _rotary(t):
    S, d = t.shape[1], t.shape[-1]
    pos = jnp.arange(S, dtype=jnp.float32)
    inv_freq = 1.0 / (ROPE_BASE ** jnp.linspace(0.0, 1.0, d // 2, dtype=jnp.float32))
    ang = pos[:, None] * inv_freq[None, :]
    cos = jnp.cos(ang)[None, :, None, :]
    sin = jnp.sin(ang)[None, :, None, :]
    t1, t2 = t[..., : d // 2], t[..., d // 2:]
    return jnp.concatenate([t1 * cos - t2 * sin, t2 * cos + t1 * sin], axis=-1)


def _retention(x, w_in, gn_g, w_out):
    B, S, _ = x.shape
    H, dk, dv, L = RET_HEADS, RET_QK_DIM, RET_V_DIM, RET_CHUNK
    proj = (x @ w_in).astype(jnp.float32)
    q, k, v, g = jnp.split(proj, [H * dk, 2 * H * dk, 2 * H * dk + H * dv], axis=-1)
    q = _rotary(q.reshape(B, S, H, dk))
    k = _rotary(k.reshape(B, S, H, dk)) * dk ** -0.5
    v = v.reshape(B, S, H, dv)
    log_gamma = jnp.log1p(-2.0 ** (-5.0 - jnp.arange(H, dtype=jnp.float32)))
    idx = jnp.arange(L, dtype=jnp.float32)
    rel = idx[:, None] - idx[None, :]
    intra_decay = jnp.where(rel >= 0, jnp.exp(log_gamma[:, None, None] * jnp.maximum(rel, 0.0)), 0.0)
    q_decay = jnp.exp(log_gamma[:, None] * (idx + 1.0))
    k_decay = jnp.exp(log_gamma[:, None] * (L - 1.0 - idx))
    chunk_decay = jnp.exp(log_gamma * L)

    def step(state, inp):
        qc, kc, vc = inp
        scores = jnp.einsum('bhld,bhmd->bhlm', qc, kc) * intra_decay
        out = jnp.einsum('bhlm,bhmv->bhlv', scores, vc)
        out = out + jnp.einsum('bhld,bhdv->bhlv', qc, state) * q_decay[None, :, :, None]
        state = state * chunk_decay[None, :, None, None] + jnp.einsum(
            'bhld,bhlv->bhdv', kc * k_decay[None, :, :, None], vc)
        return state, out

    state0 = jnp.zeros((B, H, dk, dv), jnp.float32)
    _, o = lax.scan(step, state0, (_to_chunks(q, L), _to_chunks(k, L), _to_chunks(v, L)))
    o = _head_norm(_from_chunks(o), RET_EPS, center=False) * gn_g.astype(jnp.float32)
    y = jax.nn.silu(g) * o
    return y.astype(x.dtype) @ w_out


def _block_diag(t, w):
    B, S, C = t.shape
    nb, bs, _ = w.shape
    return jnp.einsum('bsnc,ncd->bsnd', t.reshape(B, S, nb, bs), w).reshape(B, S, C)


def _causal_conv(u, w, b):
    K, C = w.shape
    y = lax.conv_general_dilated(u, w[:, None, :], window_strides=(1,), padding=[(K - 1, 0)],
                                 dimension_numbers=('NWC', 'WIO', 'NWC'), feature_group_count=C)
    return y + b


def _mlstm(x, w_in, conv_w, conv_b, wq, wk, wv, w_gate, b_gate, gn_g, skip, w_out):
    B, S, _ = x.shape
    H, dh, L = MLSTM_HEADS, MLSTM_HEAD_DIM, MLSTM_CHUNK
    f32 = jnp.float32
    u, z = jnp.split(x @ w_in, 2, axis=-1)
    c = jax.nn.silu(_causal_conv(u, conv_w, conv_b))
    q = _block_diag(c, wq)
    k = _block_diag(c, wk)
    v = _block_diag(u, wv)
    gates = (jnp.concatenate([q, k, v], axis=-1) @ w_gate + b_gate).astype(f32)
    i_pre = gates[..., :H]
    log_f = jax.nn.log_sigmoid(gates[..., H:])
    qh = q.astype(f32).reshape(B, S, H, dh)
    kh = k.astype(f32).reshape(B, S, H, dh) * dh ** -0.5
    vh = v.astype(f32).reshape(B, S, H, dh)
    causal = jnp.tril(jnp.ones((L, L), dtype=bool))

    def step(carry, inp):
        C, n, m = carry
        qc, kc, vc, ic, fc = inp
        b = jnp.cumsum(fc, axis=-1)
        log_d = jnp.where(causal, b[..., :, None] - b[..., None, :] + ic[..., None, :], -jnp.inf)
        log_inter = b + m[..., None]
        m_row = jnp.maximum(log_inter, jnp.max(log_d, axis=-1))
        d_mat = jnp.exp(log_d - m_row[..., None])
        w_inter = jnp.exp(log_inter - m_row)
        s = jnp.einsum('bhld,bhmd->bhlm', qc, kc) * d_mat
        num = jnp.einsum('bhlm,bhmv->bhlv', s, vc) + w_inter[..., None] * jnp.einsum('bhld,bhdv->bhlv', qc, C)
        den = jnp.sum(s, axis=-1) + w_inter * jnp.einsum('bhld,bhd->bhl', qc, n)
        h = num / jnp.maximum(jnp.abs(den), jnp.exp(-m_row))[..., None]
        b_last = b[..., -1]
        log_w = b_last[..., None] - b + ic
        m_new = jnp.maximum(b_last + m, jnp.max(log_w, axis=-1))
        w_k = jnp.exp(log_w - m_new[..., None])
        carry_decay = jnp.exp(b_last + m - m_new)
        C = carry_decay[..., None, None] * C + jnp.einsum('bhld,bhlv->bhdv', kc * w_k[..., None], vc)
        n = carry_decay[..., None] * n + jnp.einsum('bhld,bhl->bhd', kc, w_k)
        return (C, n, m_new), h

    carry0 = (jnp.zeros((B, H, dh, dh), f32), jnp.zeros((B, H, dh), f32), jnp.zeros((B, H), f32))
    _, h = lax.scan(step, carry0, (_to_chunks(qh, L), _to_chunks(kh, L), _to_chunks(vh, L),
                                   _to_chunks(i_pre, L), _to_chunks(log_f, L)))
    h = _head_norm(_from_chunks(h), MLSTM_EPS, center=True) * gn_g
    h = (h + skip * c.astype(f32)) * jax.nn.silu(z.astype(f32))
    return h.astype(x.dtype) @ w_out


def _rwkv7(x, mu, w_rkv, w0, w_lora_a, w_lora_b, a0, a_lora_a, a_lora_b, g_lora_a, g_lora_b,
           k_k, k_a, r_k, gn_g, gn_b, w_out):
    B, S, D = x.shape
    H, dh = RWKV_HEADS, RWKV_HEAD_DIM
    f32 = jnp.float32
    x_prev = jnp.pad(x[:, :-1], ((0, 0), (1, 0), (0, 0)))
    mixes = x[None] + (x_prev - x)[None] * mu[:, None, None, :]
    rkv = jnp.einsum('nbsd,nde->nbse', mixes[:3], w_rkv).astype(f32)
    r, k, v = rkv[0], rkv[1], rkv[2]
    xw, xa, xg = mixes[3], mixes[4], mixes[5]
    w_log = -jax.nn.softplus(-(w0 + jnp.tanh(xw @ w_lora_a) @ w_lora_b).astype(f32)) - 0.5
    decay = jnp.exp(-jnp.exp(w_log))
    a = jax.nn.sigmoid((a0 + (xa @ a_lora_a) @ a_lora_b).astype(f32))
    g = (jax.nn.sigmoid(xg @ g_lora_a) @ g_lora_b).astype(f32)
    kk = (k * k_k).reshape(B, S, H, dh)
    kk = kk * lax.rsqrt(jnp.maximum(jnp.sum(kk * kk, axis=-1, keepdims=True), 1e-24))
    k = k * (1.0 + (a - 1.0) * k_a)

    def heads_time_major(t):
        return jnp.moveaxis(t.reshape(B, S, H, dh), 1, 0)

    def step(state, inp):
        r_t, w_t, k_t, v_t, kk_t, a_t = inp
        sa = jnp.einsum('bhvk,bhk->bhv', state, -kk_t)
        state = (state * w_t[:, :, None, :] + sa[..., None] * (kk_t * a_t)[:, :, None, :]
                 + v_t[..., None] * k_t[:, :, None, :])
        y_t = jnp.einsum('bhvk,bhk->bhv', state, r_t)
        return state, y_t

    state0 = jnp.zeros((B, H, dh, dh), f32)
    _, y = lax.scan(step, state0, (heads_time_major(r), heads_time_major(decay), heads_time_major(k),
                                   heads_time_major(v), jnp.moveaxis(kk, 1, 0), heads_time_major(a)))
    y = jnp.moveaxis(y, 0, 1)
    y = _head_norm(y, RWKV_GN_EPS, center=True) * gn_g + gn_b
    rh, kh, vh = r.reshape(B, S, H, dh), k.reshape(B, S, H, dh), v.reshape(B, S, H, dh)
    bonus = (jnp.sum(rh * kh * r_k.astype(f32), axis=-1, keepdims=True) * vh).reshape(B, S, D)
    out = (y + bonus) * g
    return out.astype(x.dtype) @ w_out


def _swiglu(x, w_gu, w_down):
    gate, up = jnp.split(x @ w_gu, 2, axis=-1)
    return (jax.nn.silu(gate) * up) @ w_down


def setup_inputs(seed: int = 0) -> dict:
    key = jax.random.key(seed)
    ks = iter(jax.random.split(key, 48))
    f32 = jnp.float32

    def nrm(shape, scale):
        return jax.random.normal(next(ks), shape, f32) * scale

    D = D_MODEL
    I = MLSTM_INNER
    HV = RET_HEADS * RET_V_DIM
    ret_cols = 2 * RET_HEADS * RET_QK_DIM + 2 * HV
    nb = I // MLSTM_QKV_BLOCK
    bs = MLSTM_QKV_BLOCK
    x = nrm((BATCH, SEQ, D), 1.0)
    norm_mix = 1.0 + nrm((DEPTH, D), 0.02)
    norm_ffn = 1.0 + nrm((DEPTH, D), 0.02)
    norm_final = 1.0 + nrm((D,), 0.02)
    ret_w_in = nrm((N_RET, D, ret_cols), D ** -0.5)
    ret_gn = 1.0 + nrm((N_RET, HV), 0.02)
    ret_w_out = nrm((N_RET, HV, D), HV ** -0.5)
    ml_w_in = nrm((N_MLSTM, D, 2 * I), D ** -0.5)
    ml_conv_w = nrm((N_MLSTM, MLSTM_CONV, I), MLSTM_CONV ** -0.5)
    ml_conv_b = nrm((N_MLSTM, I), 0.01)
    ml_wq = nrm((N_MLSTM, nb, bs, bs), bs ** -0.5)
    ml_wk = nrm((N_MLSTM, nb, bs, bs), bs ** -0.5)
    ml_wv = nrm((N_MLSTM, nb, bs, bs), bs ** -0.5)
    ml_w_gate = nrm((N_MLSTM, 3 * I, 2 * MLSTM_HEADS), 0.1 * (3 * I) ** -0.5)
    f_bias = jnp.linspace(3.0, 6.0, MLSTM_HEADS, dtype=f32)
    ml_b_gate = jnp.concatenate([nrm((N_MLSTM, MLSTM_HEADS), 0.1),
                                 f_bias[None, :] + nrm((N_MLSTM, MLSTM_HEADS), 0.1)], axis=-1)
    ml_gn = 1.0 + nrm((N_MLSTM, I), 0.02)
    ml_skip = 1.0 + nrm((N_MLSTM, I), 0.02)
    ml_w_out = nrm((N_MLSTM, I, D), I ** -0.5)
    rw_mu = jax.random.uniform(next(ks), (N_RWKV, 6, D), f32)
    rw_w_rkv = nrm((N_RWKV, 3, D, D), D ** -0.5)
    rw_w0 = jnp.linspace(-6.0, -1.0, D, dtype=f32)[None, :] + nrm((N_RWKV, D), 0.1)
    rw_w_lora_a = nrm((N_RWKV, D, RWKV_DECAY_LORA), D ** -0.5)
    rw_w_lora_b = nrm((N_RWKV, RWKV_DECAY_LORA, D), 0.5 * RWKV_DECAY_LORA ** -0.5)
    rw_a0 = nrm((N_RWKV, D), 0.1)
    rw_a_lora_a = nrm((N_RWKV, D, RWKV_ICL_LORA), D ** -0.5)
    rw_a_lora_b = nrm((N_RWKV, RWKV_ICL_LORA, D), 0.5 * RWKV_ICL_LORA ** -0.5)
    rw_g_lora_a = nrm((N_RWKV, D, RWKV_GATE_LORA), D ** -0.5)
    rw_g_lora_b = nrm((N_RWKV, RWKV_GATE_LORA, D), RWKV_GATE_LORA ** -0.5)
    rw_k_k = 0.85 + nrm((N_RWKV, D), 0.05)
    rw_k_a = 1.0 + nrm((N_RWKV, D), 0.05)
    rw_r_k = nrm((N_RWKV, RWKV_HEADS, RWKV_HEAD_DIM), 0.1)
    rw_gn_g = 1.0 + nrm((N_RWKV, D), 0.02)
    rw_gn_b = nrm((N_RWKV, D), 0.01)
    rw_w_out = nrm((N_RWKV, D, D), D ** -0.5)
    ffn_w_gu = nrm((DEPTH, D, 2 * D_FF), D ** -0.5)
    ffn_w_down = nrm((DEPTH, D_FF, D), D_FF ** -0.5)
    return {
        'x': x, 'norm_mix': norm_mix, 'norm_ffn': norm_ffn, 'norm_final': norm_final,
        'ret_w_in': ret_w_in, 'ret_gn': ret_gn, 'ret_w_out': ret_w_out,
        'ml_w_in': ml_w_in, 'ml_conv_w': ml_conv_w, 'ml_conv_b': ml_conv_b,
        'ml_wq': ml_wq, 'ml_wk': ml_wk, 'ml_wv': ml_wv, 'ml_w_gate': ml_w_gate, 'ml_b_gate': ml_b_gate,
        'ml_gn': ml_gn, 'ml_skip': ml_skip, 'ml_w_out': ml_w_out,
        'rw_mu': rw_mu, 'rw_w_rkv': rw_w_rkv, 'rw_w0': rw_w0, 'rw_w_lora_a': rw_w_lora_a,
        'rw_w_lora_b': rw_w_lora_b, 'rw_a0': rw_a0, 'rw_a_lora_a': rw_a_lora_a, 'rw_a_lora_b': rw_a_lora_b,
        'rw_g_lora_a': rw_g_lora_a, 'rw_g_lora_b': rw_g_lora_b, 'rw_k_k': rw_k_k, 'rw_k_a': rw_k_a,
        'rw_r_k': rw_r_k, 'rw_gn_g': rw_gn_g, 'rw_gn_b': rw_gn_b, 'rw_w_out': rw_w_out,
        'ffn_w_gu': ffn_w_gu, 'ffn_w_down': ffn_w_down,
    }


def reference(x, norm_mix, norm_ffn, norm_final, ret_w_in, ret_gn, ret_w_out,
              ml_w_in, ml_conv_w, ml_conv_b, ml_wq, ml_wk, ml_wv, ml_w_gate, ml_b_gate,
              ml_gn, ml_skip, ml_w_out,
              rw_mu, rw_w_rkv, rw_w0, rw_w_lora_a, rw_w_lora_b, rw_a0, rw_a_lora_a, rw_a_lora_b,
              rw_g_lora_a, rw_g_lora_b, rw_k_k, rw_k_a, rw_r_k, rw_gn_g, rw_gn_b, rw_w_out,
              ffn_w_gu, ffn_w_down):
    h = x
    for i in range(DEPTH):
        xn = _rms_norm(h, norm_mix[i])
        kind = i % N_MIXERS
        j = i // N_MIXERS
        if kind == 0:
            y = _retention(xn, ret_w_in[j], ret_gn[j], ret_w_out[j])
        elif kind == 1:
            y = _mlstm(xn, ml_w_in[j], ml_conv_w[j], ml_conv_b[j], ml_wq[j], ml_wk[j], ml_wv[j],
                       ml_w_gate[j], ml_b_gate[j], ml_gn[j], ml_skip[j], ml_w_out[j])
        else:
            y = _rwkv7(xn, rw_mu[j], rw_w_rkv[j], rw_w0[j], rw_w_lora_a[j], rw_w_lora_b[j],
                       rw_a0[j], rw_a_lora_a[j], rw_a_lora_b[j], rw_g_lora_a[j], rw_g_lora_b[j],
                       rw_k_k[j], rw_k_a[j], rw_r_k[j], rw_gn_g[j], rw_gn_b[j], rw_w_out[j])
        h = h + y.astype(h.dtype)
        h = h + _swiglu(_rms_norm(h, norm_ffn[i]), ffn_w_gu[i], ffn_w_down[i]).astype(h.dtype)
    return _rms_norm(h, norm_final)
```

```python
import functools
import math

import jax
import jax.numpy as jnp
from jax import lax
from jax.experimental import pallas as pl
from jax.experimental.pallas import tpu as pltpu

F32 = jnp.float32
BF16 = jnp.bfloat16

D_MODEL = 1024
D_FF = 2816
RMS_EPS = 1e-6

RET_HEADS = 4
RET_DK = 256
RET_DV = 512
RET_EPS = 1e-6
ROPE_BASE = 10000.0

ML_INNER = 2048
ML_HEADS = 4
ML_DH = 512
ML_CONV = 4
ML_EPS = 1e-6

RW_HEADS = 16
RW_DH = 64
RW_GN_EPS = 64e-5

V7X_LANES = 128
V7X_SUBLANES = 8
V7X_VMEM_LIMIT = 56 * 1024 * 1024

TM_PROJ = 1024
TM_OUT = 512
TM_FFN = 512
FFN_CHUNK = 256
TM_PRE = 256
RET_CHUNK = 128
ML_CHUNK = 256
RW_CHUNK = 64


def _params(sem, vmem=V7X_VMEM_LIMIT):
    return pltpu.CompilerParams(dimension_semantics=sem, vmem_limit_bytes=vmem)


def _dot(a, b):
    return jnp.dot(a, b, preferred_element_type=F32)


def _dot_nt(a, b):
    return lax.dot_general(a, b, (((1,), (1,)), ((), ())), preferred_element_type=F32)


def _dot_tn(a, b):
    return lax.dot_general(a, b, (((0,), (0,)), ((), ())), preferred_element_type=F32)


def _dot_exact(a, b):
    hi = b.astype(BF16)
    lo = (b - hi.astype(F32)).astype(BF16)
    a16 = a.astype(BF16)
    return _dot(a16, hi) + _dot(a16, lo)


def _rms(x, g):
    ms = jnp.mean(x * x, axis=-1, keepdims=True)
    return x * lax.rsqrt(ms + RMS_EPS) * g


def _silu(x):
    return x * jax.nn.sigmoid(x)


def _const_spec(shape):
    nd = len(shape)
    return pl.BlockSpec(shape, lambda *_: (0,) * nd)


def _ret_proj_kernel(h_ref, g_ref, w_ref, cos_ref, sin_ref, o_ref, xn_ref, *, tn):
    j = pl.program_id(1)

    @pl.when(j == 0)
    def _():
        xn_ref[...] = _rms(h_ref[...], g_ref[...]).astype(BF16)

    acc = _dot(xn_ref[...], w_ref[...])
    n_qk = 2 * RET_HEADS * RET_DK // tn
    n_q = RET_HEADS * RET_DK // tn
    n_v = RET_HEADS * RET_DV // tn
    half = RET_DK // 2

    @pl.when(j < n_qk)
    def _():
        cos = cos_ref[...]
        sin = sin_ref[...]
        scale = jnp.where(j >= n_q, RET_DK ** -0.5, 1.0).astype(F32)
        for s in range(tn // RET_DK):
            t1 = acc[:, s * RET_DK:s * RET_DK + half]
            t2 = acc[:, s * RET_DK + half:(s + 1) * RET_DK]
            o_ref[:, s * RET_DK:s * RET_DK + half] = ((t1 * cos - t2 * sin) * scale).astype(BF16)
            o_ref[:, s * RET_DK + half:(s + 1) * RET_DK] = ((t2 * cos + t1 * sin) * scale).astype(BF16)

    @pl.when(jnp.logical_and(j >= n_qk, j < n_qk + n_v))
    def _():
        o_ref[...] = acc.astype(BF16)

    @pl.when(j >= n_qk + n_v)
    def _():
        o_ref[...] = _silu(acc).astype(BF16)


def _ret_proj(h, g, w, cos, sin, seq):
    t, d = h.shape
    n = w.shape[1]
    tm, tn = min(TM_PROJ, seq), 512
    spt = seq // tm
    return pl.pallas_call(
        functools.partial(_ret_proj_kernel, tn=tn),
        out_shape=jax.ShapeDtypeStruct((t, n), BF16),
        grid=(t // tm, n // tn),
        in_specs=[
            pl.BlockSpec((tm, d), lambda i, j: (i, 0)),
            pl.BlockSpec((1, d), lambda i, j: (0, 0)),
            pl.BlockSpec((d, tn), lambda i, j: (0, j)),
            pl.BlockSpec((tm, RET_DK // 2), lambda i, j: (i % spt, 0)),
            pl.BlockSpec((tm, RET_DK // 2), lambda i, j: (i % spt, 0)),
        ],
        out_specs=pl.BlockSpec((tm, tn), lambda i, j: (i, j)),
        scratch_shapes=[pltpu.VMEM((tm, d), BF16)],
        compiler_params=_params(("parallel", "arbitrary")),
        name="ret_proj",
    )(h, g, w, cos, sin)


def _ml_proj_kernel(h_ref, g_ref, w_ref, o_ref, xn_ref, *, tn):
    j = pl.program_id(1)

    @pl.when(j == 0)
    def _():
        xn_ref[...] = _rms(h_ref[...], g_ref[...]).astype(BF16)

    acc = _dot(xn_ref[...], w_ref[...])
    n_u = ML_INNER // tn

    @pl.when(j < n_u)
    def _():
        o_ref[...] = acc.astype(BF16)

    @pl.when(j >= n_u)
    def _():
        o_ref[...] = _silu(acc).astype(BF16)


def _ml_proj(h, g, w, seq):
    t, d = h.shape
    n = w.shape[1]
    tm, tn = min(TM_PROJ, seq), 512
    return pl.pallas_call(
        functools.partial(_ml_proj_kernel, tn=tn),
        out_shape=jax.ShapeDtypeStruct((t, n), BF16),
        grid=(t // tm, n // tn),
        in_specs=[
            pl.BlockSpec((tm, d), lambda i, j: (i, 0)),
            pl.BlockSpec((1, d), lambda i, j: (0, 0)),
            pl.BlockSpec((d, tn), lambda i, j: (0, j)),
        ],
        out_specs=pl.BlockSpec((tm, tn), lambda i, j: (i, j)),
        scratch_shapes=[pltpu.VMEM((tm, d), BF16)],
        compiler_params=_params(("parallel", "arbitrary")),
        name="ml_proj",
    )(h, g, w)


def _out_proj_kernel(a_ref, w_ref, h_ref, o_ref):
    o_ref[...] = h_ref[...] + _dot(a_ref[...], w_ref[...])


def _out_proj(a, w, h):
    t, k = a.shape
    d = w.shape[1]
    tm = min(TM_OUT, t)
    return pl.pallas_call(
        _out_proj_kernel,
        out_shape=jax.ShapeDtypeStruct((t, d), F32),
        grid=(t // tm,),
        in_specs=[
            pl.BlockSpec((tm, k), lambda i: (i, 0)),
            _const_spec((k, d)),
            pl.BlockSpec((tm, d), lambda i: (i, 0)),
        ],
        out_specs=pl.BlockSpec((tm, d), lambda i: (i, 0)),
        compiler_params=_params(("parallel",)),
        name="out_proj",
    )(a, w, h)


def _ffn_kernel(h_ref, g_ref, wg_ref, wu_ref, wd_ref, gf_ref, o_ref, *, final_norm):
    h = h_ref[...]
    xn = _rms(h, g_ref[...]).astype(BF16)
    acc = h
    for f in range(0, D_FF, FFN_CHUNK):
        gate = _dot(xn, wg_ref[:, f:f + FFN_CHUNK])
        up = _dot(xn, wu_ref[:, f:f + FFN_CHUNK])
        act = (_silu(gate) * up).astype(BF16)
        acc = acc + _dot(act, wd_ref[f:f + FFN_CHUNK, :])
    if final_norm:
        acc = _rms(acc, gf_ref[...])
    o_ref[...] = acc


def _ffn(h, g, wg, wu, wd, gf, final_norm):
    t, d = h.shape
    tm = min(TM_FFN, t)
    return pl.pallas_call(
        functools.partial(_ffn_kernel, final_norm=final_norm),
        out_shape=jax.ShapeDtypeStruct((t, d), F32),
        grid=(t // tm,),
        in_specs=[
            pl.BlockSpec((tm, d), lambda i: (i, 0)),
            _const_spec((1, d)),
            _const_spec((d, D_FF)),
            _const_spec((d, D_FF)),
            _const_spec((D_FF, d)),
            _const_spec((1, d)),
        ],
        out_specs=pl.BlockSpec((tm, d), lambda i: (i, 0)),
        compiler_params=_params(("parallel",)),
        name="ffn",
    )(h, g, wg, wu, wd, gf)


def _ret_log_gamma(head):
    return math.log1p(-2.0 ** (-5.0 - head))


def _ret_chunk_kernel(q_ref, k_ref, v_ref, g_ref, gn_ref, o_ref, st_ref):
    c = pl.program_id(1)
    ln = q_ref.shape[0]

    @pl.when(c == 0)
    def _():
        st_ref[...] = jnp.zeros_like(st_ref)

    row = lax.broadcasted_iota(jnp.int32, (ln, ln), 0)
    col = lax.broadcasted_iota(jnp.int32, (ln, ln), 1)
    rel = (row - col).astype(F32)
    pos = lax.broadcasted_iota(jnp.int32, (ln, 1), 0).astype(F32)
    for hd in range(RET_HEADS):
        lg = _ret_log_gamma(hd)
        q = q_ref[:, hd * RET_DK:(hd + 1) * RET_DK]
        k = k_ref[:, hd * RET_DK:(hd + 1) * RET_DK]
        v = v_ref[:, hd * RET_DV:(hd + 1) * RET_DV]
        decay = jnp.where(rel >= 0, jnp.exp(lg * jnp.maximum(rel, 0.0)), 0.0)
        scores = _dot_nt(q, k) * decay
        st = st_ref[hd]
        out = _dot(scores.astype(BF16), v)
        out = out + _dot(q, st.astype(BF16)) * jnp.exp(lg * (pos + 1.0))
        k_dec = (k.astype(F32) * jnp.exp(lg * (ln - 1.0 - pos))).astype(BF16)
        st_ref[hd] = st * math.exp(lg * ln) + _dot_tn(k_dec, v)
        ms = jnp.mean(out * out, axis=-1, keepdims=True)
        y = out * lax.rsqrt(ms + RET_EPS) * gn_ref[:, hd * RET_DV:(hd + 1) * RET_DV]
        y = y * g_ref[:, hd * RET_DV:(hd + 1) * RET_DV].astype(F32)
        o_ref[:, hd * RET_DV:(hd + 1) * RET_DV] = y.astype(BF16)


def _ret_chunk(proj, gn, batch, seq):
    t = proj.shape[0]
    ln = min(RET_CHUNK, seq)
    nc = seq // ln
    hk = RET_HEADS * RET_DK
    hv = RET_HEADS * RET_DV
    return pl.pallas_call(
        _ret_chunk_kernel,
        out_shape=jax.ShapeDtypeStruct((t, hv), BF16),
        grid=(batch, nc),
        in_specs=[
            pl.BlockSpec((ln, hk), lambda b, c: (b * nc + c, 0)),
            pl.BlockSpec((ln, hk), lambda b, c: (b * nc + c, 1)),
            pl.BlockSpec((ln, hv), lambda b, c: (b * nc + c, 1)),
            pl.BlockSpec((ln, hv), lambda b, c: (b * nc + c, 2)),
            pl.BlockSpec((1, hv), lambda b, c: (0, 0)),
        ],
        out_specs=pl.BlockSpec((ln, hv), lambda b, c: (b * nc + c, 0)),
        scratch_shapes=[pltpu.VMEM((RET_HEADS, RET_DK, RET_DV), F32)],
        compiler_params=_params(("parallel", "arbitrary")),
        name="ret_chunk",
    )(proj, proj, proj, proj, gn)


def _prev_rows(cur, prev8, shift):
    ext = jnp.concatenate([prev8, cur], axis=0)
    return pltpu.roll(ext, shift, axis=0)[V7X_SUBLANES:, :]


def _log_sigmoid(x):
    return jnp.minimum(x, 0.0) - jnp.log1p(jnp.exp(-jnp.abs(x)))


def _ml_pre_kernel(u_ref, up_ref, cw_ref, cb_ref, mq_ref, mk_ref, mv_ref, wgt_ref, bg_ref,
                   c_ref, q_ref, k_ref, v_ref, gt_ref, *, tiles_per_seq):
    i = pl.program_id(0)
    u = u_ref[...].astype(F32)
    first = (i % tiles_per_seq) == 0
    prev = jnp.where(first, 0.0, up_ref[...].astype(F32))
    conv = u * cw_ref[ML_CONV - 1:ML_CONV, :] + cb_ref[...]
    for s in range(1, ML_CONV):
        conv = conv + _prev_rows(u, prev, s) * cw_ref[ML_CONV - 1 - s:ML_CONV - s, :]
    c = _silu(conv)
    c16 = c.astype(BF16)
    u16 = u_ref[...]
    c_ref[...] = c16
    blk = mq_ref.shape[1]
    gates = jnp.zeros((u.shape[0], V7X_LANES), F32) + bg_ref[...]
    for n in range(ML_INNER // blk):
        sl = slice(n * blk, (n + 1) * blk)
        qn = _dot(c16[:, sl], mq_ref[n]).astype(BF16)
        kn = _dot(c16[:, sl], mk_ref[n]).astype(BF16)
        vn = _dot(u16[:, sl], mv_ref[n]).astype(BF16)
        q_ref[:, sl] = qn
        k_ref[:, sl] = (kn.astype(F32) * (ML_DH ** -0.5)).astype(BF16)
        v_ref[:, sl] = vn
        gates = gates + _dot(qn, wgt_ref[0, sl, :]) + _dot(kn, wgt_ref[1, sl, :]) + _dot(vn, wgt_ref[2, sl, :])
    lane = lax.broadcasted_iota(jnp.int32, gates.shape, 1)
    gt_ref[...] = jnp.where(jnp.logical_and(lane >= ML_HEADS, lane < 2 * ML_HEADS),
                            _log_sigmoid(gates), gates)


def _ml_pre(proj, conv_w, conv_b, mq, mk, mv, wgt, bg, seq):
    t = proj.shape[0]
    tm = min(TM_PRE, seq)
    tps = seq // tm
    rb = tm // V7X_SUBLANES
    blk = mq.shape[1]
    act = jax.ShapeDtypeStruct((t, ML_INNER), BF16)
    row_spec = pl.BlockSpec((tm, ML_INNER), lambda i: (i, 0))
    return pl.pallas_call(
        functools.partial(_ml_pre_kernel, tiles_per_seq=tps),
        out_shape=(act, act, act, act, jax.ShapeDtypeStruct((t, V7X_LANES), F32)),
        grid=(t // tm,),
        in_specs=[
            row_spec,
            pl.BlockSpec((V7X_SUBLANES, ML_INNER), lambda i: (jnp.maximum(i * rb - 1, 0), 0)),
            _const_spec((ML_CONV, ML_INNER)),
            _const_spec((1, ML_INNER)),
            _const_spec(mq.shape), _const_spec(mk.shape), _const_spec(mv.shape),
            _const_spec(wgt.shape),
            _const_spec((1, V7X_LANES)),
        ],
        out_specs=(row_spec, row_spec, row_spec, row_spec,
                   pl.BlockSpec((tm, V7X_LANES), lambda i: (i, 0))),
        compiler_params=_params(("parallel",)),
        name="ml_pre",
    )(proj, proj, conv_w, conv_b, mq, mk, mv, wgt, bg)


def _ml_chunk_kernel(q_ref, k_ref, v_ref, gt_ref, c_ref, z_ref, gn_ref, sk_ref, o_ref,
                     cst_ref, nst_ref, mst_ref):
    ci = pl.program_id(1)
    ln = q_ref.shape[0]

    @pl.when(ci == 0)
    def _():
        cst_ref[...] = jnp.zeros_like(cst_ref)
        nst_ref[...] = jnp.zeros_like(nst_ref)
        mst_ref[...] = jnp.zeros_like(mst_ref)

    row = lax.broadcasted_iota(jnp.int32, (ln, ln), 0)
    col = lax.broadcasted_iota(jnp.int32, (ln, ln), 1)
    causal = col <= row
    gates = gt_ref[...]
    cum = _dot_exact(causal.astype(F32), gates)
    gates_t = gates.T
    cum_t = cum.T
    for hd in range(ML_HEADS):
        sl = slice(hd * ML_DH, (hd + 1) * ML_DH)
        q = q_ref[:, sl]
        k = k_ref[:, sl]
        v = v_ref[:, sl]
        i_col = gates[:, hd:hd + 1]
        i_row = gates_t[hd:hd + 1, :]
        b_col = cum[:, ML_HEADS + hd:ML_HEADS + hd + 1]
        b_row = cum_t[ML_HEADS + hd:ML_HEADS + hd + 1, :]
        m_prev = mst_ref[hd:hd + 1, 0:1]
        log_d = jnp.where(causal, b_col - b_row + i_row, -jnp.inf)
        log_inter = b_col + m_prev
        m_row = jnp.maximum(log_inter, jnp.max(log_d, axis=-1, keepdims=True))
        d_mat = jnp.exp(log_d - m_row)
        w_inter = jnp.exp(log_inter - m_row)
        s = _dot_nt(q, k) * d_mat
        cst = cst_ref[hd]
        n_row = nst_ref[hd:hd + 1, :]
        num = _dot(s.astype(BF16), v) + w_inter * _dot(q, cst.astype(BF16))
        den = jnp.sum(s, axis=-1, keepdims=True) + w_inter * jnp.sum(
            q.astype(F32) * n_row, axis=-1, keepdims=True)
        hh = num / jnp.maximum(jnp.abs(den), jnp.exp(-m_row))
        b_last = b_col[ln - 1:ln, :]
        log_w = b_last - b_col + i_col
        m_new = jnp.maximum(b_last + m_prev, jnp.max(log_w, axis=0, keepdims=True))
        kw = k.astype(F32) * jnp.exp(log_w - m_new)
        carry = jnp.exp(b_last + m_prev - m_new)
        cst_ref[hd] = carry * cst + _dot_tn(kw.astype(BF16), v)
        nst_ref[hd:hd + 1, :] = carry * n_row + jnp.sum(kw, axis=0, keepdims=True)
        mst_ref[hd:hd + 1, :] = jnp.broadcast_to(m_new, (1, V7X_LANES))
        hc = hh - jnp.mean(hh, axis=-1, keepdims=True)
        y = hc * lax.rsqrt(jnp.mean(hc * hc, axis=-1, keepdims=True) + ML_EPS) * gn_ref[:, sl]
        y = (y + sk_ref[:, sl] * c_ref[:, sl].astype(F32)) * z_ref[:, sl].astype(F32)
        o_ref[:, sl] = y.astype(BF16)


def _ml_chunk(q, k, v, gates, c, proj, gn, skip, batch, seq):
    t = q.shape[0]
    ln = min(ML_CHUNK, seq)
    nc = seq // ln
    row_spec = pl.BlockSpec((ln, ML_INNER), lambda b, ci: (b * nc + ci, 0))
    return pl.pallas_call(
        _ml_chunk_kernel,
        out_shape=jax.ShapeDtypeStruct((t, ML_INNER), BF16),
        grid=(batch, nc),
        in_specs=[
            row_spec, row_spec, row_spec,
            pl.BlockSpec((ln, V7X_LANES), lambda b, ci: (b * nc + ci, 0)),
            row_spec,
            pl.BlockSpec((ln, ML_INNER), lambda b, ci: (b * nc + ci, 1)),
            pl.BlockSpec((1, ML_INNER), lambda b, ci: (0, 0)),
            pl.BlockSpec((1, ML_INNER), lambda b, ci: (0, 0)),
        ],
        out_specs=row_spec,
        scratch_shapes=[
            pltpu.VMEM((ML_HEADS, ML_DH, ML_DH), F32),
            pltpu.VMEM((V7X_SUBLANES, ML_DH), F32),
            pltpu.VMEM((V7X_SUBLANES, V7X_LANES), F32),
        ],
        compiler_params=_params(("parallel", "arbitrary")),
        name="ml_chunk",
    )(q, k, v, gates, c, proj, gn, skip)


def _dot_split(x, m):
    hi = x.astype(BF16)
    lo = (x - hi.astype(F32)).astype(BF16)
    return _dot(hi, m) + _dot(lo, m)


def _rw_pre_kernel(h_ref, hp_ref, g_ref, mu_ref, wrkv_ref, la_ref, lbw_ref, lba_ref, lbg_ref,
                   w0_ref, a0_ref, kk_ref, ka_ref, rk_ref, e_ref, et_ref,
                   r_out, lw_out, k_out, v_out, kn_out, b_out, bonus_out, g_out, *, tiles_per_seq,
                   lora_w, lora_a):
    i = pl.program_id(0)
    g = g_ref[...]
    xn = _rms(h_ref[...], g)
    first = (i % tiles_per_seq) == 0
    xp8 = jnp.where(first, 0.0, _rms(hp_ref[...], g))
    dx = _prev_rows(xn, xp8, 1) - xn

    def mix(n):
        return (xn + dx * mu_ref[n:n + 1, :]).astype(BF16)

    r = _dot(mix(0), wrkv_ref[0])
    k = _dot(mix(1), wrkv_ref[1])
    v = _dot(mix(2), wrkv_ref[2])
    hw = jnp.tanh(_dot(mix(3), la_ref[:, 0:lora_w])).astype(BF16)
    ha = _dot(mix(4), la_ref[:, lora_w:lora_w + lora_a]).astype(BF16)
    hg = jax.nn.sigmoid(_dot(mix(5), la_ref[:, lora_w + lora_a:])).astype(BF16)
    w_pre = w0_ref[...] + _dot(hw, lbw_ref[...])
    w_log = jnp.minimum(w_pre, 0.0) - jnp.log1p(jnp.exp(-jnp.abs(w_pre))) - 0.5
    lw_out[...] = -jnp.exp(w_log)
    alpha = jax.nn.sigmoid(a0_ref[...] + _dot(ha, lba_ref[...]))
    g_out[...] = _dot(hg, lbg_ref[...]).astype(BF16)

    kk = k * kk_ref[...]
    ss = _dot_split(kk * kk, e_ref[...])
    inv = lax.rsqrt(jnp.maximum(ss, 1e-24))
    kn = kk * _dot_split(inv, et_ref[...])
    k_mod = k * (1.0 + (alpha - 1.0) * ka_ref[...])
    rk = _dot_split(r * k_mod * rk_ref[...], e_ref[...])
    bonus_out[...] = (_dot_split(rk, et_ref[...]) * v).astype(BF16)
    r_out[...] = r.astype(BF16)
    k_out[...] = k_mod.astype(BF16)
    v_out[...] = v.astype(BF16)
    kn_out[...] = kn.astype(BF16)
    b_out[...] = (kn * alpha).astype(BF16)


def _rw_pre(h, g, mu, wrkv, la, lbw, lba, lbg, w0, a0, k_k, k_a, r_k, e, et, seq, lora_w, lora_a):
    t, d = h.shape
    tm = min(TM_PRE, seq)
    tps = seq // tm
    rb = tm // V7X_SUBLANES
    row_spec = pl.BlockSpec((tm, d), lambda i: (i, 0))
    act = jax.ShapeDtypeStruct((t, d), BF16)
    vec = _const_spec((1, d))
    return pl.pallas_call(
        functools.partial(_rw_pre_kernel, tiles_per_seq=tps, lora_w=lora_w, lora_a=lora_a),
        out_shape=(act, jax.ShapeDtypeStruct((t, d), F32), act, act, act, act, act, act),
        grid=(t // tm,),
        in_specs=[
            row_spec,
            pl.BlockSpec((V7X_SUBLANES, d), lambda i: (jnp.maximum(i * rb - 1, 0), 0)),
            vec,
            _const_spec(mu.shape),
            _const_spec(wrkv.shape),
            _const_spec(la.shape),
            _const_spec(lbw.shape), _const_spec(lba.shape), _const_spec(lbg.shape),
            vec, vec, vec, vec, vec,
            _const_spec(e.shape), _const_spec(et.shape),
        ],
        out_specs=(row_spec,) * 8,
        compiler_params=_params(("parallel",)),
        name="rw_pre",
    )(h, h, g, mu, wrkv, la, lbw, lba, lbg, w0, a0, k_k, k_a, r_k, e, et)


def _rw_chunk_kernel(r_ref, lw_ref, k_ref, v_ref, kn_ref, b_ref, bonus_ref, g_ref, gng_ref, gnb_ref,
                     o_ref, st_ref):
    ci = pl.program_id(1)
    ln = r_ref.shape[0]
    pair = 2 * RW_DH

    @pl.when(ci == 0)
    def _():
        st_ref[...] = jnp.zeros_like(st_ref)

    trow = lax.broadcasted_iota(jnp.int32, (ln, ln), 0)
    tcol = lax.broadcasted_iota(jnp.int32, (ln, ln), 1)
    lw = lw_ref[...]
    cum = _dot_exact((tcol <= trow).astype(F32), lw)
    cum_last = cum[ln - 1:ln, :]
    g_inc = jnp.exp(cum)
    g_exc = jnp.exp(cum - lw)
    g_inv = jnp.exp(-cum)
    g_rem = jnp.exp(cum_last - cum)
    g_last = jnp.exp(cum_last)
    kn = kn_ref[...].astype(F32)
    bb = b_ref[...].astype(F32)
    kk = k_ref[...].astype(F32)
    a_all = -kn * g_exc
    bt_all = bb * g_inv
    kt_all = kk * g_inv
    rt_all = r_ref[...].astype(F32) * g_inc
    bh_all = bb * g_rem
    kh_all = kk * g_rem
    v_all = v_ref[...].astype(F32)

    lane = lax.broadcasted_iota(jnp.int32, (ln, pair), 1)
    head0 = lane < RW_DH
    srow = lax.broadcasted_iota(jnp.int32, (2 * ln, 2 * ln), 0)
    scol = lax.broadcasted_iota(jnp.int32, (2 * ln, 2 * ln), 1)
    strict = scol < srow
    incl = scol <= srow

    def stack(x):
        return jnp.concatenate([jnp.where(head0, x, 0.0), jnp.where(head0, 0.0, x)], axis=0).astype(BF16)

    for p in range(RW_HEADS // 2):
        sl = slice(p * pair, (p + 1) * pair)
        a_s = stack(a_all[:, sl])
        r_s = stack(rt_all[:, sl])
        b_s = stack(bt_all[:, sl])
        k_s = stack(kt_all[:, sl])
        bh_s = stack(bh_all[:, sl])
        kh_s = stack(kh_all[:, sl])
        v_p = v_all[:, sl]
        v_s = jnp.concatenate([v_p, pltpu.roll(v_p, RW_DH, axis=1)], axis=0)[:, :RW_DH].astype(BF16)
        st = st_ref[p]
        st16 = st.astype(BF16)
        ar_s = jnp.concatenate([a_s, r_s], axis=0)
        m_b = _dot_nt(ar_s, b_s)
        m_k = _dot_nt(ar_s, k_s)
        inter = _dot_nt(ar_s, st16)
        n_mat = jnp.where(strict, m_b[:2 * ln], 0.0)
        ak = jnp.where(strict, m_k[:2 * ln], 0.0).astype(BF16)
        rb = jnp.where(incl, m_b[2 * ln:], 0.0).astype(BF16)
        rk = jnp.where(incl, m_k[2 * ln:], 0.0).astype(BF16)
        u = inter[:2 * ln] + _dot(ak, v_s)
        pw = n_mat
        n_steps = int(math.log2(ln))
        for step in range(n_steps):
            p16 = pw.astype(BF16)
            u = u + _dot(p16, u.astype(BF16))
            if step + 1 < n_steps:
                pw = _dot(p16, p16)
        u16 = u.astype(BF16)
        y_s = inter[2 * ln:] + _dot(rb, u16) + _dot(rk, v_s)
        uv = jnp.concatenate([u16, v_s], axis=0)
        bk = jnp.concatenate([bh_s, kh_s], axis=0)
        st_ref[p] = st * g_last[:, sl] + _dot_tn(uv, bk)
        yc = y_s - jnp.mean(y_s, axis=-1, keepdims=True)
        yn = yc * lax.rsqrt(jnp.mean(yc * yc, axis=-1, keepdims=True) + RW_GN_EPS)
        y_p = jnp.concatenate([yn[:ln], yn[ln:]], axis=1)
        out = (y_p * gng_ref[:, sl] + gnb_ref[:, sl] + bonus_ref[:, sl].astype(F32))
        o_ref[:, sl] = (out * g_ref[:, sl].astype(F32)).astype(BF16)


def _rw_chunk(r, lw, k, v, kn, b, bonus, g, gn_g, gn_b, batch, seq):
    t, d = r.shape
    ln = min(RW_CHUNK, seq)
    nc = seq // ln
    row_spec = pl.BlockSpec((ln, d), lambda bi, ci: (bi * nc + ci, 0))
    vec = pl.BlockSpec((1, d), lambda bi, ci: (0, 0))
    return pl.pallas_call(
        _rw_chunk_kernel,
        out_shape=jax.ShapeDtypeStruct((t, d), BF16),
        grid=(batch, nc),
        in_specs=[row_spec] * 8 + [vec, vec],
        out_specs=row_spec,
        scratch_shapes=[pltpu.VMEM((RW_HEADS // 2, RW_DH, 2 * RW_DH), F32)],
        compiler_params=_params(("parallel", "arbitrary")),
        name="rw_chunk",
    )(r, lw, k, v, kn, b, bonus, g, gn_g, gn_b)


def _rope_tables(seq):
    half = RET_DK // 2
    pos = jnp.arange(seq, dtype=F32)
    inv_freq = 1.0 / (ROPE_BASE ** jnp.linspace(0.0, 1.0, half, dtype=F32))
    ang = pos[:, None] * inv_freq[None, :]
    return jnp.cos(ang), jnp.sin(ang)


def _retention_layer(h, norm_g, w_in, gn, w_out, batch, seq):
    cos, sin = _rope_tables(seq)
    proj = _ret_proj(h, norm_g[None, :], w_in.astype(BF16), cos, sin, seq)
    y = _ret_chunk(proj, gn[None, :].astype(F32), batch, seq)
    return _out_proj(y, w_out.astype(BF16), h)


def _block_diag_tiles(w, tile):
    nb, bs, _ = w.shape
    per = tile // bs
    w = w.reshape(nb // per, per, bs, bs)
    eye = jnp.eye(per, dtype=w.dtype)
    dense = jnp.einsum("npcd,pq->npcqd", w, eye)
    return dense.reshape(nb // per, tile, tile)


def _mlstm_layer(h, norm_g, w_in, conv_w, conv_b, wq, wk, wv, w_gate, b_gate, gn, skip, w_out,
                 batch, seq):
    proj = _ml_proj(h, norm_g[None, :], w_in.astype(BF16), seq)
    tile = 2 * V7X_LANES
    mq = _block_diag_tiles(wq, tile).astype(BF16)
    mk = _block_diag_tiles(wk, tile).astype(BF16)
    mv = _block_diag_tiles(wv, tile).astype(BF16)
    n_gate = w_gate.shape[1]
    wgt = jnp.pad(w_gate.reshape(3, ML_INNER, n_gate), ((0, 0), (0, 0), (0, V7X_LANES - n_gate))).astype(BF16)
    bg = jnp.pad(b_gate, (0, V7X_LANES - n_gate))[None, :].astype(F32)
    c, q, k, v, gates = _ml_pre(proj, conv_w, conv_b[None, :], mq, mk, mv, wgt, bg, seq)
    y = _ml_chunk(q, k, v, gates, c, proj, gn[None, :], skip[None, :], batch, seq)
    return _out_proj(y, w_out.astype(BF16), h)


def _pad_to(x, axis, size):
    pad = [(0, 0)] * x.ndim
    pad[axis] = (0, size - x.shape[axis])
    return jnp.pad(x, pad)


def _rwkv_layer(h, norm_g, mu, w_rkv, w0, w_la, w_lb, a0, a_la, a_lb, g_la, g_lb, k_k, k_a, r_k,
                gn_g, gn_b, w_out, batch, seq):
    d = h.shape[1]
    lw_ = -(-w_la.shape[1] // V7X_LANES) * V7X_LANES
    la_ = -(-a_la.shape[1] // V7X_LANES) * V7X_LANES
    lg_ = -(-g_la.shape[1] // V7X_LANES) * V7X_LANES
    la = jnp.concatenate([_pad_to(w_la, 1, lw_), _pad_to(a_la, 1, la_), _pad_to(g_la, 1, lg_)],
                         axis=1).astype(BF16)
    lbw = _pad_to(w_lb, 0, lw_).astype(BF16)
    lba = _pad_to(a_lb, 0, la_).astype(BF16)
    lbg = _pad_to(g_lb, 0, lg_).astype(BF16)
    head = jnp.arange(d) // RW_DH
    e = (head[:, None] == jnp.arange(V7X_LANES)[None, :]).astype(BF16)
    et = e.T
    outs = _rw_pre(h, norm_g[None, :], mu, w_rkv.astype(BF16), la, lbw, lba, lbg,
                   w0[None, :], a0[None, :], k_k[None, :], k_a[None, :], r_k.reshape(1, d),
                   e, et, seq, lw_, la_)
    r, lw, k, v, kn, b, bonus, g = outs
    y = _rw_chunk(r, lw, k, v, kn, b, bonus, g, gn_g[None, :], gn_b[None, :], batch, seq)
    return _out_proj(y, w_out.astype(BF16), h)


def _ffn_layer(h, norm_g, w_gu, w_down, norm_final, final_norm):
    wg = w_gu[:, :D_FF].astype(BF16)
    wu = w_gu[:, D_FF:].astype(BF16)
    return _ffn(h, norm_g[None, :], wg, wu, w_down.astype(BF16), norm_final[None, :], final_norm)


def kernel(x, norm_mix, norm_ffn, norm_final, ret_w_in, ret_gn, ret_w_out, ml_w_in, ml_conv_w, ml_conv_b, ml_wq, ml_wk, ml_wv, ml_w_gate, ml_b_gate, ml_gn, ml_skip, ml_w_out, rw_mu, rw_w_rkv, rw_w0, rw_w_lora_a, rw_w_lora_b, rw_a0, rw_a_lora_a, rw_a_lora_b, rw_g_lora_a, rw_g_lora_b, rw_k_k, rw_k_a, rw_r_k, rw_gn_g, rw_gn_b, rw_w_out, ffn_w_gu, ffn_w_down):
    batch, seq, d = x.shape
    depth = norm_mix.shape[0]
    h = x.reshape(batch * seq, d)
    for i in range(depth):
        kind, j = i % 3, i // 3
        if kind == 0:
            h = _retention_layer(h, norm_mix[i], ret_w_in[j], ret_gn[j], ret_w_out[j], batch, seq)
        elif kind == 1:
            h = _mlstm_layer(h, norm_mix[i], ml_w_in[j], ml_conv_w[j], ml_conv_b[j], ml_wq[j], ml_wk[j],
                             ml_wv[j], ml_w_gate[j], ml_b_gate[j], ml_gn[j], ml_skip[j], ml_w_out[j],
                             batch, seq)
        else:
            h = _rwkv_layer(h, norm_mix[i], rw_mu[j], rw_w_rkv[j], rw_w0[j], rw_w_lora_a[j],
                            rw_w_lora_b[j], rw_a0[j], rw_a_lora_a[j], rw_a_lora_b[j], rw_g_lora_a[j],
                            rw_g_lora_b[j], rw_k_k[j], rw_k_a[j], rw_r_k[j], rw_gn_g[j], rw_gn_b[j],
                            rw_w_out[j], batch, seq)
        h = _ffn_layer(h, norm_ffn[i], ffn_w_gu[i], ffn_w_down[i], norm_final, i == depth - 1)
    return h.reshape(batch, seq, d)
```

```python
import functools
import math

import jax
import jax.numpy as jnp
from jax import lax
from jax.experimental import pallas as pl
from jax.experimental.pallas import tpu as pltpu

F32 = jnp.float32
BF16 = jnp.bfloat16

D_MODEL = 1024
D_FF = 2816
RMS_EPS = 1e-6

RET_HEADS = 4
RET_DK = 256
RET_DV = 512
RET_EPS = 1e-6
ROPE_BASE = 10000.0

ML_INNER = 2048
ML_HEADS = 4
ML_DH = 512
ML_CONV = 4
ML_EPS = 1e-6

RW_HEADS = 16
RW_DH = 64
RW_GN_EPS = 64e-5

V7X_LANES = 128
V7X_SUBLANES = 8
V7X_VMEM_LIMIT = 56 * 1024 * 1024

TM_PROJ = 1024
TM_OUT = 512
TM_FFN = 512
FFN_CHUNK = 256
TM_PRE = 256
RET_CHUNK = 128
ML_CHUNK = 256
RW_CHUNK = 64


def _params(sem, vmem=V7X_VMEM_LIMIT):
    return pltpu.CompilerParams(dimension_semantics=sem, vmem_limit_bytes=vmem)


def _dot(a, b):
    return jnp.dot(a, b, preferred_element_type=F32)


def _dot_nt(a, b):
    return lax.dot_general(a, b, (((1,), (1,)), ((), ())), preferred_element_type=F32)


def _dot_tn(a, b):
    return lax.dot_general(a, b, (((0,), (0,)), ((), ())), preferred_element_type=F32)


def _dot_exact(a, b):
    hi = b.astype(BF16)
    lo = (b - hi.astype(F32)).astype(BF16)
    a16 = a.astype(BF16)
    return _dot(a16, hi) + _dot(a16, lo)


def _rms(x, g):
    ms = jnp.mean(x * x, axis=-1, keepdims=True)
    return x * lax.rsqrt(ms + RMS_EPS) * g


def _silu(x):
    return x * jax.nn.sigmoid(x)


def _const_spec(shape):
    nd = len(shape)
    return pl.BlockSpec(shape, lambda *_: (0,) * nd)


def _ret_proj_kernel(h_ref, g_ref, w_ref, cos_ref, sin_ref, o_ref, xn_ref, *, tn):
    j = pl.program_id(1)

    @pl.when(j == 0)
    def _():
        xn_ref[...] = _rms(h_ref[...], g_ref[...]).astype(BF16)

    acc = _dot(xn_ref[...], w_ref[...])
    n_qk = 2 * RET_HEADS * RET_DK // tn
    n_q = RET_HEADS * RET_DK // tn
    n_v = RET_HEADS * RET_DV // tn
    half = RET_DK // 2

    @pl.when(j < n_qk)
    def _():
        cos = cos_ref[...]
        sin = sin_ref[...]
        scale = jnp.where(j >= n_q, RET_DK ** -0.5, 1.0).astype(F32)
        for s in range(tn // RET_DK):
            t1 = acc[:, s * RET_DK:s * RET_DK + half]
            t2 = acc[:, s * RET_DK + half:(s + 1) * RET_DK]
            o_ref[:, s * RET_DK:s * RET_DK + half] = ((t1 * cos - t2 * sin) * scale).astype(BF16)
            o_ref[:, s * RET_DK + half:(s + 1) * RET_DK] = ((t2 * cos + t1 * sin) * scale).astype(BF16)

    @pl.when(jnp.logical_and(j >= n_qk, j < n_qk + n_v))
    def _():
        o_ref[...] = acc.astype(BF16)

    @pl.when(j >= n_qk + n_v)
    def _():
        o_ref[...] = _silu(acc).astype(BF16)


def _ret_proj(h, g, w, cos, sin, seq):
    t, d = h.shape
    n = w.shape[1]
    tm, tn = min(TM_PROJ, seq), 512
    spt = seq // tm
    return pl.pallas_call(
        functools.partial(_ret_proj_kernel, tn=tn),
        out_shape=jax.ShapeDtypeStruct((t, n), BF16),
        grid=(t // tm, n // tn),
        in_specs=[
            pl.BlockSpec((tm, d), lambda i, j: (i, 0)),
            pl.BlockSpec((1, d), lambda i, j: (0, 0)),
            pl.BlockSpec((d, tn), lambda i, j: (0, j)),
            pl.BlockSpec((tm, RET_DK // 2), lambda i, j: (i % spt, 0)),
            pl.BlockSpec((tm, RET_DK // 2), lambda i, j: (i % spt, 0)),
        ],
        out_specs=pl.BlockSpec((tm, tn), lambda i, j: (i, j)),
        scratch_shapes=[pltpu.VMEM((tm, d), BF16)],
        compiler_params=_params(("parallel", "arbitrary")),
        name="ret_proj",
    )(h, g, w, cos, sin)


def _ml_proj_kernel(h_ref, g_ref, w_ref, o_ref, xn_ref, *, tn):
    j = pl.program_id(1)

    @pl.when(j == 0)
    def _():
        xn_ref[...] = _rms(h_ref[...], g_ref[...]).astype(BF16)

    acc = _dot(xn_ref[...], w_ref[...])
    n_u = ML_INNER // tn

    @pl.when(j < n_u)
    def _():
        o_ref[...] = acc.astype(BF16)

    @pl.when(j >= n_u)
    def _():
        o_ref[...] = _silu(acc).astype(BF16)


def _ml_proj(h, g, w, seq):
    t, d = h.shape
    n = w.shape[1]
    tm, tn = min(TM_PROJ, seq), 512
    return pl.pallas_call(
        functools.partial(_ml_proj_kernel, tn=tn),
        out_shape=jax.ShapeDtypeStruct((t, n), BF16),
        grid=(t // tm, n // tn),
        in_specs=[
            pl.BlockSpec((tm, d), lambda i, j: (i, 0)),
            pl.BlockSpec((1, d), lambda i, j: (0, 0)),
            pl.BlockSpec((d, tn), lambda i, j: (0, j)),
        ],
        out_specs=pl.BlockSpec((tm, tn), lambda i, j: (i, j)),
        scratch_shapes=[pltpu.VMEM((tm, d), BF16)],
        compiler_params=_params(("parallel", "arbitrary")),
        name="ml_proj",
    )(h, g, w)


def _out_proj_kernel(a_ref, w_ref, h_ref, o_ref):
    o_ref[...] = h_ref[...] + _dot(a_ref[...], w_ref[...])


def _out_proj(a, w, h):
    t, k = a.shape
    d = w.shape[1]
    tm = min(TM_OUT, t)
    return pl.pallas_call(
        _out_proj_kernel,
        out_shape=jax.ShapeDtypeStruct((t, d), F32),
        grid=(t // tm,),
        in_specs=[
            pl.BlockSpec((tm, k), lambda i: (i, 0)),
            _const_spec((k, d)),
            pl.BlockSpec((tm, d), lambda i: (i, 0)),
        ],
        out_specs=pl.BlockSpec((tm, d), lambda i: (i, 0)),
        compiler_params=_params(("parallel",)),
        name="out_proj",
    )(a, w, h)


def _ffn_kernel(h_ref, g_ref, wg_ref, wu_ref, wd_ref, gf_ref, o_ref, *, final_norm):
    h = h_ref[...]
    xn = _rms(h, g_ref[...]).astype(BF16)
    acc = h
    for f in range(0, D_FF, FFN_CHUNK):
        gate = _dot(xn, wg_ref[:, f:f + FFN_CHUNK])
        up = _dot(xn, wu_ref[:, f:f + FFN_CHUNK])
        act = (_silu(gate) * up).astype(BF16)
        acc = acc + _dot(act, wd_ref[f:f + FFN_CHUNK, :])
    if final_norm:
        acc = _rms(acc, gf_ref[...])
    o_ref[...] = acc


def _ffn(h, g, wg, wu, wd, gf, final_norm):
    t, d = h.shape
    tm = min(TM_FFN, t)
    return pl.pallas_call(
        functools.partial(_ffn_kernel, final_norm=final_norm),
        out_shape=jax.ShapeDtypeStruct((t, d), F32),
        grid=(t // tm,),
        in_specs=[
            pl.BlockSpec((tm, d), lambda i: (i, 0)),
            _const_spec((1, d)),
            _const_spec((d, D_FF)),
            _const_spec((d, D_FF)),
            _const_spec((D_FF, d)),
            _const_spec((1, d)),
        ],
        out_specs=pl.BlockSpec((tm, d), lambda i: (i, 0)),
        compiler_params=_params(("parallel",)),
        name="ffn",
    )(h, g, wg, wu, wd, gf)


def _ret_log_gamma(head):
    return math.log1p(-2.0 ** (-5.0 - head))


def _ret_chunk_kernel(q_ref, k_ref, v_ref, g_ref, gn_ref, o_ref, st_ref):
    c = pl.program_id(1)
    ln = q_ref.shape[0]

    @pl.when(c == 0)
    def _():
        st_ref[...] = jnp.zeros_like(st_ref)

    row = lax.broadcasted_iota(jnp.int32, (ln, ln), 0)
    col = lax.broadcasted_iota(jnp.int32, (ln, ln), 1)
    rel = (row - col).astype(F32)
    pos = lax.broadcasted_iota(jnp.int32, (ln, 1), 0).astype(F32)
    for hd in range(RET_HEADS):
        lg = _ret_log_gamma(hd)
        q = q_ref[:, hd * RET_DK:(hd + 1) * RET_DK]
        k = k_ref[:, hd * RET_DK:(hd + 1) * RET_DK]
        v = v_ref[:, hd * RET_DV:(hd + 1) * RET_DV]
        decay = jnp.where(rel >= 0, jnp.exp(lg * jnp.maximum(rel, 0.0)), 0.0)
        scores = _dot_nt(q, k) * decay
        st = st_ref[hd]
        out = _dot(scores.astype(BF16), v)
        out = out + _dot(q, st.astype(BF16)) * jnp.exp(lg * (pos + 1.0))
        k_dec = (k.astype(F32) * jnp.exp(lg * (ln - 1.0 - pos))).astype(BF16)
        st_ref[hd] = st * math.exp(lg * ln) + _dot_tn(k_dec, v)
        ms = jnp.mean(out * out, axis=-1, keepdims=True)
        y = out * lax.rsqrt(ms + RET_EPS) * gn_ref[:, hd * RET_DV:(hd + 1) * RET_DV]
        y = y * g_ref[:, hd * RET_DV:(hd + 1) * RET_DV].astype(F32)
        o_ref[:, hd * RET_DV:(hd + 1) * RET_DV] = y.astype(BF16)


def _ret_chunk(proj, gn, batch, seq):
    t = proj.shape[0]
    ln = min(RET_CHUNK, seq)
    nc = seq // ln
    hk = RET_HEADS * RET_DK
    hv = RET_HEADS * RET_DV
    return pl.pallas_call(
        _ret_chunk_kernel,
        out_shape=jax.ShapeDtypeStruct((t, hv), BF16),
        grid=(batch, nc),
        in_specs=[
            pl.BlockSpec((ln, hk), lambda b, c: (b * nc + c, 0)),
            pl.BlockSpec((ln, hk), lambda b, c: (b * nc + c, 1)),
            pl.BlockSpec((ln, hv), lambda b, c: (b * nc + c, 1)),
            pl.BlockSpec((ln, hv), lambda b, c: (b * nc + c, 2)),
            pl.BlockSpec((1, hv), lambda b, c: (0, 0)),
        ],
        out_specs=pl.BlockSpec((ln, hv), lambda b, c: (b * nc + c, 0)),
        scratch_shapes=[pltpu.VMEM((RET_HEADS, RET_DK, RET_DV), F32)],
        compiler_params=_params(("parallel", "arbitrary")),
        name="ret_chunk",
    )(proj, proj, proj, proj, gn)


def _prev_rows(cur, prev8, shift):
    ext = jnp.concatenate([prev8, cur], axis=0)
    return pltpu.roll(ext, shift, axis=0)[V7X_SUBLANES:, :]


def _log_sigmoid(x):
    return jnp.minimum(x, 0.0) - jnp.log1p(jnp.exp(-jnp.abs(x)))


def _ml_pre_kernel(u_ref, up_ref, cw_ref, cb_ref, mq_ref, mk_ref, mv_ref, wgt_ref, bg_ref,
                   c_ref, q_ref, k_ref, v_ref, gt_ref, *, tiles_per_seq):
    i = pl.program_id(0)
    u = u_ref[...].astype(F32)
    first = (i % tiles_per_seq) == 0
    prev = jnp.where(first, 0.0, up_ref[...].astype(F32))
    conv = u * cw_ref[ML_CONV - 1:ML_CONV, :] + cb_ref[...]
    for s in range(1, ML_CONV):
        conv = conv + _prev_rows(u, prev, s) * cw_ref[ML_CONV - 1 - s:ML_CONV - s, :]
    c = _silu(conv)
    c16 = c.astype(BF16)
    u16 = u_ref[...]
    c_ref[...] = c16
    blk = mq_ref.shape[1]
    gates = jnp.zeros((u.shape[0], V7X_LANES), F32) + bg_ref[...]
    for n in range(ML_INNER // blk):
        sl = slice(n * blk, (n + 1) * blk)
        qn = _dot(c16[:, sl], mq_ref[n]).astype(BF16)
        kn = _dot(c16[:, sl], mk_ref[n]).astype(BF16)
        vn = _dot(u16[:, sl], mv_ref[n]).astype(BF16)
        q_ref[:, sl] = qn
        k_ref[:, sl] = (kn.astype(F32) * (ML_DH ** -0.5)).astype(BF16)
        v_ref[:, sl] = vn
        gates = gates + _dot(qn, wgt_ref[0, sl, :]) + _dot(kn, wgt_ref[1, sl, :]) + _dot(vn, wgt_ref[2, sl, :])
    lane = lax.broadcasted_iota(jnp.int32, gates.shape, 1)
    gt_ref[...] = jnp.where(jnp.logical_and(lane >= ML_HEADS, lane < 2 * ML_HEADS),
                            _log_sigmoid(gates), gates)


def _ml_pre(proj, conv_w, conv_b, mq, mk, mv, wgt, bg, seq):
    t = proj.shape[0]
    tm = min(TM_PRE, seq)
    tps = seq // tm
    rb = tm // V7X_SUBLANES
    blk = mq.shape[1]
    act = jax.ShapeDtypeStruct((t, ML_INNER), BF16)
    row_spec = pl.BlockSpec((tm, ML_INNER), lambda i: (i, 0))
    return pl.pallas_call(
        functools.partial(_ml_pre_kernel, tiles_per_seq=tps),
        out_shape=(act, act, act, act, jax.ShapeDtypeStruct((t, V7X_LANES), F32)),
        grid=(t // tm,),
        in_specs=[
            row_spec,
            pl.BlockSpec((V7X_SUBLANES, ML_INNER), lambda i: (jnp.maximum(i * rb - 1, 0), 0)),
            _const_spec((ML_CONV, ML_INNER)),
            _const_spec((1, ML_INNER)),
            _const_spec(mq.shape), _const_spec(mk.shape), _const_spec(mv.shape),
            _const_spec(wgt.shape),
            _const_spec((1, V7X_LANES)),
        ],
        out_specs=(row_spec, row_spec, row_spec, row_spec,
                   pl.BlockSpec((tm, V7X_LANES), lambda i: (i, 0))),
        compiler_params=_params(("parallel",)),
        name="ml_pre",
    )(proj, proj, conv_w, conv_b, mq, mk, mv, wgt, bg)


def _ml_chunk_kernel(q_ref, k_ref, v_ref, gt_ref, c_ref, z_ref, gn_ref, sk_ref, o_ref,
                     cst_ref, nst_ref, mst_ref):
    ci = pl.program_id(1)
    ln = q_ref.shape[0]

    @pl.when(ci == 0)
    def _():
        cst_ref[...] = jnp.zeros_like(cst_ref)
        nst_ref[...] = jnp.zeros_like(nst_ref)
        mst_ref[...] = jnp.zeros_like(mst_ref)

    row = lax.broadcasted_iota(jnp.int32, (ln, ln), 0)
    col = lax.broadcasted_iota(jnp.int32, (ln, ln), 1)
    causal = col <= row
    gates = gt_ref[...]
    cum = _dot_exact(causal.astype(F32), gates)
    gates_t = gates.T
    cum_t = cum.T
    for hd in range(ML_HEADS):
        sl = slice(hd * ML_DH, (hd + 1) * ML_DH)
        q = q_ref[:, sl]
        k = k_ref[:, sl]
        v = v_ref[:, sl]
        i_col = gates[:, hd:hd + 1]
        i_row = gates_t[hd:hd + 1, :]
        b_col = cum[:, ML_HEADS + hd:ML_HEADS + hd + 1]
        b_row = cum_t[ML_HEADS + hd:ML_HEADS + hd + 1, :]
        m_prev = mst_ref[hd:hd + 1, 0:1]
        log_d = jnp.where(causal, b_col - b_row + i_row, -jnp.inf)
        log_inter = b_col + m_prev
        m_row = jnp.maximum(log_inter, jnp.max(log_d, axis=-1, keepdims=True))
        d_mat = jnp.exp(log_d - m_row)
        w_inter = jnp.exp(log_inter - m_row)
        s = _dot_nt(q, k) * d_mat
        cst = cst_ref[hd]
        n_row = nst_ref[hd:hd + 1, :]
        num = _dot(s.astype(BF16), v) + w_inter * _dot(q, cst.astype(BF16))
        den = jnp.sum(s, axis=-1, keepdims=True) + w_inter * jnp.sum(
            q.astype(F32) * n_row, axis=-1, keepdims=True)
        hh = num / jnp.maximum(jnp.abs(den), jnp.exp(-m_row))
        b_last = b_col[ln - 1:ln, :]
        log_w = b_last - b_col + i_col
        m_new = jnp.maximum(b_last + m_prev, jnp.max(log_w, axis=0, keepdims=True))
        kw = k.astype(F32) * jnp.exp(log_w - m_new)
        carry = jnp.exp(b_last + m_prev - m_new)
        cst_ref[hd] = carry * cst + _dot_tn(kw.astype(BF16), v)
        nst_ref[hd:hd + 1, :] = carry * n_row + jnp.sum(kw, axis=0, keepdims=True)
        mst_ref[hd:hd + 1, :] = jnp.broadcast_to(m_new, (1, V7X_LANES))
        hc = hh - jnp.mean(hh, axis=-1, keepdims=True)
        y = hc * lax.rsqrt(jnp.mean(hc * hc, axis=-1, keepdims=True) + ML_EPS) * gn_ref[:, sl]
        y = (y + sk_ref[:, sl] * c_ref[:, sl].astype(F32)) * z_ref[:, sl].astype(F32)
        o_ref[:, sl] = y.astype(BF16)


def _ml_chunk(q, k, v, gates, c, proj, gn, skip, batch, seq):
    t = q.shape[0]
    ln = min(ML_CHUNK, seq)
    nc = seq // ln
    row_spec = pl.BlockSpec((ln, ML_INNER), lambda b, ci: (b * nc + ci, 0))
    return pl.pallas_call(
        _ml_chunk_kernel,
        out_shape=jax.ShapeDtypeStruct((t, ML_INNER), BF16),
        grid=(batch, nc),
        in_specs=[
            row_spec, row_spec, row_spec,
            pl.BlockSpec((ln, V7X_LANES), lambda b, ci: (b * nc + ci, 0)),
            row_spec,
            pl.BlockSpec((ln, ML_INNER), lambda b, ci: (b * nc + ci, 1)),
            pl.BlockSpec((1, ML_INNER), lambda b, ci: (0, 0)),
            pl.BlockSpec((1, ML_INNER), lambda b, ci: (0, 0)),
        ],
        out_specs=row_spec,
        scratch_shapes=[
            pltpu.VMEM((ML_HEADS, ML_DH, ML_DH), F32),
            pltpu.VMEM((V7X_SUBLANES, ML_DH), F32),
            pltpu.VMEM((V7X_SUBLANES, V7X_LANES), F32),
        ],
        compiler_params=_params(("parallel", "arbitrary")),
        name="ml_chunk",
    )(q, k, v, gates, c, proj, gn, skip)


def _dot_split(x, m):
    hi = x.astype(BF16)
    lo = (x - hi.astype(F32)).astype(BF16)
    return _dot(hi, m) + _dot(lo, m)


def _rw_pre_kernel(h_ref, hp_ref, g_ref, mu_ref, wrkv_ref, la_ref, lbw_ref, lba_ref, lbg_ref,
                   w0_ref, a0_ref, kk_ref, ka_ref, rk_ref, e_ref, et_ref,
                   r_out, lw_out, k_out, v_out, kn_out, b_out, bonus_out, g_out, *, tiles_per_seq,
                   lora_w, lora_a):
    i = pl.program_id(0)
    g = g_ref[...]
    xn = _rms(h_ref[...], g)
    first = (i % tiles_per_seq) == 0
    xp8 = jnp.where(first, 0.0, _rms(hp_ref[...], g))
    dx = _prev_rows(xn, xp8, 1) - xn

    def mix(n):
        return (xn + dx * mu_ref[n:n + 1, :]).astype(BF16)

    r = _dot(mix(0), wrkv_ref[0])
    k = _dot(mix(1), wrkv_ref[1])
    v = _dot(mix(2), wrkv_ref[2])
    hw = jnp.tanh(_dot(mix(3), la_ref[:, 0:lora_w])).astype(BF16)
    ha = _dot(mix(4), la_ref[:, lora_w:lora_w + lora_a]).astype(BF16)
    hg = jax.nn.sigmoid(_dot(mix(5), la_ref[:, lora_w + lora_a:])).astype(BF16)
    w_pre = w0_ref[...] + _dot(hw, lbw_ref[...])
    w_log = jnp.minimum(w_pre, 0.0) - jnp.log1p(jnp.exp(-jnp.abs(w_pre))) - 0.5
    lw_out[...] = -jnp.exp(w_log)
    alpha = jax.nn.sigmoid(a0_ref[...] + _dot(ha, lba_ref[...]))
    g_out[...] = _dot(hg, lbg_ref[...]).astype(BF16)

    kk = k * kk_ref[...]
    ss = _dot_split(kk * kk, e_ref[...])
    inv = lax.rsqrt(jnp.maximum(ss, 1e-24))
    kn = kk * _dot_split(inv, et_ref[...])
    k_mod = k * (1.0 + (alpha - 1.0) * ka_ref[...])
    rk = _dot_split(r * k_mod * rk_ref[...], e_ref[...])
    bonus_out[...] = (_dot_split(rk, et_ref[...]) * v).astype(BF16)
    r_out[...] = r.astype(BF16)
    k_out[...] = k_mod.astype(BF16)
    v_out[...] = v.astype(BF16)
    kn_out[...] = kn.astype(BF16)
    b_out[...] = (kn * alpha).astype(BF16)


def _rw_pre(h, g, mu, wrkv, la, lbw, lba, lbg, w0, a0, k_k, k_a, r_k, e, et, seq, lora_w, lora_a):
    t, d = h.shape
    tm = min(TM_PRE, seq)
    tps = seq // tm
    rb = tm // V7X_SUBLANES
    row_spec = pl.BlockSpec((tm, d), lambda i: (i, 0))
    act = jax.ShapeDtypeStruct((t, d), BF16)
    vec = _const_spec((1, d))
    return pl.pallas_call(
        functools.partial(_rw_pre_kernel, tiles_per_seq=tps, lora_w=lora_w, lora_a=lora_a),
        out_shape=(act, jax.ShapeDtypeStruct((t, d), F32), act, act, act, act, act, act),
        grid=(t // tm,),
        in_specs=[
            row_spec,
            pl.BlockSpec((V7X_SUBLANES, d), lambda i: (jnp.maximum(i * rb - 1, 0), 0)),
            vec,
            _const_spec(mu.shape),
            _const_spec(wrkv.shape),
            _const_spec(la.shape),
            _const_spec(lbw.shape), _const_spec(lba.shape), _const_spec(lbg.shape),
            vec, vec, vec, vec, vec,
            _const_spec(e.shape), _const_spec(et.shape),
        ],
        out_specs=(row_spec,) * 8,
        compiler_params=_params(("parallel",)),
        name="rw_pre",
    )(h, h, g, mu, wrkv, la, lbw, lba, lbg, w0, a0, k_k, k_a, r_k, e, et)


def _rw_chunk_kernel(r_ref, lw_ref, k_ref, v_ref, kn_ref, b_ref, bonus_ref, g_ref, gng_ref, gnb_ref,
                     o_ref, st_ref):
    ci = pl.program_id(1)
    ln = r_ref.shape[0]
    pair = 2 * RW_DH

    @pl.when(ci == 0)
    def _():
        st_ref[...] = jnp.zeros_like(st_ref)

    trow = lax.broadcasted_iota(jnp.int32, (ln, ln), 0)
    tcol = lax.broadcasted_iota(jnp.int32, (ln, ln), 1)
    lw = lw_ref[...]
    cum = _dot_exact((tcol <= trow).astype(F32), lw)
    cum_last = cum[ln - 1:ln, :]
    g_inc = jnp.exp(cum)
    g_exc = jnp.exp(cum - lw)
    g_inv = jnp.exp(-cum)
    g_rem = jnp.exp(cum_last - cum)
    g_last = jnp.exp(cum_last)
    kn = kn_ref[...].astype(F32)
    bb = b_ref[...].astype(F32)
    kk = k_ref[...].astype(F32)
    a_all = -kn * g_exc
    bt_all = bb * g_inv
    kt_all = kk * g_inv
    rt_all = r_ref[...].astype(F32) * g_inc
    bh_all = bb * g_rem
    kh_all = kk * g_rem
    v_all = v_ref[...].astype(F32)

    lane = lax.broadcasted_iota(jnp.int32, (ln, pair), 1)
    head0 = lane < RW_DH
    srow = lax.broadcasted_iota(jnp.int32, (2 * ln, 2 * ln), 0)
    scol = lax.broadcasted_iota(jnp.int32, (2 * ln, 2 * ln), 1)
    strict = scol < srow
    incl = scol <= srow

    def stack(x):
        return jnp.concatenate([jnp.where(head0, x, 0.0), jnp.where(head0, 0.0, x)], axis=0).astype(BF16)

    pairs = range(RW_HEADS // 2)
    sls = [slice(p * pair, (p + 1) * pair) for p in pairs]
    ar_s = [jnp.concatenate([stack(a_all[:, sl]), stack(rt_all[:, sl])], axis=0) for sl in sls]
    b_s = [stack(bt_all[:, sl]) for sl in sls]
    k_s = [stack(kt_all[:, sl]) for sl in sls]
    bk = [jnp.concatenate([stack(bh_all[:, sl]), stack(kh_all[:, sl])], axis=0) for sl in sls]
    v_s = []
    for sl in sls:
        v_p = v_all[:, sl]
        v_s.append(jnp.concatenate([v_p, pltpu.roll(v_p, RW_DH, axis=1)], axis=0)[:, :RW_DH].astype(BF16))
    st = [st_ref[p] for p in pairs]
    m_b = [_dot_nt(ar_s[p], b_s[p]) for p in pairs]
    m_k = [_dot_nt(ar_s[p], k_s[p]) for p in pairs]
    inter = [_dot_nt(ar_s[p], st[p].astype(BF16)) for p in pairs]
    pw = [jnp.where(strict, m_b[p][:2 * ln], 0.0) for p in pairs]
    ak = [jnp.where(strict, m_k[p][:2 * ln], 0.0).astype(BF16) for p in pairs]
    rb = [jnp.where(incl, m_b[p][2 * ln:], 0.0).astype(BF16) for p in pairs]
    rk = [jnp.where(incl, m_k[p][2 * ln:], 0.0).astype(BF16) for p in pairs]
    u = [inter[p][:2 * ln] + _dot(ak[p], v_s[p]) for p in pairs]
    n_steps = int(math.log2(ln))
    for step in range(n_steps):
        p16 = [pw[p].astype(BF16) for p in pairs]
        u = [u[p] + _dot(p16[p], u[p].astype(BF16)) for p in pairs]
        if step + 1 < n_steps:
            pw = [_dot(p16[p], p16[p]) for p in pairs]
    u16 = [u[p].astype(BF16) for p in pairs]
    y_s = [inter[p][2 * ln:] + _dot(rb[p], u16[p]) + _dot(rk[p], v_s[p]) for p in pairs]
    for p in pairs:
        sl = sls[p]
        uv = jnp.concatenate([u16[p], v_s[p]], axis=0)
        st_ref[p] = st[p] * g_last[:, sl] + _dot_tn(uv, bk[p])
        yc = y_s[p] - jnp.mean(y_s[p], axis=-1, keepdims=True)
        yn = yc * lax.rsqrt(jnp.mean(yc * yc, axis=-1, keepdims=True) + RW_GN_EPS)
        y_p = jnp.concatenate([yn[:ln], yn[ln:]], axis=1)
        out = (y_p * gng_ref[:, sl] + gnb_ref[:, sl] + bonus_ref[:, sl].astype(F32))
        o_ref[:, sl] = (out * g_ref[:, sl].astype(F32)).astype(BF16)


def _rw_chunk(r, lw, k, v, kn, b, bonus, g, gn_g, gn_b, batch, seq):
    t, d = r.shape
    ln = min(RW_CHUNK, seq)
    nc = seq // ln
    row_spec = pl.BlockSpec((ln, d), lambda bi, ci: (bi * nc + ci, 0))
    vec = pl.BlockSpec((1, d), lambda bi, ci: (0, 0))
    return pl.pallas_call(
        _rw_chunk_kernel,
        out_shape=jax.ShapeDtypeStruct((t, d), BF16),
        grid=(batch, nc),
        in_specs=[row_spec] * 8 + [vec, vec],
        out_specs=row_spec,
        scratch_shapes=[pltpu.VMEM((RW_HEADS // 2, RW_DH, 2 * RW_DH), F32)],
        compiler_params=_params(("parallel", "arbitrary")),
        name="rw_chunk",
    )(r, lw, k, v, kn, b, bonus, g, gn_g, gn_b)


def _rope_tables(seq):
    half = RET_DK // 2
    pos = jnp.arange(seq, dtype=F32)
    inv_freq = 1.0 / (ROPE_BASE ** jnp.linspace(0.0, 1.0, half, dtype=F32))
    ang = pos[:, None] * inv_freq[None, :]
    return jnp.cos(ang), jnp.sin(ang)


def _retention_layer(h, norm_g, w_in, gn, w_out, batch, seq):
    cos, sin = _rope_tables(seq)
    proj = _ret_proj(h, norm_g[None, :], w_in.astype(BF16), cos, sin, seq)
    y = _ret_chunk(proj, gn[None, :].astype(F32), batch, seq)
    return _out_proj(y, w_out.astype(BF16), h)


def _block_diag_tiles(w, tile):
    nb, bs, _ = w.shape
    per = tile // bs
    w = w.reshape(nb // per, per, bs, bs)
    eye = jnp.eye(per, dtype=w.dtype)
    dense = jnp.einsum("npcd,pq->npcqd", w, eye)
    return dense.reshape(nb // per, tile, tile)


def _mlstm_layer(h, norm_g, w_in, conv_w, conv_b, wq, wk, wv, w_gate, b_gate, gn, skip, w_out,
                 batch, seq):
    proj = _ml_proj(h, norm_g[None, :], w_in.astype(BF16), seq)
    tile = 2 * V7X_LANES
    mq = _block_diag_tiles(wq, tile).astype(BF16)
    mk = _block_diag_tiles(wk, tile).astype(BF16)
    mv = _block_diag_tiles(wv, tile).astype(BF16)
    n_gate = w_gate.shape[1]
    wgt = jnp.pad(w_gate.reshape(3, ML_INNER, n_gate), ((0, 0), (0, 0), (0, V7X_LANES - n_gate))).astype(BF16)
    bg = jnp.pad(b_gate, (0, V7X_LANES - n_gate))[None, :].astype(F32)
    c, q, k, v, gates = _ml_pre(proj, conv_w, conv_b[None, :], mq, mk, mv, wgt, bg, seq)
    y = _ml_chunk(q, k, v, gates, c, proj, gn[None, :], skip[None, :], batch, seq)
    return _out_proj(y, w_out.astype(BF16), h)


def _pad_to(x, axis, size):
    pad = [(0, 0)] * x.ndim
    pad[axis] = (0, size - x.shape[axis])
    return jnp.pad(x, pad)


def _rwkv_layer(h, norm_g, mu, w_rkv, w0, w_la, w_lb, a0, a_la, a_lb, g_la, g_lb, k_k, k_a, r_k,
                gn_g, gn_b, w_out, batch, seq):
    d = h.shape[1]
    lw_ = -(-w_la.shape[1] // V7X_LANES) * V7X_LANES
    la_ = -(-a_la.shape[1] // V7X_LANES) * V7X_LANES
    lg_ = -(-g_la.shape[1] // V7X_LANES) * V7X_LANES
    la = jnp.concatenate([_pad_to(w_la, 1, lw_), _pad_to(a_la, 1, la_), _pad_to(g_la, 1, lg_)],
                         axis=1).astype(BF16)
    lbw = _pad_to(w_lb, 0, lw_).astype(BF16)
    lba = _pad_to(a_lb, 0, la_).astype(BF16)
    lbg = _pad_to(g_lb, 0, lg_).astype(BF16)
    head = jnp.arange(d) // RW_DH
    e = (head[:, None] == jnp.arange(V7X_LANES)[None, :]).astype(BF16)
    et = e.T
    outs = _rw_pre(h, norm_g[None, :], mu, w_rkv.astype(BF16), la, lbw, lba, lbg,
                   w0[None, :], a0[None, :], k_k[None, :], k_a[None, :], r_k.reshape(1, d),
                   e, et, seq, lw_, la_)
    r, lw, k, v, kn, b, bonus, g = outs
    y = _rw_chunk(r, lw, k, v, kn, b, bonus, g, gn_g[None, :], gn_b[None, :], batch, seq)
    return _out_proj(y, w_out.astype(BF16), h)


def _ffn_layer(h, norm_g, w_gu, w_down, norm_final, final_norm):
    wg = w_gu[:, :D_FF].astype(BF16)
    wu = w_gu[:, D_FF:].astype(BF16)
    return _ffn(h, norm_g[None, :], wg, wu, w_down.astype(BF16), norm_final[None, :], final_norm)


def kernel(x, norm_mix, norm_ffn, norm_final, ret_w_in, ret_gn, ret_w_out, ml_w_in, ml_conv_w, ml_conv_b, ml_wq, ml_wk, ml_wv, ml_w_gate, ml_b_gate, ml_gn, ml_skip, ml_w_out, rw_mu, rw_w_rkv, rw_w0, rw_w_lora_a, rw_w_lora_b, rw_a0, rw_a_lora_a, rw_a_lora_b, rw_g_lora_a, rw_g_lora_b, rw_k_k, rw_k_a, rw_r_k, rw_gn_g, rw_gn_b, rw_w_out, ffn_w_gu, ffn_w_down):
    batch, seq, d = x.shape
    depth = norm_mix.shape[0]
    h = x.reshape(batch * seq, d)
    for i in range(depth):
        kind, j = i % 3, i // 3
        if kind == 0:
            h = _retention_layer(h, norm_mix[i], ret_w_in[j], ret_gn[j], ret_w_out[j], batch, seq)
        elif kind == 1:
            h = _mlstm_layer(h, norm_mix[i], ml_w_in[j], ml_conv_w[j], ml_conv_b[j], ml_wq[j], ml_wk[j],
                             ml_wv[j], ml_w_gate[j], ml_b_gate[j], ml_gn[j], ml_skip[j], ml_w_out[j],
                             batch, seq)
        else:
            h = _rwkv_layer(h, norm_mix[i], rw_mu[j], rw_w_rkv[j], rw_w0[j], rw_w_lora_a[j],
                            rw_w_lora_b[j], rw_a0[j], rw_a_lora_a[j], rw_a_lora_b[j], rw_g_lora_a[j],
                            rw_g_lora_b[j], rw_k_k[j], rw_k_a[j], rw_r_k[j], rw_gn_g[j], rw_gn_b[j],
                            rw_w_out[j], batch, seq)
        h = _ffn_layer(h, norm_ffn[i], ffn_w_gu[i], ffn_w_down[i], norm_final, i == depth - 1)
    return h.reshape(batch, seq, d)
```

```python
import functools
import math

import jax
import jax.numpy as jnp
from jax import lax
from jax.experimental import pallas as pl
from jax.experimental.pallas import tpu as pltpu

F32 = jnp.float32
BF16 = jnp.bfloat16

D_MODEL = 1024
D_FF = 2816
RMS_EPS = 1e-6

RET_HEADS = 4
RET_DK = 256
RET_DV = 512
RET_EPS = 1e-6
ROPE_BASE = 10000.0

ML_INNER = 2048
ML_HEADS = 4
ML_DH = 512
ML_CONV = 4
ML_EPS = 1e-6

RW_HEADS = 16
RW_DH = 64
RW_GN_EPS = 64e-5

V7X_LANES = 128
V7X_SUBLANES = 8
V7X_VMEM_LIMIT = 56 * 1024 * 1024

TM_PROJ = 512
PROJ_CHUNK = 512
TM_OUT = 512
TM_FFN = 512
FFN_CHUNK = 256
TM_PRE = 256
RET_CHUNK = 128
ML_CHUNK = 256
RW_CHUNK = 64
RW_BATCH_PER_STEP = 2


def _params(sem, vmem=V7X_VMEM_LIMIT):
    return pltpu.CompilerParams(dimension_semantics=sem, vmem_limit_bytes=vmem)


def _dot(a, b):
    return jnp.dot(a, b, preferred_element_type=F32)


def _dot_nt(a, b):
    return lax.dot_general(a, b, (((1,), (1,)), ((), ())), preferred_element_type=F32)


def _dot_tn(a, b):
    return lax.dot_general(a, b, (((0,), (0,)), ((), ())), preferred_element_type=F32)


def _dot_exact(a, b):
    hi = b.astype(BF16)
    lo = (b - hi.astype(F32)).astype(BF16)
    a16 = a.astype(BF16)
    return _dot(a16, hi) + _dot(a16, lo)


def _rms(x, g):
    ms = jnp.mean(x * x, axis=-1, keepdims=True)
    return x * lax.rsqrt(ms + RMS_EPS) * g


def _silu(x):
    return x * jax.nn.sigmoid(x)


def _const_spec(shape):
    nd = len(shape)
    return pl.BlockSpec(shape, lambda *_: (0,) * nd)


def _ret_proj_kernel(h_ref, g_ref, w_ref, cos_ref, sin_ref, o_ref):
    xn = _rms(h_ref[...], g_ref[...]).astype(BF16)
    cos = cos_ref[...]
    sin = sin_ref[...]
    half = RET_DK // 2
    hk = RET_HEADS * RET_DK
    hv = RET_HEADS * RET_DV
    for hd in range(2 * RET_HEADS):
        c0 = hd * RET_DK
        acc = _dot(xn, w_ref[:, c0:c0 + RET_DK])
        scale = RET_DK ** -0.5 if hd >= RET_HEADS else 1.0
        t1 = acc[:, :half]
        t2 = acc[:, half:]
        o_ref[:, c0:c0 + half] = ((t1 * cos - t2 * sin) * scale).astype(BF16)
        o_ref[:, c0 + half:c0 + RET_DK] = ((t2 * cos + t1 * sin) * scale).astype(BF16)
    for c0 in range(2 * hk, 2 * hk + hv, PROJ_CHUNK):
        o_ref[:, c0:c0 + PROJ_CHUNK] = _dot(xn, w_ref[:, c0:c0 + PROJ_CHUNK]).astype(BF16)
    for c0 in range(2 * hk + hv, 2 * hk + 2 * hv, PROJ_CHUNK):
        o_ref[:, c0:c0 + PROJ_CHUNK] = _silu(_dot(xn, w_ref[:, c0:c0 + PROJ_CHUNK])).astype(BF16)


def _ret_proj(h, g, w, cos, sin, seq):
    t, d = h.shape
    n = w.shape[1]
    tm = min(TM_PROJ, seq)
    spt = seq // tm
    return pl.pallas_call(
        _ret_proj_kernel,
        out_shape=jax.ShapeDtypeStruct((t, n), BF16),
        grid=(t // tm,),
        in_specs=[
            pl.BlockSpec((tm, d), lambda i: (i, 0)),
            _const_spec((1, d)),
            _const_spec((d, n)),
            pl.BlockSpec((tm, RET_DK // 2), lambda i: (i % spt, 0)),
            pl.BlockSpec((tm, RET_DK // 2), lambda i: (i % spt, 0)),
        ],
        out_specs=pl.BlockSpec((tm, n), lambda i: (i, 0)),
        compiler_params=_params(("parallel",)),
        name="ret_proj",
    )(h, g, w, cos, sin)


def _ml_proj_kernel(h_ref, g_ref, w_ref, o_ref):
    xn = _rms(h_ref[...], g_ref[...]).astype(BF16)
    for c0 in range(0, ML_INNER, PROJ_CHUNK):
        o_ref[:, c0:c0 + PROJ_CHUNK] = _dot(xn, w_ref[:, c0:c0 + PROJ_CHUNK]).astype(BF16)
    for c0 in range(ML_INNER, 2 * ML_INNER, PROJ_CHUNK):
        o_ref[:, c0:c0 + PROJ_CHUNK] = _silu(_dot(xn, w_ref[:, c0:c0 + PROJ_CHUNK])).astype(BF16)


def _ml_proj(h, g, w, seq):
    t, d = h.shape
    n = w.shape[1]
    tm = min(TM_PROJ, seq)
    return pl.pallas_call(
        _ml_proj_kernel,
        out_shape=jax.ShapeDtypeStruct((t, n), BF16),
        grid=(t // tm,),
        in_specs=[
            pl.BlockSpec((tm, d), lambda i: (i, 0)),
            _const_spec((1, d)),
            _const_spec((d, n)),
        ],
        out_specs=pl.BlockSpec((tm, n), lambda i: (i, 0)),
        compiler_params=_params(("parallel",)),
        name="ml_proj",
    )(h, g, w)


def _out_proj_kernel(a_ref, w_ref, h_ref, o_ref):
    o_ref[...] = h_ref[...] + _dot(a_ref[...], w_ref[...])


def _out_proj(a, w, h):
    t, k = a.shape
    d = w.shape[1]
    tm = min(TM_OUT, t)
    return pl.pallas_call(
        _out_proj_kernel,
        out_shape=jax.ShapeDtypeStruct((t, d), F32),
        grid=(t // tm,),
        in_specs=[
            pl.BlockSpec((tm, k), lambda i: (i, 0)),
            _const_spec((k, d)),
            pl.BlockSpec((tm, d), lambda i: (i, 0)),
        ],
        out_specs=pl.BlockSpec((tm, d), lambda i: (i, 0)),
        compiler_params=_params(("parallel",)),
        name="out_proj",
    )(a, w, h)


def _ffn_kernel(h_ref, g_ref, wgu_ref, wd_ref, gf_ref, o_ref, *, final_norm):
    h = h_ref[...]
    xn = _rms(h, g_ref[...]).astype(BF16)
    acc = h
    for f in range(0, D_FF, FFN_CHUNK):
        gate = _dot(xn, wgu_ref[:, f:f + FFN_CHUNK])
        up = _dot(xn, wgu_ref[:, D_FF + f:D_FF + f + FFN_CHUNK])
        act = (_silu(gate) * up).astype(BF16)
        acc = acc + _dot(act, wd_ref[f:f + FFN_CHUNK, :])
    if final_norm:
        acc = _rms(acc, gf_ref[...])
    o_ref[...] = acc


def _ffn(h, g, wgu, wd, gf, final_norm):
    t, d = h.shape
    tm = min(TM_FFN, t)
    return pl.pallas_call(
        functools.partial(_ffn_kernel, final_norm=final_norm),
        out_shape=jax.ShapeDtypeStruct((t, d), F32),
        grid=(t // tm,),
        in_specs=[
            pl.BlockSpec((tm, d), lambda i: (i, 0)),
            _const_spec((1, d)),
            _const_spec((d, 2 * D_FF)),
            _const_spec((D_FF, d)),
            _const_spec((1, d)),
        ],
        out_specs=pl.BlockSpec((tm, d), lambda i: (i, 0)),
        compiler_params=_params(("parallel",)),
        name="ffn",
    )(h, g, wgu, wd, gf)


def _ret_log_gamma(head):
    return math.log1p(-2.0 ** (-5.0 - head))


def _ret_chunk_kernel(q_ref, k_ref, v_ref, g_ref, gn_ref, o_ref, st_ref):
    c = pl.program_id(1)
    ln = q_ref.shape[0]

    @pl.when(c == 0)
    def _():
        st_ref[...] = jnp.zeros_like(st_ref)

    row = lax.broadcasted_iota(jnp.int32, (ln, ln), 0)
    col = lax.broadcasted_iota(jnp.int32, (ln, ln), 1)
    rel = (row - col).astype(F32)
    pos = lax.broadcasted_iota(jnp.int32, (ln, 1), 0).astype(F32)
    for hd in range(RET_HEADS):
        lg = _ret_log_gamma(hd)
        q = q_ref[:, hd * RET_DK:(hd + 1) * RET_DK]
        k = k_ref[:, hd * RET_DK:(hd + 1) * RET_DK]
        v = v_ref[:, hd * RET_DV:(hd + 1) * RET_DV]
        decay = jnp.where(rel >= 0, jnp.exp(lg * jnp.maximum(rel, 0.0)), 0.0)
        scores = _dot_nt(q, k) * decay
        st = st_ref[hd]
        out = _dot(scores.astype(BF16), v)
        out = out + _dot(q, st.astype(BF16)) * jnp.exp(lg * (pos + 1.0))
        k_dec = (k.astype(F32) * jnp.exp(lg * (ln - 1.0 - pos))).astype(BF16)
        st_ref[hd] = st * math.exp(lg * ln) + _dot_tn(k_dec, v)
        ms = jnp.mean(out * out, axis=-1, keepdims=True)
        y = out * lax.rsqrt(ms + RET_EPS) * gn_ref[:, hd * RET_DV:(hd + 1) * RET_DV]
        y = y * g_ref[:, hd * RET_DV:(hd + 1) * RET_DV].astype(F32)
        o_ref[:, hd * RET_DV:(hd + 1) * RET_DV] = y.astype(BF16)


def _ret_chunk(proj, gn, batch, seq):
    t = proj.shape[0]
    ln = min(RET_CHUNK, seq)
    nc = seq // ln
    hk = RET_HEADS * RET_DK
    hv = RET_HEADS * RET_DV
    return pl.pallas_call(
        _ret_chunk_kernel,
        out_shape=jax.ShapeDtypeStruct((t, hv), BF16),
        grid=(batch, nc),
        in_specs=[
            pl.BlockSpec((ln, hk), lambda b, c: (b * nc + c, 0)),
            pl.BlockSpec((ln, hk), lambda b, c: (b * nc + c, 1)),
            pl.BlockSpec((ln, hv), lambda b, c: (b * nc + c, 1)),
            pl.BlockSpec((ln, hv), lambda b, c: (b * nc + c, 2)),
            pl.BlockSpec((1, hv), lambda b, c: (0, 0)),
        ],
        out_specs=pl.BlockSpec((ln, hv), lambda b, c: (b * nc + c, 0)),
        scratch_shapes=[pltpu.VMEM((RET_HEADS, RET_DK, RET_DV), F32)],
        compiler_params=_params(("parallel", "arbitrary")),
        name="ret_chunk",
    )(proj, proj, proj, proj, gn)


def _prev_rows(cur, prev8, shift):
    ext = jnp.concatenate([prev8, cur], axis=0)
    return pltpu.roll(ext, shift, axis=0)[V7X_SUBLANES:, :]


def _log_sigmoid(x):
    return jnp.minimum(x, 0.0) - jnp.log1p(jnp.exp(-jnp.abs(x)))


def _ml_pre_kernel(u_ref, up_ref, cw_ref, cb_ref, mq_ref, mk_ref, mv_ref, wgt_ref, bg_ref,
                   c_ref, q_ref, k_ref, v_ref, gt_ref, *, tiles_per_seq):
    i = pl.program_id(0)
    u = u_ref[...].astype(F32)
    first = (i % tiles_per_seq) == 0
    prev = jnp.where(first, 0.0, up_ref[...].astype(F32))
    conv = u * cw_ref[ML_CONV - 1:ML_CONV, :] + cb_ref[...]
    for s in range(1, ML_CONV):
        conv = conv + _prev_rows(u, prev, s) * cw_ref[ML_CONV - 1 - s:ML_CONV - s, :]
    c = _silu(conv)
    c16 = c.astype(BF16)
    u16 = u_ref[...]
    c_ref[...] = c16
    blk = mq_ref.shape[1]
    gates = jnp.zeros((u.shape[0], V7X_LANES), F32) + bg_ref[...]
    for n in range(ML_INNER // blk):
        sl = slice(n * blk, (n + 1) * blk)
        qn = _dot(c16[:, sl], mq_ref[n]).astype(BF16)
        kn = _dot(c16[:, sl], mk_ref[n]).astype(BF16)
        vn = _dot(u16[:, sl], mv_ref[n]).astype(BF16)
        q_ref[:, sl] = qn
        k_ref[:, sl] = (kn.astype(F32) * (ML_DH ** -0.5)).astype(BF16)
        v_ref[:, sl] = vn
        gates = gates + _dot(qn, wgt_ref[0, sl, :]) + _dot(kn, wgt_ref[1, sl, :]) + _dot(vn, wgt_ref[2, sl, :])
    lane = lax.broadcasted_iota(jnp.int32, gates.shape, 1)
    gt_ref[...] = jnp.where(jnp.logical_and(lane >= ML_HEADS, lane < 2 * ML_HEADS),
                            _log_sigmoid(gates), gates)


def _ml_pre(proj, conv_w, conv_b, mq, mk, mv, wgt, bg, seq):
    t = proj.shape[0]
    tm = min(TM_PRE, seq)
    tps = seq // tm
    rb = tm // V7X_SUBLANES
    blk = mq.shape[1]
    act = jax.ShapeDtypeStruct((t, ML_INNER), BF16)
    row_spec = pl.BlockSpec((tm, ML_INNER), lambda i: (i, 0))
    return pl.pallas_call(
        functools.partial(_ml_pre_kernel, tiles_per_seq=tps),
        out_shape=(act, act, act, act, jax.ShapeDtypeStruct((t, V7X_LANES), F32)),
        grid=(t // tm,),
        in_specs=[
            row_spec,
            pl.BlockSpec((V7X_SUBLANES, ML_INNER), lambda i: (jnp.maximum(i * rb - 1, 0), 0)),
            _const_spec((ML_CONV, ML_INNER)),
            _const_spec((1, ML_INNER)),
            _const_spec(mq.shape), _const_spec(mk.shape), _const_spec(mv.shape),
            _const_spec(wgt.shape),
            _const_spec((1, V7X_LANES)),
        ],
        out_specs=(row_spec, row_spec, row_spec, row_spec,
                   pl.BlockSpec((tm, V7X_LANES), lambda i: (i, 0))),
        compiler_params=_params(("parallel",)),
        name="ml_pre",
    )(proj, proj, conv_w, conv_b, mq, mk, mv, wgt, bg)


def _ml_chunk_kernel(q_ref, k_ref, v_ref, gt_ref, c_ref, z_ref, gn_ref, sk_ref, o_ref,
                     cst_ref, nst_ref, mst_ref):
    ci = pl.program_id(1)
    ln = q_ref.shape[0]

    @pl.when(ci == 0)
    def _():
        cst_ref[...] = jnp.zeros_like(cst_ref)
        nst_ref[...] = jnp.zeros_like(nst_ref)
        mst_ref[...] = jnp.zeros_like(mst_ref)

    row = lax.broadcasted_iota(jnp.int32, (ln, ln), 0)
    col = lax.broadcasted_iota(jnp.int32, (ln, ln), 1)
    causal = col <= row
    gates = gt_ref[...]
    cum = _dot_exact(causal.astype(F32), gates)
    gates_t = gates.T
    cum_t = cum.T
    for hd in range(ML_HEADS):
        sl = slice(hd * ML_DH, (hd + 1) * ML_DH)
        q = q_ref[:, sl]
        k = k_ref[:, sl]
        v = v_ref[:, sl]
        i_col = gates[:, hd:hd + 1]
        i_row = gates_t[hd:hd + 1, :]
        b_col = cum[:, ML_HEADS + hd:ML_HEADS + hd + 1]
        b_row = cum_t[ML_HEADS + hd:ML_HEADS + hd + 1, :]
        m_prev = mst_ref[hd:hd + 1, 0:1]
        log_d = jnp.where(causal, b_col - b_row + i_row, -jnp.inf)
        log_inter = b_col + m_prev
        m_row = jnp.maximum(log_inter, jnp.max(log_d, axis=-1, keepdims=True))
        d_mat = jnp.exp(log_d - m_row)
        w_inter = jnp.exp(log_inter - m_row)
        s = _dot_nt(q, k) * d_mat
        cst = cst_ref[hd]
        n_row = nst_ref[hd:hd + 1, :]
        num = _dot(s.astype(BF16), v) + w_inter * _dot(q, cst.astype(BF16))
        den = jnp.sum(s, axis=-1, keepdims=True) + w_inter * jnp.sum(
            q.astype(F32) * n_row, axis=-1, keepdims=True)
        hh = num / jnp.maximum(jnp.abs(den), jnp.exp(-m_row))
        b_last = b_col[ln - 1:ln, :]
        log_w = b_last - b_col + i_col
        m_new = jnp.maximum(b_last + m_prev, jnp.max(log_w, axis=0, keepdims=True))
        kw = k.astype(F32) * jnp.exp(log_w - m_new)
        carry = jnp.exp(b_last + m_prev - m_new)
        cst_ref[hd] = carry * cst + _dot_tn(kw.astype(BF16), v)
        nst_ref[hd:hd + 1, :] = carry * n_row + jnp.sum(kw, axis=0, keepdims=True)
        mst_ref[hd:hd + 1, :] = jnp.broadcast_to(m_new, (1, V7X_LANES))
        hc = hh - jnp.mean(hh, axis=-1, keepdims=True)
        y = hc * lax.rsqrt(jnp.mean(hc * hc, axis=-1, keepdims=True) + ML_EPS) * gn_ref[:, sl]
        y = (y + sk_ref[:, sl] * c_ref[:, sl].astype(F32)) * z_ref[:, sl].astype(F32)
        o_ref[:, sl] = y.astype(BF16)


def _ml_chunk(q, k, v, gates, c, proj, gn, skip, batch, seq):
    t = q.shape[0]
    ln = min(ML_CHUNK, seq)
    nc = seq // ln
    row_spec = pl.BlockSpec((ln, ML_INNER), lambda b, ci: (b * nc + ci, 0))
    return pl.pallas_call(
        _ml_chunk_kernel,
        out_shape=jax.ShapeDtypeStruct((t, ML_INNER), BF16),
        grid=(batch, nc),
        in_specs=[
            row_spec, row_spec, row_spec,
            pl.BlockSpec((ln, V7X_LANES), lambda b, ci: (b * nc + ci, 0)),
            row_spec,
            pl.BlockSpec((ln, ML_INNER), lambda b, ci: (b * nc + ci, 1)),
            pl.BlockSpec((1, ML_INNER), lambda b, ci: (0, 0)),
            pl.BlockSpec((1, ML_INNER), lambda b, ci: (0, 0)),
        ],
        out_specs=row_spec,
        scratch_shapes=[
            pltpu.VMEM((ML_HEADS, ML_DH, ML_DH), F32),
            pltpu.VMEM((V7X_SUBLANES, ML_DH), F32),
            pltpu.VMEM((V7X_SUBLANES, V7X_LANES), F32),
        ],
        compiler_params=_params(("parallel", "arbitrary")),
        name="ml_chunk",
    )(q, k, v, gates, c, proj, gn, skip)


def _dot_split(x, m):
    hi = x.astype(BF16)
    lo = (x - hi.astype(F32)).astype(BF16)
    return _dot(hi, m) + _dot(lo, m)


def _rw_pre_kernel(h_ref, hp_ref, g_ref, mu_ref, wrkv_ref, la_ref, lbw_ref, lba_ref, lbg_ref,
                   w0_ref, a0_ref, kk_ref, ka_ref, rk_ref, e_ref, et_ref,
                   r_out, lw_out, k_out, v_out, kn_out, b_out, bonus_out, g_out, *, tiles_per_seq,
                   lora_w, lora_a):
    i = pl.program_id(0)
    g = g_ref[...]
    xn = _rms(h_ref[...], g)
    first = (i % tiles_per_seq) == 0
    xp8 = jnp.where(first, 0.0, _rms(hp_ref[...], g))
    dx = _prev_rows(xn, xp8, 1) - xn

    def mix(n):
        return (xn + dx * mu_ref[n:n + 1, :]).astype(BF16)

    r = _dot(mix(0), wrkv_ref[0])
    k = _dot(mix(1), wrkv_ref[1])
    v = _dot(mix(2), wrkv_ref[2])
    hw = jnp.tanh(_dot(mix(3), la_ref[:, 0:lora_w])).astype(BF16)
    ha = _dot(mix(4), la_ref[:, lora_w:lora_w + lora_a]).astype(BF16)
    hg = jax.nn.sigmoid(_dot(mix(5), la_ref[:, lora_w + lora_a:])).astype(BF16)
    w_pre = w0_ref[...] + _dot(hw, lbw_ref[...])
    w_log = jnp.minimum(w_pre, 0.0) - jnp.log1p(jnp.exp(-jnp.abs(w_pre))) - 0.5
    lw_out[...] = -jnp.exp(w_log)
    alpha = jax.nn.sigmoid(a0_ref[...] + _dot(ha, lba_ref[...]))
    g_out[...] = _dot(hg, lbg_ref[...]).astype(BF16)

    kk = k * kk_ref[...]
    ss = _dot_split(kk * kk, e_ref[...])
    inv = lax.rsqrt(jnp.maximum(ss, 1e-24))
    kn = kk * _dot_split(inv, et_ref[...])
    k_mod = k * (1.0 + (alpha - 1.0) * ka_ref[...])
    rk = _dot_split(r * k_mod * rk_ref[...], e_ref[...])
    bonus_out[...] = (_dot_split(rk, et_ref[...]) * v).astype(BF16)
    r_out[...] = r.astype(BF16)
    k_out[...] = k_mod.astype(BF16)
    v_out[...] = v.astype(BF16)
    kn_out[...] = kn.astype(BF16)
    b_out[...] = (kn * alpha).astype(BF16)


def _rw_pre(h, g, mu, wrkv, la, lbw, lba, lbg, w0, a0, k_k, k_a, r_k, e, et, seq, lora_w, lora_a):
    t, d = h.shape
    tm = min(TM_PRE, seq)
    tps = seq // tm
    rb = tm // V7X_SUBLANES
    row_spec = pl.BlockSpec((tm, d), lambda i: (i, 0))
    act = jax.ShapeDtypeStruct((t, d), BF16)
    vec = _const_spec((1, d))
    return pl.pallas_call(
        functools.partial(_rw_pre_kernel, tiles_per_seq=tps, lora_w=lora_w, lora_a=lora_a),
        out_shape=(act, jax.ShapeDtypeStruct((t, d), F32), act, act, act, act, act, act),
        grid=(t // tm,),
        in_specs=[
            row_spec,
            pl.BlockSpec((V7X_SUBLANES, d), lambda i: (jnp.maximum(i * rb - 1, 0), 0)),
            vec,
            _const_spec(mu.shape),
            _const_spec(wrkv.shape),
            _const_spec(la.shape),
            _const_spec(lbw.shape), _const_spec(lba.shape), _const_spec(lbg.shape),
            vec, vec, vec, vec, vec,
            _const_spec(e.shape), _const_spec(et.shape),
        ],
        out_specs=(row_spec,) * 8,
        compiler_params=_params(("parallel",)),
        name="rw_pre",
    )(h, h, g, mu, wrkv, la, lbw, lba, lbg, w0, a0, k_k, k_a, r_k, e, et)


def _rw_chunk_kernel(r_ref, lw_ref, k_ref, v_ref, kn_ref, b_ref, bonus_ref, g_ref, gng_ref, gnb_ref,
                     o_ref, st_ref):
    ci = pl.program_id(1)
    nb, ln, _ = r_ref.shape
    pair = 2 * RW_DH

    @pl.when(ci == 0)
    def _():
        st_ref[...] = jnp.zeros_like(st_ref)

    trow = lax.broadcasted_iota(jnp.int32, (ln, ln), 0)
    tcol = lax.broadcasted_iota(jnp.int32, (ln, ln), 1)
    tril = (tcol <= trow).astype(F32)
    lane = lax.broadcasted_iota(jnp.int32, (ln, pair), 1)
    head0 = lane < RW_DH
    srow = lax.broadcasted_iota(jnp.int32, (2 * ln, 2 * ln), 0)
    scol = lax.broadcasted_iota(jnp.int32, (2 * ln, 2 * ln), 1)
    strict_t = srow < scol
    incl_t = srow <= scol

    def stack(x):
        return jnp.concatenate([jnp.where(head0, x, 0.0), jnp.where(head0, 0.0, x)], axis=0).astype(BF16)

    chains = [(bi, p) for bi in range(nb) for p in range(RW_HEADS // 2)]
    pairs = range(len(chains))
    sls = [slice(p * pair, (p + 1) * pair) for _, p in chains]
    ar_s, b_s, k_s, bk, v_t, g_last = [], [], [], [], [], []
    for bi in range(nb):
        lw = lw_ref[bi]
        cum = _dot_exact(tril, lw)
        cum_last = cum[ln - 1:ln, :]
        g_inv = jnp.exp(-cum)
        g_rem = jnp.exp(cum_last - cum)
        kn = kn_ref[bi].astype(F32)
        bb = b_ref[bi].astype(F32)
        kk = k_ref[bi].astype(F32)
        a_all = -kn * jnp.exp(cum - lw)
        rt_all = r_ref[bi].astype(F32) * jnp.exp(cum)
        bt_all = bb * g_inv
        kt_all = kk * g_inv
        bh_all = bb * g_rem
        kh_all = kk * g_rem
        v_all = v_ref[bi].astype(F32)
        gl = jnp.exp(cum_last)
        for p in range(RW_HEADS // 2):
            sl = slice(p * pair, (p + 1) * pair)
            ar_s.append(jnp.concatenate([stack(a_all[:, sl]), stack(rt_all[:, sl])], axis=0))
            b_s.append(stack(bt_all[:, sl]))
            k_s.append(stack(kt_all[:, sl]))
            bk.append(jnp.concatenate([stack(bh_all[:, sl]), stack(kh_all[:, sl])], axis=0))
            vt = v_all[:, sl].T
            v_t.append(jnp.concatenate([vt[:RW_DH], vt[RW_DH:]], axis=1).astype(BF16))
            g_last.append(gl[:, sl])
    st = [st_ref[bi, p] for bi, p in chains]
    m_b = [_dot_nt(b_s[p], ar_s[p]) for p in pairs]
    m_k = [_dot_nt(k_s[p], ar_s[p]) for p in pairs]
    inter = [_dot_nt(st[p].astype(BF16), ar_s[p]) for p in pairs]
    pw = [jnp.where(strict_t, m_b[p][:, :2 * ln], 0.0) for p in pairs]
    ak = [jnp.where(strict_t, m_k[p][:, :2 * ln], 0.0).astype(BF16) for p in pairs]
    rb = [jnp.where(incl_t, m_b[p][:, 2 * ln:], 0.0).astype(BF16) for p in pairs]
    rk = [jnp.where(incl_t, m_k[p][:, 2 * ln:], 0.0).astype(BF16) for p in pairs]
    u = [inter[p][:, :2 * ln] + _dot(v_t[p], ak[p]) for p in pairs]
    n_steps = int(math.log2(ln))
    for step in range(n_steps):
        p16 = [pw[p].astype(BF16) for p in pairs]
        u = [u[p] + _dot(u[p].astype(BF16), p16[p]) for p in pairs]
        if step + 1 < n_steps:
            pw = [_dot(p16[p], p16[p]) for p in pairs]
    u16 = [u[p].astype(BF16) for p in pairs]
    y_t = [inter[p][:, 2 * ln:] + _dot(u16[p], rb[p]) + _dot(v_t[p], rk[p]) for p in pairs]
    for c, (bi, p) in enumerate(chains):
        sl = sls[c]
        uv = jnp.concatenate([u16[c], v_t[c]], axis=1)
        st_ref[bi, p] = st[c] * g_last[c] + _dot(uv, bk[c])
        yc = y_t[c] - jnp.mean(y_t[c], axis=0, keepdims=True)
        yn = (yc * lax.rsqrt(jnp.mean(yc * yc, axis=0, keepdims=True) + RW_GN_EPS)).T
        y_p = jnp.concatenate([yn[:ln], yn[ln:]], axis=1)
        out = (y_p * gng_ref[:, sl] + gnb_ref[:, sl] + bonus_ref[bi, :, sl].astype(F32))
        o_ref[bi, :, sl] = (out * g_ref[bi, :, sl].astype(F32)).astype(BF16)


def _rw_chunk(r, lw, k, v, kn, b, bonus, g, gn_g, gn_b, batch, seq):
    t, d = r.shape
    ln = min(RW_CHUNK, seq)
    nc = seq // ln
    nb = math.gcd(batch, RW_BATCH_PER_STEP)
    row_spec = pl.BlockSpec((nb, ln, d), lambda bi, ci: (bi, ci, 0))
    vec = pl.BlockSpec((1, d), lambda bi, ci: (0, 0))
    acts = [x.reshape(batch, seq, d) for x in (r, lw, k, v, kn, b, bonus, g)]
    out = pl.pallas_call(
        _rw_chunk_kernel,
        out_shape=jax.ShapeDtypeStruct((batch, seq, d), BF16),
        grid=(batch // nb, nc),
        in_specs=[row_spec] * 8 + [vec, vec],
        out_specs=row_spec,
        scratch_shapes=[pltpu.VMEM((nb, RW_HEADS // 2, RW_DH, 2 * RW_DH), F32)],
        compiler_params=_params(("parallel", "arbitrary")),
        name="rw_chunk",
    )(*acts, gn_g, gn_b)
    return out.reshape(t, d)


def _rope_tables(seq):
    half = RET_DK // 2
    pos = jnp.arange(seq, dtype=F32)
    inv_freq = 1.0 / (ROPE_BASE ** jnp.linspace(0.0, 1.0, half, dtype=F32))
    ang = pos[:, None] * inv_freq[None, :]
    return jnp.cos(ang), jnp.sin(ang)


def _retention_layer(h, norm_g, w_in, gn, w_out, batch, seq):
    cos, sin = _rope_tables(seq)
    proj = _ret_proj(h, norm_g[None, :], w_in.astype(BF16), cos, sin, seq)
    y = _ret_chunk(proj, gn[None, :].astype(F32), batch, seq)
    return _out_proj(y, w_out.astype(BF16), h)


def _block_diag_tiles(w, tile):
    nb, bs, _ = w.shape
    per = tile // bs
    w = w.reshape(nb // per, per, bs, bs)
    eye = jnp.eye(per, dtype=w.dtype)
    dense = jnp.einsum("npcd,pq->npcqd", w, eye)
    return dense.reshape(nb // per, tile, tile)


def _mlstm_layer(h, norm_g, w_in, conv_w, conv_b, wq, wk, wv, w_gate, b_gate, gn, skip, w_out,
                 batch, seq):
    proj = _ml_proj(h, norm_g[None, :], w_in.astype(BF16), seq)
    tile = 2 * V7X_LANES
    mq = _block_diag_tiles(wq, tile).astype(BF16)
    mk = _block_diag_tiles(wk, tile).astype(BF16)
    mv = _block_diag_tiles(wv, tile).astype(BF16)
    n_gate = w_gate.shape[1]
    wgt = jnp.pad(w_gate.reshape(3, ML_INNER, n_gate), ((0, 0), (0, 0), (0, V7X_LANES - n_gate))).astype(BF16)
    bg = jnp.pad(b_gate, (0, V7X_LANES - n_gate))[None, :].astype(F32)
    c, q, k, v, gates = _ml_pre(proj, conv_w, conv_b[None, :], mq, mk, mv, wgt, bg, seq)
    y = _ml_chunk(q, k, v, gates, c, proj, gn[None, :], skip[None, :], batch, seq)
    return _out_proj(y, w_out.astype(BF16), h)


def _pad_to(x, axis, size):
    pad = [(0, 0)] * x.ndim
    pad[axis] = (0, size - x.shape[axis])
    return jnp.pad(x, pad)


def _rwkv_layer(h, norm_g, mu, w_rkv, w0, w_la, w_lb, a0, a_la, a_lb, g_la, g_lb, k_k, k_a, r_k,
                gn_g, gn_b, w_out, batch, seq):
    d = h.shape[1]
    lw_ = -(-w_la.shape[1] // V7X_LANES) * V7X_LANES
    la_ = -(-a_la.shape[1] // V7X_LANES) * V7X_LANES
    lg_ = -(-g_la.shape[1] // V7X_LANES) * V7X_LANES
    la = jnp.concatenate([_pad_to(w_la, 1, lw_), _pad_to(a_la, 1, la_), _pad_to(g_la, 1, lg_)],
                         axis=1).astype(BF16)
    lbw = _pad_to(w_lb, 0, lw_).astype(BF16)
    lba = _pad_to(a_lb, 0, la_).astype(BF16)
    lbg = _pad_to(g_lb, 0, lg_).astype(BF16)
    head = jnp.arange(d) // RW_DH
    e = (head[:, None] == jnp.arange(V7X_LANES)[None, :]).astype(BF16)
    et = e.T
    outs = _rw_pre(h, norm_g[None, :], mu, w_rkv.astype(BF16), la, lbw, lba, lbg,
                   w0[None, :], a0[None, :], k_k[None, :], k_a[None, :], r_k.reshape(1, d),
                   e, et, seq, lw_, la_)
    r, lw, k, v, kn, b, bonus, g = outs
    y = _rw_chunk(r, lw, k, v, kn, b, bonus, g, gn_g[None, :], gn_b[None, :], batch, seq)
    return _out_proj(y, w_out.astype(BF16), h)


def _ffn_layer(h, norm_g, w_gu, w_down, norm_final, final_norm):
    return _ffn(h, norm_g[None, :], w_gu.astype(BF16), w_down.astype(BF16), norm_final[None, :], final_norm)


def kernel(x, norm_mix, norm_ffn, norm_final, ret_w_in, ret_gn, ret_w_out, ml_w_in, ml_conv_w, ml_conv_b, ml_wq, ml_wk, ml_wv, ml_w_gate, ml_b_gate, ml_gn, ml_skip, ml_w_out, rw_mu, rw_w_rkv, rw_w0, rw_w_lora_a, rw_w_lora_b, rw_a0, rw_a_lora_a, rw_a_lora_b, rw_g_lora_a, rw_g_lora_b, rw_k_k, rw_k_a, rw_r_k, rw_gn_g, rw_gn_b, rw_w_out, ffn_w_gu, ffn_w_down):
    batch, seq, d = x.shape
    depth = norm_mix.shape[0]
    h = x.reshape(batch * seq, d)
    for i in range(depth):
        kind, j = i % 3, i // 3
        if kind == 0:
            h = _retention_layer(h, norm_mix[i], ret_w_in[j], ret_gn[j], ret_w_out[j], batch, seq)
        elif kind == 1:
            h = _mlstm_layer(h, norm_mix[i], ml_w_in[j], ml_conv_w[j], ml_conv_b[j], ml_wq[j], ml_wk[j],
                             ml_wv[j], ml_w_gate[j], ml_b_gate[j], ml_gn[j], ml_skip[j], ml_w_out[j],
                             batch, seq)
        else:
            h = _rwkv_layer(h, norm_mix[i], rw_mu[j], rw_w_rkv[j], rw_w0[j], rw_w_lora_a[j],
                            rw_w_lora_b[j], rw_a0[j], rw_a_lora_a[j], rw_a_lora_b[j], rw_g_lora_a[j],
                            rw_g_lora_b[j], rw_k_k[j], rw_k_a[j], rw_r_k[j], rw_gn_g[j], rw_gn_b[j],
                            rw_w_out[j], batch, seq)
        h = _ffn_layer(h, norm_ffn[i], ffn_w_gu[i], ffn_w_down[i], norm_final, i == depth - 1)
    return h.reshape(batch, seq, d)
```

```python
import functools
import math

import jax
import jax.numpy as jnp
from jax import lax
from jax.experimental import pallas as pl
from jax.experimental.pallas import tpu as pltpu

F32 = jnp.float32
BF16 = jnp.bfloat16

D_MODEL = 1024
D_FF = 2816
RMS_EPS = 1e-6

RET_HEADS = 4
RET_DK = 256
RET_DV = 512
RET_EPS = 1e-6
ROPE_BASE = 10000.0

ML_INNER = 2048
ML_HEADS = 4
ML_DH = 512
ML_CONV = 4
ML_EPS = 1e-6

RW_HEADS = 16
RW_DH = 64
RW_GN_EPS = 64e-5

V7X_LANES = 128
V7X_SUBLANES = 8
V7X_VMEM_LIMIT = 56 * 1024 * 1024

TM_PROJ = 512
PROJ_CHUNK = 512
TM_FFN = 512
FFN_CHUNK = 256
TM_PRE = 256
RET_CHUNK = 256
ML_CHUNK = 256
RW_CHUNK = 64
RW_BATCH_PER_STEP = 2


def _params(sem, vmem=V7X_VMEM_LIMIT):
    return pltpu.CompilerParams(dimension_semantics=sem, vmem_limit_bytes=vmem)


def _dot(a, b):
    return jnp.dot(a, b, preferred_element_type=F32)


def _dot_nt(a, b):
    return lax.dot_general(a, b, (((1,), (1,)), ((), ())), preferred_element_type=F32)


def _dot_tn(a, b):
    return lax.dot_general(a, b, (((0,), (0,)), ((), ())), preferred_element_type=F32)


def _dot_exact(a, b):
    hi = b.astype(BF16)
    lo = (b - hi.astype(F32)).astype(BF16)
    a16 = a.astype(BF16)
    return _dot(a16, hi) + _dot(a16, lo)


def _rms(x, g):
    ms = jnp.mean(x * x, axis=-1, keepdims=True)
    return x * lax.rsqrt(ms + RMS_EPS) * g


def _silu(x):
    return x * jax.nn.sigmoid(x)


def _const_spec(shape):
    nd = len(shape)
    return pl.BlockSpec(shape, lambda *_: (0,) * nd, pipeline_mode=pl.Buffered(1))


def _layer_spec(shape, layer):
    nd = len(shape)
    return pl.BlockSpec((None,) + tuple(shape), lambda *_: (layer,) + (0,) * nd,
                        pipeline_mode=pl.Buffered(1))


def _ret_proj_kernel(h_ref, g_ref, w_ref, cos_ref, sin_ref, o_ref):
    xn = _rms(h_ref[...], g_ref[...]).astype(BF16)
    cos = cos_ref[...]
    sin = sin_ref[...]
    half = RET_DK // 2
    hk = RET_HEADS * RET_DK
    hv = RET_HEADS * RET_DV
    for hd in range(2 * RET_HEADS):
        c0 = hd * RET_DK
        acc = _dot(xn, w_ref[:, c0:c0 + RET_DK])
        scale = RET_DK ** -0.5 if hd >= RET_HEADS else 1.0
        t1 = acc[:, :half]
        t2 = acc[:, half:]
        o_ref[:, c0:c0 + half] = ((t1 * cos - t2 * sin) * scale).astype(BF16)
        o_ref[:, c0 + half:c0 + RET_DK] = ((t2 * cos + t1 * sin) * scale).astype(BF16)
    for c0 in range(2 * hk, 2 * hk + hv, PROJ_CHUNK):
        o_ref[:, c0:c0 + PROJ_CHUNK] = _dot(xn, w_ref[:, c0:c0 + PROJ_CHUNK]).astype(BF16)
    for c0 in range(2 * hk + hv, 2 * hk + 2 * hv, PROJ_CHUNK):
        o_ref[:, c0:c0 + PROJ_CHUNK] = _silu(_dot(xn, w_ref[:, c0:c0 + PROJ_CHUNK])).astype(BF16)


def _ret_proj(h, g, w, layer, cos, sin, seq):
    t, d = h.shape
    n = w.shape[2]
    tm = min(TM_PROJ, seq)
    spt = seq // tm
    return pl.pallas_call(
        _ret_proj_kernel,
        out_shape=jax.ShapeDtypeStruct((t, n), BF16),
        grid=(t // tm,),
        in_specs=[
            pl.BlockSpec((tm, d), lambda i: (i, 0)),
            _const_spec((1, d)),
            _layer_spec((d, n), layer),
            pl.BlockSpec((tm, RET_DK // 2), lambda i: (i % spt, 0)),
            pl.BlockSpec((tm, RET_DK // 2), lambda i: (i % spt, 0)),
        ],
        out_specs=pl.BlockSpec((tm, n), lambda i: (i, 0)),
        compiler_params=_params(("parallel",)),
        name="ret_proj",
    )(h, g, w, cos, sin)


def _ml_proj_kernel(h_ref, g_ref, w_ref, o_ref):
    xn = _rms(h_ref[...], g_ref[...]).astype(BF16)
    for c0 in range(0, ML_INNER, PROJ_CHUNK):
        o_ref[:, c0:c0 + PROJ_CHUNK] = _dot(xn, w_ref[:, c0:c0 + PROJ_CHUNK]).astype(BF16)
    for c0 in range(ML_INNER, 2 * ML_INNER, PROJ_CHUNK):
        o_ref[:, c0:c0 + PROJ_CHUNK] = _silu(_dot(xn, w_ref[:, c0:c0 + PROJ_CHUNK])).astype(BF16)


def _ml_proj(h, g, w, seq):
    t, d = h.shape
    n = w.shape[1]
    tm = min(TM_PROJ, seq)
    return pl.pallas_call(
        _ml_proj_kernel,
        out_shape=jax.ShapeDtypeStruct((t, n), BF16),
        grid=(t // tm,),
        in_specs=[
            pl.BlockSpec((tm, d), lambda i: (i, 0)),
            _const_spec((1, d)),
            _const_spec((d, n)),
        ],
        out_specs=pl.BlockSpec((tm, n), lambda i: (i, 0)),
        compiler_params=_params(("parallel",)),
        name="ml_proj",
    )(h, g, w)


def _mix_ffn_kernel(y_ref, wo_ref, h_ref, g_ref, wgu_ref, wd_ref, gf_ref, o_ref, *, final_norm):
    h = h_ref[...] + _dot(y_ref[...], wo_ref[...])
    xn = _rms(h, g_ref[...]).astype(BF16)
    acc = h
    for f in range(0, D_FF, FFN_CHUNK):
        gate = _dot(xn, wgu_ref[:, f:f + FFN_CHUNK])
        up = _dot(xn, wgu_ref[:, D_FF + f:D_FF + f + FFN_CHUNK])
        act = (_silu(gate) * up).astype(BF16)
        acc = acc + _dot(act, wd_ref[f:f + FFN_CHUNK, :])
    if final_norm:
        acc = _rms(acc, gf_ref[...])
    o_ref[...] = acc


def _mix_ffn(y, wo, wo_layer, h, g, wgu, wd, ffn_layer, gf, final_norm):
    t, d = h.shape
    k = y.shape[1]
    tm = min(TM_FFN, t)
    return pl.pallas_call(
        functools.partial(_mix_ffn_kernel, final_norm=final_norm),
        out_shape=jax.ShapeDtypeStruct((t, d), F32),
        grid=(t // tm,),
        in_specs=[
            pl.BlockSpec((tm, k), lambda i: (i, 0)),
            _layer_spec((k, d), wo_layer),
            pl.BlockSpec((tm, d), lambda i: (i, 0)),
            _const_spec((1, d)),
            _layer_spec((d, 2 * D_FF), ffn_layer),
            _layer_spec((D_FF, d), ffn_layer),
            _const_spec((1, d)),
        ],
        out_specs=pl.BlockSpec((tm, d), lambda i: (i, 0)),
        compiler_params=_params(("parallel",)),
        name="mix_ffn",
    )(y, wo, h, g, wgu, wd, gf)


def _ret_log_gamma(head):
    return math.log1p(-2.0 ** (-5.0 - head))


def _ret_chunk_kernel(q_ref, k_ref, v_ref, g_ref, gn_ref, o_ref, st_ref):
    c = pl.program_id(1)
    ln = q_ref.shape[0]

    @pl.when(c == 0)
    def _():
        st_ref[...] = jnp.zeros_like(st_ref)

    row = lax.broadcasted_iota(jnp.int32, (ln, ln), 0)
    col = lax.broadcasted_iota(jnp.int32, (ln, ln), 1)
    rel = (row - col).astype(F32)
    pos = lax.broadcasted_iota(jnp.int32, (ln, 1), 0).astype(F32)
    for hd in range(RET_HEADS):
        lg = _ret_log_gamma(hd)
        q = q_ref[:, hd * RET_DK:(hd + 1) * RET_DK]
        k = k_ref[:, hd * RET_DK:(hd + 1) * RET_DK]
        v = v_ref[:, hd * RET_DV:(hd + 1) * RET_DV]
        decay = jnp.where(rel >= 0, jnp.exp(lg * jnp.maximum(rel, 0.0)), 0.0)
        scores = _dot_nt(q, k) * decay
        st = st_ref[hd]
        out = _dot(scores.astype(BF16), v)
        out = out + _dot(q, st.astype(BF16)) * jnp.exp(lg * (pos + 1.0))
        k_dec = (k.astype(F32) * jnp.exp(lg * (ln - 1.0 - pos))).astype(BF16)
        st_ref[hd] = st * math.exp(lg * ln) + _dot_tn(k_dec, v)
        ms = jnp.mean(out * out, axis=-1, keepdims=True)
        y = out * lax.rsqrt(ms + RET_EPS) * gn_ref[:, hd * RET_DV:(hd + 1) * RET_DV]
        y = y * g_ref[:, hd * RET_DV:(hd + 1) * RET_DV].astype(F32)
        o_ref[:, hd * RET_DV:(hd + 1) * RET_DV] = y.astype(BF16)


def _ret_chunk(proj, gn, batch, seq):
    t = proj.shape[0]
    ln = min(RET_CHUNK, seq)
    nc = seq // ln
    hk = RET_HEADS * RET_DK
    hv = RET_HEADS * RET_DV
    return pl.pallas_call(
        _ret_chunk_kernel,
        out_shape=jax.ShapeDtypeStruct((t, hv), BF16),
        grid=(batch, nc),
        in_specs=[
            pl.BlockSpec((ln, hk), lambda b, c: (b * nc + c, 0)),
            pl.BlockSpec((ln, hk), lambda b, c: (b * nc + c, 1)),
            pl.BlockSpec((ln, hv), lambda b, c: (b * nc + c, 1)),
            pl.BlockSpec((ln, hv), lambda b, c: (b * nc + c, 2)),
            pl.BlockSpec((1, hv), lambda b, c: (0, 0)),
        ],
        out_specs=pl.BlockSpec((ln, hv), lambda b, c: (b * nc + c, 0)),
        scratch_shapes=[pltpu.VMEM((RET_HEADS, RET_DK, RET_DV), F32)],
        compiler_params=_params(("parallel", "arbitrary")),
        name="ret_chunk",
    )(proj, proj, proj, proj, gn)


def _prev_rows(cur, prev8, shift):
    ext = jnp.concatenate([prev8, cur], axis=0)
    return pltpu.roll(ext, shift, axis=0)[V7X_SUBLANES:, :]


def _log_sigmoid(x):
    return jnp.minimum(x, 0.0) - jnp.log1p(jnp.exp(-jnp.abs(x)))


def _ml_pre_kernel(u_ref, up_ref, cw_ref, cb_ref, mq_ref, mk_ref, mv_ref, wgt_ref, bg_ref,
                   c_ref, q_ref, k_ref, v_ref, gt_ref, *, tiles_per_seq):
    i = pl.program_id(0)
    u = u_ref[...].astype(F32)
    first = (i % tiles_per_seq) == 0
    prev = jnp.where(first, 0.0, up_ref[...].astype(F32))
    conv = u * cw_ref[ML_CONV - 1:ML_CONV, :] + cb_ref[...]
    for s in range(1, ML_CONV):
        conv = conv + _prev_rows(u, prev, s) * cw_ref[ML_CONV - 1 - s:ML_CONV - s, :]
    c = _silu(conv)
    c16 = c.astype(BF16)
    u16 = u_ref[...]
    c_ref[...] = c16
    blk = mq_ref.shape[1]
    gates = jnp.zeros((u.shape[0], V7X_LANES), F32) + bg_ref[...]
    for n in range(ML_INNER // blk):
        sl = slice(n * blk, (n + 1) * blk)
        qn = _dot(c16[:, sl], mq_ref[n]).astype(BF16)
        kn = _dot(c16[:, sl], mk_ref[n]).astype(BF16)
        vn = _dot(u16[:, sl], mv_ref[n]).astype(BF16)
        q_ref[:, sl] = qn
        k_ref[:, sl] = (kn.astype(F32) * (ML_DH ** -0.5)).astype(BF16)
        v_ref[:, sl] = vn
        gates = gates + _dot(qn, wgt_ref[0, sl, :]) + _dot(kn, wgt_ref[1, sl, :]) + _dot(vn, wgt_ref[2, sl, :])
    lane = lax.broadcasted_iota(jnp.int32, gates.shape, 1)
    gt_ref[...] = jnp.where(jnp.logical_and(lane >= ML_HEADS, lane < 2 * ML_HEADS),
                            _log_sigmoid(gates), gates)


def _ml_pre(proj, conv_w, conv_b, mq, mk, mv, wgt, bg, seq):
    t = proj.shape[0]
    tm = min(TM_PRE, seq)
    tps = seq // tm
    rb = tm // V7X_SUBLANES
    blk = mq.shape[1]
    act = jax.ShapeDtypeStruct((t, ML_INNER), BF16)
    row_spec = pl.BlockSpec((tm, ML_INNER), lambda i: (i, 0))
    return pl.pallas_call(
        functools.partial(_ml_pre_kernel, tiles_per_seq=tps),
        out_shape=(act, act, act, act, jax.ShapeDtypeStruct((t, V7X_LANES), F32)),
        grid=(t // tm,),
        in_specs=[
            row_spec,
            pl.BlockSpec((V7X_SUBLANES, ML_INNER), lambda i: (jnp.maximum(i * rb - 1, 0), 0)),
            _const_spec((ML_CONV, ML_INNER)),
            _const_spec((1, ML_INNER)),
            _const_spec(mq.shape), _const_spec(mk.shape), _const_spec(mv.shape),
            _const_spec(wgt.shape),
            _const_spec((1, V7X_LANES)),
        ],
        out_specs=(row_spec, row_spec, row_spec, row_spec,
                   pl.BlockSpec((tm, V7X_LANES), lambda i: (i, 0))),
        compiler_params=_params(("parallel",)),
        name="ml_pre",
    )(proj, proj, conv_w, conv_b, mq, mk, mv, wgt, bg)


def _ml_chunk_kernel(q_ref, k_ref, v_ref, gt_ref, c_ref, z_ref, gn_ref, sk_ref, o_ref,
                     cst_ref, nst_ref, mst_ref):
    ci = pl.program_id(1)
    ln = q_ref.shape[0]

    @pl.when(ci == 0)
    def _():
        cst_ref[...] = jnp.zeros_like(cst_ref)
        nst_ref[...] = jnp.zeros_like(nst_ref)
        mst_ref[...] = jnp.zeros_like(mst_ref)

    row = lax.broadcasted_iota(jnp.int32, (ln, ln), 0)
    col = lax.broadcasted_iota(jnp.int32, (ln, ln), 1)
    causal = col <= row
    gates = gt_ref[...]
    cum = _dot_exact(causal.astype(F32), gates)
    gates_t = gates.T
    cum_t = cum.T
    for hd in range(ML_HEADS):
        sl = slice(hd * ML_DH, (hd + 1) * ML_DH)
        q = q_ref[:, sl]
        k = k_ref[:, sl]
        v = v_ref[:, sl]
        i_col = gates[:, hd:hd + 1]
        i_row = gates_t[hd:hd + 1, :]
        b_col = cum[:, ML_HEADS + hd:ML_HEADS + hd + 1]
        b_row = cum_t[ML_HEADS + hd:ML_HEADS + hd + 1, :]
        m_prev = mst_ref[hd:hd + 1, 0:1]
        log_d = jnp.where(causal, b_col - b_row + i_row, -jnp.inf)
        log_inter = b_col + m_prev
        m_row = jnp.maximum(log_inter, jnp.max(log_d, axis=-1, keepdims=True))
        d_mat = jnp.exp(log_d - m_row)
        w_inter = jnp.exp(log_inter - m_row)
        s = _dot_nt(q, k) * d_mat
        cst = cst_ref[hd]
        n_row = nst_ref[hd:hd + 1, :]
        num = _dot(s.astype(BF16), v) + w_inter * _dot(q, cst.astype(BF16))
        den = jnp.sum(s, axis=-1, keepdims=True) + w_inter * jnp.sum(
            q.astype(F32) * n_row, axis=-1, keepdims=True)
        hh = num / jnp.maximum(jnp.abs(den), jnp.exp(-m_row))
        b_last = b_col[ln - 1:ln, :]
        log_w = b_last - b_col + i_col
        m_new = jnp.maximum(b_last + m_prev, jnp.max(log_w, axis=0, keepdims=True))
        kw = k.astype(F32) * jnp.exp(log_w - m_new)
        carry = jnp.exp(b_last + m_prev - m_new)
        cst_ref[hd] = carry * cst + _dot_tn(kw.astype(BF16), v)
        nst_ref[hd:hd + 1, :] = carry * n_row + jnp.sum(kw, axis=0, keepdims=True)
        mst_ref[hd:hd + 1, :] = jnp.broadcast_to(m_new, (1, V7X_LANES))
        hc = hh - jnp.mean(hh, axis=-1, keepdims=True)
        y = hc * lax.rsqrt(jnp.mean(hc * hc, axis=-1, keepdims=True) + ML_EPS) * gn_ref[:, sl]
        y = (y + sk_ref[:, sl] * c_ref[:, sl].astype(F32)) * z_ref[:, sl].astype(F32)
        o_ref[:, sl] = y.astype(BF16)


def _ml_chunk(q, k, v, gates, c, proj, gn, skip, batch, seq):
    t = q.shape[0]
    ln = min(ML_CHUNK, seq)
    nc = seq // ln
    row_spec = pl.BlockSpec((ln, ML_INNER), lambda b, ci: (b * nc + ci, 0))
    return pl.pallas_call(
        _ml_chunk_kernel,
        out_shape=jax.ShapeDtypeStruct((t, ML_INNER), BF16),
        grid=(batch, nc),
        in_specs=[
            row_spec, row_spec, row_spec,
            pl.BlockSpec((ln, V7X_LANES), lambda b, ci: (b * nc + ci, 0)),
            row_spec,
            pl.BlockSpec((ln, ML_INNER), lambda b, ci: (b * nc + ci, 1)),
            pl.BlockSpec((1, ML_INNER), lambda b, ci: (0, 0)),
            pl.BlockSpec((1, ML_INNER), lambda b, ci: (0, 0)),
        ],
        out_specs=row_spec,
        scratch_shapes=[
            pltpu.VMEM((ML_HEADS, ML_DH, ML_DH), F32),
            pltpu.VMEM((V7X_SUBLANES, ML_DH), F32),
            pltpu.VMEM((V7X_SUBLANES, V7X_LANES), F32),
        ],
        compiler_params=_params(("parallel", "arbitrary")),
        name="ml_chunk",
    )(q, k, v, gates, c, proj, gn, skip)


def _dot_split(x, m):
    hi = x.astype(BF16)
    lo = (x - hi.astype(F32)).astype(BF16)
    return _dot(hi, m) + _dot(lo, m)


def _rw_pre_kernel(h_ref, hp_ref, g_ref, mu_ref, wrkv_ref, la_ref, lbw_ref, lba_ref, lbg_ref,
                   w0_ref, a0_ref, kk_ref, ka_ref, rk_ref, e_ref, et_ref,
                   r_out, lw_out, k_out, v_out, kn_out, b_out, bonus_out, g_out, *, tiles_per_seq,
                   lora_w, lora_a):
    i = pl.program_id(0)
    g = g_ref[...]
    xn = _rms(h_ref[...], g)
    first = (i % tiles_per_seq) == 0
    xp8 = jnp.where(first, 0.0, _rms(hp_ref[...], g))
    dx = _prev_rows(xn, xp8, 1) - xn

    def mix(n):
        return (xn + dx * mu_ref[n:n + 1, :]).astype(BF16)

    r = _dot(mix(0), wrkv_ref[0])
    k = _dot(mix(1), wrkv_ref[1])
    v = _dot(mix(2), wrkv_ref[2])
    hw = jnp.tanh(_dot(mix(3), la_ref[:, 0:lora_w])).astype(BF16)
    ha = _dot(mix(4), la_ref[:, lora_w:lora_w + lora_a]).astype(BF16)
    hg = jax.nn.sigmoid(_dot(mix(5), la_ref[:, lora_w + lora_a:])).astype(BF16)
    w_pre = w0_ref[...] + _dot(hw, lbw_ref[...])
    w_log = jnp.minimum(w_pre, 0.0) - jnp.log1p(jnp.exp(-jnp.abs(w_pre))) - 0.5
    lw_out[...] = -jnp.exp(w_log)
    alpha = jax.nn.sigmoid(a0_ref[...] + _dot(ha, lba_ref[...]))
    g_out[...] = _dot(hg, lbg_ref[...]).astype(BF16)

    kk = k * kk_ref[...]
    ss = _dot_split(kk * kk, e_ref[...])
    inv = lax.rsqrt(jnp.maximum(ss, 1e-24))
    kn = kk * _dot_split(inv, et_ref[...])
    k_mod = k * (1.0 + (alpha - 1.0) * ka_ref[...])
    rk = _dot_split(r * k_mod * rk_ref[...], e_ref[...])
    bonus_out[...] = (_dot_split(rk, et_ref[...]) * v).astype(BF16)
    r_out[...] = r.astype(BF16)
    k_out[...] = k_mod.astype(BF16)
    v_out[...] = v.astype(BF16)
    kn_out[...] = kn.astype(BF16)
    b_out[...] = (kn * alpha).astype(BF16)


def _rw_pre(h, g, mu, wrkv, la, lbw, lba, lbg, w0, a0, k_k, k_a, r_k, e, et, seq, lora_w, lora_a):
    t, d = h.shape
    tm = min(TM_PRE, seq)
    tps = seq // tm
    rb = tm // V7X_SUBLANES
    row_spec = pl.BlockSpec((tm, d), lambda i: (i, 0))
    act = jax.ShapeDtypeStruct((t, d), BF16)
    vec = _const_spec((1, d))
    return pl.pallas_call(
        functools.partial(_rw_pre_kernel, tiles_per_seq=tps, lora_w=lora_w, lora_a=lora_a),
        out_shape=(act, jax.ShapeDtypeStruct((t, d), F32), act, act, act, act, act, act),
        grid=(t // tm,),
        in_specs=[
            row_spec,
            pl.BlockSpec((V7X_SUBLANES, d), lambda i: (jnp.maximum(i * rb - 1, 0), 0)),
            vec,
            _const_spec(mu.shape),
            _const_spec(wrkv.shape),
            _const_spec(la.shape),
            _const_spec(lbw.shape), _const_spec(lba.shape), _const_spec(lbg.shape),
            vec, vec, vec, vec, vec,
            _const_spec(e.shape), _const_spec(et.shape),
        ],
        out_specs=(row_spec,) * 8,
        compiler_params=_params(("parallel",)),
        name="rw_pre",
    )(h, h, g, mu, wrkv, la, lbw, lba, lbg, w0, a0, k_k, k_a, r_k, e, et)


def _rw_chunk_kernel(r_ref, lw_ref, k_ref, v_ref, kn_ref, b_ref, bonus_ref, g_ref, gng_ref, gnb_ref,
                     o_ref, st_ref):
    ci = pl.program_id(1)
    nb, ln, _ = r_ref.shape
    pair = 2 * RW_DH

    @pl.when(ci == 0)
    def _():
        st_ref[...] = jnp.zeros_like(st_ref)

    trow = lax.broadcasted_iota(jnp.int32, (ln, ln), 0)
    tcol = lax.broadcasted_iota(jnp.int32, (ln, ln), 1)
    tril = (tcol <= trow).astype(F32)
    lane = lax.broadcasted_iota(jnp.int32, (ln, pair), 1)
    head0 = lane < RW_DH
    srow = lax.broadcasted_iota(jnp.int32, (2 * ln, 2 * ln), 0)
    scol = lax.broadcasted_iota(jnp.int32, (2 * ln, 2 * ln), 1)
    strict_t = srow < scol
    incl_t = srow <= scol

    def stack(x):
        return jnp.concatenate([jnp.where(head0, x, 0.0), jnp.where(head0, 0.0, x)], axis=0).astype(BF16)

    chains = [(bi, p) for bi in range(nb) for p in range(RW_HEADS // 2)]
    pairs = range(len(chains))
    sls = [slice(p * pair, (p + 1) * pair) for _, p in chains]
    ar_s, b_s, k_s, bk, v_t, g_last = [], [], [], [], [], []
    for bi in range(nb):
        lw = lw_ref[bi]
        cum = _dot_exact(tril, lw)
        cum_last = cum[ln - 1:ln, :]
        g_inv = jnp.exp(-cum)
        g_rem = jnp.exp(cum_last - cum)
        kn = kn_ref[bi].astype(F32)
        bb = b_ref[bi].astype(F32)
        kk = k_ref[bi].astype(F32)
        a_all = -kn * jnp.exp(cum - lw)
        rt_all = r_ref[bi].astype(F32) * jnp.exp(cum)
        bt_all = bb * g_inv
        kt_all = kk * g_inv
        bh_all = bb * g_rem
        kh_all = kk * g_rem
        v_all = v_ref[bi].astype(F32)
        gl = jnp.exp(cum_last)
        for p in range(RW_HEADS // 2):
            sl = slice(p * pair, (p + 1) * pair)
            ar_s.append(jnp.concatenate([stack(a_all[:, sl]), stack(rt_all[:, sl])], axis=0))
            b_s.append(stack(bt_all[:, sl]))
            k_s.append(stack(kt_all[:, sl]))
            bk.append(jnp.concatenate([stack(bh_all[:, sl]), stack(kh_all[:, sl])], axis=0))
            vt = v_all[:, sl].T
            v_t.append(jnp.concatenate([vt[:RW_DH], vt[RW_DH:]], axis=1).astype(BF16))
            g_last.append(gl[:, sl])
    st = [st_ref[bi, p] for bi, p in chains]
    m_b = [_dot_nt(b_s[p], ar_s[p]) for p in pairs]
    m_k = [_dot_nt(k_s[p], ar_s[p]) for p in pairs]
    inter = [_dot_nt(st[p].astype(BF16), ar_s[p]) for p in pairs]
    pw = [jnp.where(strict_t, m_b[p][:, :2 * ln], 0.0) for p in pairs]
    ak = [jnp.where(strict_t, m_k[p][:, :2 * ln], 0.0).astype(BF16) for p in pairs]
    rb = [jnp.where(incl_t, m_b[p][:, 2 * ln:], 0.0).astype(BF16) for p in pairs]
    rk = [jnp.where(incl_t, m_k[p][:, 2 * ln:], 0.0).astype(BF16) for p in pairs]
    u = [inter[p][:, :2 * ln] + _dot(v_t[p], ak[p]) for p in pairs]
    n_steps = int(math.log2(ln))
    for step in range(n_steps):
        p16 = [pw[p].astype(BF16) for p in pairs]
        u = [u[p] + _dot(u[p].astype(BF16), p16[p]) for p in pairs]
        if step + 1 < n_steps:
            pw = [_dot(p16[p], p16[p]) for p in pairs]
    u16 = [u[p].astype(BF16) for p in pairs]
    y_t = [inter[p][:, 2 * ln:] + _dot(u16[p], rb[p]) + _dot(v_t[p], rk[p]) for p in pairs]
    for c, (bi, p) in enumerate(chains):
        sl = sls[c]
        uv = jnp.concatenate([u16[c], v_t[c]], axis=1)
        st_ref[bi, p] = st[c] * g_last[c] + _dot(uv, bk[c])
        yc = y_t[c] - jnp.mean(y_t[c], axis=0, keepdims=True)
        yn = (yc * lax.rsqrt(jnp.mean(yc * yc, axis=0, keepdims=True) + RW_GN_EPS)).T
        y_p = jnp.concatenate([yn[:ln], yn[ln:]], axis=1)
        out = (y_p * gng_ref[:, sl] + gnb_ref[:, sl] + bonus_ref[bi, :, sl].astype(F32))
        o_ref[bi, :, sl] = (out * g_ref[bi, :, sl].astype(F32)).astype(BF16)


def _rw_chunk(r, lw, k, v, kn, b, bonus, g, gn_g, gn_b, batch, seq):
    t, d = r.shape
    ln = min(RW_CHUNK, seq)
    nc = seq // ln
    nb = math.gcd(batch, RW_BATCH_PER_STEP)
    row_spec = pl.BlockSpec((nb, ln, d), lambda bi, ci: (bi, ci, 0))
    vec = pl.BlockSpec((1, d), lambda bi, ci: (0, 0))
    acts = [x.reshape(batch, seq, d) for x in (r, lw, k, v, kn, b, bonus, g)]
    out = pl.pallas_call(
        _rw_chunk_kernel,
        out_shape=jax.ShapeDtypeStruct((batch, seq, d), BF16),
        grid=(batch // nb, nc),
        in_specs=[row_spec] * 8 + [vec, vec],
        out_specs=row_spec,
        scratch_shapes=[pltpu.VMEM((nb, RW_HEADS // 2, RW_DH, 2 * RW_DH), F32)],
        compiler_params=_params(("parallel", "arbitrary")),
        name="rw_chunk",
    )(*acts, gn_g, gn_b)
    return out.reshape(t, d)


def _rope_tables(seq):
    half = RET_DK // 2
    pos = jnp.arange(seq, dtype=F32)
    inv_freq = 1.0 / (ROPE_BASE ** jnp.linspace(0.0, 1.0, half, dtype=F32))
    ang = pos[:, None] * inv_freq[None, :]
    return jnp.cos(ang), jnp.sin(ang)


def _retention_mixer(h, norm_g, w_in16, layer, gn, batch, seq):
    cos, sin = _rope_tables(seq)
    proj = _ret_proj(h, norm_g[None, :], w_in16, layer, cos, sin, seq)
    return _ret_chunk(proj, gn[None, :].astype(F32), batch, seq)


def _block_diag_tiles(w, tile):
    nb, bs, _ = w.shape
    rows = w.reshape(nb * bs // tile, tile, bs)
    dense = jnp.tile(rows, (1, 1, tile // bs))
    idx = jnp.arange(tile) // bs
    return jnp.where(idx[:, None] == idx[None, :], dense, 0.0)


def _mlstm_mixer(h, norm_g, w_in, conv_w, conv_b, wq, wk, wv, w_gate, b_gate, gn, skip, batch, seq):
    proj = _ml_proj(h, norm_g[None, :], w_in.astype(BF16), seq)
    tile = 2 * V7X_LANES
    mq = _block_diag_tiles(wq, tile).astype(BF16)
    mk = _block_diag_tiles(wk, tile).astype(BF16)
    mv = _block_diag_tiles(wv, tile).astype(BF16)
    n_gate = w_gate.shape[1]
    wgt = jnp.pad(w_gate.reshape(3, ML_INNER, n_gate), ((0, 0), (0, 0), (0, V7X_LANES - n_gate))).astype(BF16)
    bg = jnp.pad(b_gate, (0, V7X_LANES - n_gate))[None, :].astype(F32)
    c, q, k, v, gates = _ml_pre(proj, conv_w, conv_b[None, :], mq, mk, mv, wgt, bg, seq)
    return _ml_chunk(q, k, v, gates, c, proj, gn[None, :], skip[None, :], batch, seq)


def _pad_to(x, axis, size):
    pad = [(0, 0)] * x.ndim
    pad[axis] = (0, size - x.shape[axis])
    return jnp.pad(x, pad)


def _rwkv_mixer(h, norm_g, mu, w_rkv, w0, w_la, w_lb, a0, a_la, a_lb, g_la, g_lb, k_k, k_a, r_k,
                gn_g, gn_b, batch, seq):
    d = h.shape[1]
    lw_ = -(-w_la.shape[1] // V7X_LANES) * V7X_LANES
    la_ = -(-a_la.shape[1] // V7X_LANES) * V7X_LANES
    lg_ = -(-g_la.shape[1] // V7X_LANES) * V7X_LANES
    la = jnp.concatenate([_pad_to(w_la, 1, lw_), _pad_to(a_la, 1, la_), _pad_to(g_la, 1, lg_)],
                         axis=1).astype(BF16)
    lbw = _pad_to(w_lb, 0, lw_).astype(BF16)
    lba = _pad_to(a_lb, 0, la_).astype(BF16)
    lbg = _pad_to(g_lb, 0, lg_).astype(BF16)
    head = jnp.arange(d) // RW_DH
    e = (head[:, None] == jnp.arange(V7X_LANES)[None, :]).astype(BF16)
    et = e.T
    outs = _rw_pre(h, norm_g[None, :], mu, w_rkv.astype(BF16), la, lbw, lba, lbg,
                   w0[None, :], a0[None, :], k_k[None, :], k_a[None, :], r_k.reshape(1, d),
                   e, et, seq, lw_, la_)
    r, lw, k, v, kn, b, bonus, g = outs
    return _rw_chunk(r, lw, k, v, kn, b, bonus, g, gn_g[None, :], gn_b[None, :], batch, seq)


def kernel(x, norm_mix, norm_ffn, norm_final, ret_w_in, ret_gn, ret_w_out, ml_w_in, ml_conv_w, ml_conv_b, ml_wq, ml_wk, ml_wv, ml_w_gate, ml_b_gate, ml_gn, ml_skip, ml_w_out, rw_mu, rw_w_rkv, rw_w0, rw_w_lora_a, rw_w_lora_b, rw_a0, rw_a_lora_a, rw_a_lora_b, rw_g_lora_a, rw_g_lora_b, rw_k_k, rw_k_a, rw_r_k, rw_gn_g, rw_gn_b, rw_w_out, ffn_w_gu, ffn_w_down):
    batch, seq, d = x.shape
    depth = norm_mix.shape[0]
    h = x.reshape(batch * seq, d)
    ret_w_in16 = ret_w_in.astype(BF16)
    w_out16 = (ret_w_out.astype(BF16), ml_w_out.astype(BF16), rw_w_out.astype(BF16))
    w_gu16 = ffn_w_gu.astype(BF16)
    w_down16 = ffn_w_down.astype(BF16)
    for i in range(depth):
        kind, j = i % 3, i // 3
        if kind == 0:
            y = _retention_mixer(h, norm_mix[i], ret_w_in16, j, ret_gn[j], batch, seq)
        elif kind == 1:
            y = _mlstm_mixer(h, norm_mix[i], ml_w_in[j], ml_conv_w[j], ml_conv_b[j], ml_wq[j], ml_wk[j],
                             ml_wv[j], ml_w_gate[j], ml_b_gate[j], ml_gn[j], ml_skip[j], batch, seq)
        else:
            y = _rwkv_mixer(h, norm_mix[i], rw_mu[j], rw_w_rkv[j], rw_w0[j], rw_w_lora_a[j],
                            rw_w_lora_b[j], rw_a0[j], rw_a_lora_a[j], rw_a_lora_b[j], rw_g_lora_a[j],
                            rw_g_lora_b[j], rw_k_k[j], rw_k_a[j], rw_r_k[j], rw_gn_g[j], rw_gn_b[j],
                            batch, seq)
        h = _mix_ffn(y, w_out16[kind], j, h, norm_ffn[i][None, :], w_gu16, w_down16, i,
                     norm_final[None, :], i == depth - 1)
    return h.reshape(batch, seq, d)
```

```python
import functools
import math

import jax
import jax.numpy as jnp
from jax import lax
from jax.experimental import pallas as pl
from jax.experimental.pallas import tpu as pltpu

F32 = jnp.float32
BF16 = jnp.bfloat16

D_MODEL = 1024
D_FF = 2816
RMS_EPS = 1e-6

RET_HEADS = 4
RET_DK = 256
RET_DV = 512
RET_EPS = 1e-6
ROPE_BASE = 10000.0

ML_INNER = 2048
ML_HEADS = 4
ML_DH = 512
ML_CONV = 4
ML_EPS = 1e-6

RW_HEADS = 16
RW_DH = 64
RW_GN_EPS = 64e-5

V7X_LANES = 128
V7X_SUBLANES = 8
V7X_VMEM_LIMIT = 56 * 1024 * 1024

TM_PROJ = 512
PROJ_CHUNK = 512
TM_FFN = 512
FFN_CHUNK = 256
TM_PRE = 256
RET_CHUNK = 256
ML_CHUNK = 256
RW_CHUNK = 64
RW_BATCH_PER_STEP = 2


def _params(sem, vmem=V7X_VMEM_LIMIT):
    return pltpu.CompilerParams(dimension_semantics=sem, vmem_limit_bytes=vmem)


def _dot(a, b):
    return jnp.dot(a, b, preferred_element_type=F32)


def _dot_nt(a, b):
    return lax.dot_general(a, b, (((1,), (1,)), ((), ())), preferred_element_type=F32)


def _dot_tn(a, b):
    return lax.dot_general(a, b, (((0,), (0,)), ((), ())), preferred_element_type=F32)


def _dot_exact(a, b):
    hi = b.astype(BF16)
    lo = (b - hi.astype(F32)).astype(BF16)
    a16 = a.astype(BF16)
    return _dot(a16, hi) + _dot(a16, lo)


def _rms(x, g):
    ms = jnp.mean(x * x, axis=-1, keepdims=True)
    return x * lax.rsqrt(ms + RMS_EPS) * g


def _silu(x):
    return x * jax.nn.sigmoid(x)


def _const_spec(shape):
    nd = len(shape)
    return pl.BlockSpec(shape, lambda *_: (0,) * nd, pipeline_mode=pl.Buffered(1))


def _layer_spec(shape, layer):
    nd = len(shape)
    return pl.BlockSpec((None,) + tuple(shape), lambda *_: (layer,) + (0,) * nd,
                        pipeline_mode=pl.Buffered(1))


def _ret_proj_kernel(h_ref, g_ref, w_ref, cos_ref, sin_ref, o_ref):
    xn = _rms(h_ref[...], g_ref[...]).astype(BF16)
    cos = cos_ref[...]
    sin = sin_ref[...]
    half = RET_DK // 2
    hk = RET_HEADS * RET_DK
    hv = RET_HEADS * RET_DV
    for hd in range(2 * RET_HEADS):
        c0 = hd * RET_DK
        acc = _dot(xn, w_ref[:, c0:c0 + RET_DK])
        scale = RET_DK ** -0.5 if hd >= RET_HEADS else 1.0
        t1 = acc[:, :half]
        t2 = acc[:, half:]
        o_ref[:, c0:c0 + half] = ((t1 * cos - t2 * sin) * scale).astype(BF16)
        o_ref[:, c0 + half:c0 + RET_DK] = ((t2 * cos + t1 * sin) * scale).astype(BF16)
    for c0 in range(2 * hk, 2 * hk + hv, PROJ_CHUNK):
        o_ref[:, c0:c0 + PROJ_CHUNK] = _dot(xn, w_ref[:, c0:c0 + PROJ_CHUNK]).astype(BF16)
    for c0 in range(2 * hk + hv, 2 * hk + 2 * hv, PROJ_CHUNK):
        o_ref[:, c0:c0 + PROJ_CHUNK] = _silu(_dot(xn, w_ref[:, c0:c0 + PROJ_CHUNK])).astype(BF16)


def _ret_proj(h, g, w, layer, cos, sin, seq):
    t, d = h.shape
    n = w.shape[2]
    tm = min(TM_PROJ, seq)
    spt = seq // tm
    return pl.pallas_call(
        _ret_proj_kernel,
        out_shape=jax.ShapeDtypeStruct((t, n), BF16),
        grid=(t // tm,),
        in_specs=[
            pl.BlockSpec((tm, d), lambda i: (i, 0)),
            _const_spec((1, d)),
            _layer_spec((d, n), layer),
            pl.BlockSpec((tm, RET_DK // 2), lambda i: (i % spt, 0)),
            pl.BlockSpec((tm, RET_DK // 2), lambda i: (i % spt, 0)),
        ],
        out_specs=pl.BlockSpec((tm, n), lambda i: (i, 0)),
        compiler_params=_params(("parallel",)),
        name="ret_proj",
    )(h, g, w, cos, sin)


def _mix_ffn_kernel(y_ref, wo_ref, h_ref, g_ref, wgu_ref, wd_ref, gf_ref, o_ref, *, final_norm):
    h = h_ref[...] + _dot(y_ref[...], wo_ref[...])
    xn = _rms(h, g_ref[...]).astype(BF16)
    acc = h
    for f in range(0, D_FF, FFN_CHUNK):
        gate = _dot(xn, wgu_ref[:, f:f + FFN_CHUNK])
        up = _dot(xn, wgu_ref[:, D_FF + f:D_FF + f + FFN_CHUNK])
        act = (_silu(gate) * up).astype(BF16)
        acc = acc + _dot(act, wd_ref[f:f + FFN_CHUNK, :])
    if final_norm:
        acc = _rms(acc, gf_ref[...])
    o_ref[...] = acc


def _mix_ffn(y, wo, wo_layer, h, g, wgu, wd, ffn_layer, gf, final_norm):
    t, d = h.shape
    k = y.shape[1]
    tm = min(TM_FFN, t)
    return pl.pallas_call(
        functools.partial(_mix_ffn_kernel, final_norm=final_norm),
        out_shape=jax.ShapeDtypeStruct((t, d), F32),
        grid=(t // tm,),
        in_specs=[
            pl.BlockSpec((tm, k), lambda i: (i, 0)),
            _layer_spec((k, d), wo_layer),
            pl.BlockSpec((tm, d), lambda i: (i, 0)),
            _const_spec((1, d)),
            _layer_spec((d, 2 * D_FF), ffn_layer),
            _layer_spec((D_FF, d), ffn_layer),
            _const_spec((1, d)),
        ],
        out_specs=pl.BlockSpec((tm, d), lambda i: (i, 0)),
        compiler_params=_params(("parallel",)),
        name="mix_ffn",
    )(y, wo, h, g, wgu, wd, gf)


def _ret_log_gamma(head):
    return math.log1p(-2.0 ** (-5.0 - head))


def _ret_chunk_kernel(q_ref, k_ref, v_ref, g_ref, gn_ref, o_ref, st_ref):
    c = pl.program_id(1)
    ln = q_ref.shape[0]

    @pl.when(c == 0)
    def _():
        st_ref[...] = jnp.zeros_like(st_ref)

    row = lax.broadcasted_iota(jnp.int32, (ln, ln), 0)
    col = lax.broadcasted_iota(jnp.int32, (ln, ln), 1)
    rel = (row - col).astype(F32)
    pos = lax.broadcasted_iota(jnp.int32, (ln, 1), 0).astype(F32)
    heads = range(RET_HEADS)
    lg = [_ret_log_gamma(hd) for hd in heads]
    q = [q_ref[:, hd * RET_DK:(hd + 1) * RET_DK] for hd in heads]
    k = [k_ref[:, hd * RET_DK:(hd + 1) * RET_DK] for hd in heads]
    v = [v_ref[:, hd * RET_DV:(hd + 1) * RET_DV] for hd in heads]
    st = [st_ref[hd] for hd in heads]
    qk = [_dot_nt(q[hd], k[hd]) for hd in heads]
    inter = [_dot(q[hd], st[hd].astype(BF16)) for hd in heads]
    k_dec = [(k[hd].astype(F32) * jnp.exp(lg[hd] * (ln - 1.0 - pos))).astype(BF16) for hd in heads]
    scores = [(qk[hd] * jnp.where(rel >= 0, jnp.exp(lg[hd] * jnp.maximum(rel, 0.0)), 0.0)).astype(BF16)
              for hd in heads]
    out = [_dot(scores[hd], v[hd]) + inter[hd] * jnp.exp(lg[hd] * (pos + 1.0)) for hd in heads]
    for hd in heads:
        st_ref[hd] = st[hd] * math.exp(lg[hd] * ln) + _dot_tn(k_dec[hd], v[hd])
    for hd in heads:
        sl = slice(hd * RET_DV, (hd + 1) * RET_DV)
        ms = jnp.mean(out[hd] * out[hd], axis=-1, keepdims=True)
        y = out[hd] * lax.rsqrt(ms + RET_EPS) * gn_ref[:, sl]
        o_ref[:, sl] = (y * g_ref[:, sl].astype(F32)).astype(BF16)


def _ret_chunk(proj, gn, batch, seq):
    t = proj.shape[0]
    ln = min(RET_CHUNK, seq)
    nc = seq // ln
    hk = RET_HEADS * RET_DK
    hv = RET_HEADS * RET_DV
    return pl.pallas_call(
        _ret_chunk_kernel,
        out_shape=jax.ShapeDtypeStruct((t, hv), BF16),
        grid=(batch, nc),
        in_specs=[
            pl.BlockSpec((ln, hk), lambda b, c: (b * nc + c, 0)),
            pl.BlockSpec((ln, hk), lambda b, c: (b * nc + c, 1)),
            pl.BlockSpec((ln, hv), lambda b, c: (b * nc + c, 1)),
            pl.BlockSpec((ln, hv), lambda b, c: (b * nc + c, 2)),
            pl.BlockSpec((1, hv), lambda b, c: (0, 0)),
        ],
        out_specs=pl.BlockSpec((ln, hv), lambda b, c: (b * nc + c, 0)),
        scratch_shapes=[pltpu.VMEM((RET_HEADS, RET_DK, RET_DV), F32)],
        compiler_params=_params(("parallel", "arbitrary")),
        name="ret_chunk",
    )(proj, proj, proj, proj, gn)


def _prev_rows(cur, prev8, shift):
    ext = jnp.concatenate([prev8, cur], axis=0)
    return pltpu.roll(ext, shift, axis=0)[V7X_SUBLANES:, :]


def _log_sigmoid(x):
    return jnp.minimum(x, 0.0) - jnp.log1p(jnp.exp(-jnp.abs(x)))


def _ml_front_kernel(h_ref, g_ref, w_ref, cw_ref, cb_ref, mq_ref, mk_ref, mv_ref, wgt_ref, bg_ref,
                     c_ref, q_ref, k_ref, v_ref, z_ref, gt_ref, tail_ref, *, tiles_per_seq):
    i = pl.program_id(0)
    tm = h_ref.shape[0]

    @pl.when((i % tiles_per_seq) == 0)
    def _():
        tail_ref[...] = jnp.zeros_like(tail_ref)

    xn = _rms(h_ref[...], g_ref[...]).astype(BF16)
    blk = mq_ref.shape[1]
    gates = jnp.zeros((tm, V7X_LANES), F32) + bg_ref[...]
    for c0 in range(0, ML_INNER, PROJ_CHUNK):
        cs = slice(c0, c0 + PROJ_CHUNK)
        u = _dot(xn, w_ref[:, cs])
        prev = tail_ref[:, cs]
        tail_ref[:, cs] = u[tm - V7X_SUBLANES:, :]
        conv = u * cw_ref[ML_CONV - 1:ML_CONV, cs] + cb_ref[:, cs]
        for s in range(1, ML_CONV):
            conv = conv + _prev_rows(u, prev, s) * cw_ref[ML_CONV - 1 - s:ML_CONV - s, cs]
        c16 = _silu(conv).astype(BF16)
        u16 = u.astype(BF16)
        c_ref[:, cs] = c16
        for n in range(PROJ_CHUNK // blk):
            ls = slice(n * blk, (n + 1) * blk)
            sl = slice(c0 + n * blk, c0 + (n + 1) * blk)
            tile = sl.start // blk
            qn = _dot(c16[:, ls], mq_ref[tile]).astype(BF16)
            kn = _dot(c16[:, ls], mk_ref[tile]).astype(BF16)
            vn = _dot(u16[:, ls], mv_ref[tile]).astype(BF16)
            q_ref[:, sl] = qn
            k_ref[:, sl] = (kn.astype(F32) * (ML_DH ** -0.5)).astype(BF16)
            v_ref[:, sl] = vn
            gates = (gates + _dot(qn, wgt_ref[0, sl, :]) + _dot(kn, wgt_ref[1, sl, :])
                     + _dot(vn, wgt_ref[2, sl, :]))
    for c0 in range(0, ML_INNER, PROJ_CHUNK):
        z_ref[:, c0:c0 + PROJ_CHUNK] = _silu(
            _dot(xn, w_ref[:, ML_INNER + c0:ML_INNER + c0 + PROJ_CHUNK])).astype(BF16)
    lane = lax.broadcasted_iota(jnp.int32, gates.shape, 1)
    gt_ref[...] = jnp.where(jnp.logical_and(lane >= ML_HEADS, lane < 2 * ML_HEADS),
                            _log_sigmoid(gates), gates)


def _ml_front(h, g, w, conv_w, conv_b, mq, mk, mv, wgt, bg, seq):
    t, d = h.shape
    tm = min(TM_PROJ, seq)
    tps = seq // tm
    act = jax.ShapeDtypeStruct((t, ML_INNER), BF16)
    row_spec = pl.BlockSpec((tm, ML_INNER), lambda i: (i, 0))
    return pl.pallas_call(
        functools.partial(_ml_front_kernel, tiles_per_seq=tps),
        out_shape=(act, act, act, act, act, jax.ShapeDtypeStruct((t, V7X_LANES), F32)),
        grid=(t // tm,),
        in_specs=[
            pl.BlockSpec((tm, d), lambda i: (i, 0)),
            _const_spec((1, d)),
            _const_spec(w.shape),
            _const_spec((ML_CONV, ML_INNER)),
            _const_spec((1, ML_INNER)),
            _const_spec(mq.shape), _const_spec(mk.shape), _const_spec(mv.shape),
            _const_spec(wgt.shape),
            _const_spec((1, V7X_LANES)),
        ],
        out_specs=(row_spec, row_spec, row_spec, row_spec, row_spec,
                   pl.BlockSpec((tm, V7X_LANES), lambda i: (i, 0))),
        scratch_shapes=[pltpu.VMEM((V7X_SUBLANES, ML_INNER), F32)],
        compiler_params=_params(("arbitrary",)),
        name="ml_front",
    )(h, g, w, conv_w, conv_b, mq, mk, mv, wgt, bg)


def _ml_chunk_kernel(q_ref, k_ref, v_ref, gt_ref, c_ref, z_ref, gn_ref, sk_ref, o_ref,
                     cst_ref, nst_ref, mst_ref):
    ci = pl.program_id(1)
    ln = q_ref.shape[0]

    @pl.when(ci == 0)
    def _():
        cst_ref[...] = jnp.zeros_like(cst_ref)
        nst_ref[...] = jnp.zeros_like(nst_ref)
        mst_ref[...] = jnp.zeros_like(mst_ref)

    row = lax.broadcasted_iota(jnp.int32, (ln, ln), 0)
    col = lax.broadcasted_iota(jnp.int32, (ln, ln), 1)
    causal = col <= row
    gates = gt_ref[...]
    cum = _dot_exact(causal.astype(F32), gates)
    gates_t = gates.T
    cum_t = cum.T
    heads = range(ML_HEADS)
    sls = [slice(hd * ML_DH, (hd + 1) * ML_DH) for hd in heads]
    q = [q_ref[:, sl] for sl in sls]
    k = [k_ref[:, sl] for sl in sls]
    v = [v_ref[:, sl] for sl in sls]
    cst = [cst_ref[hd] for hd in heads]
    n_row = [nst_ref[hd:hd + 1, :] for hd in heads]
    m_prev = [mst_ref[hd:hd + 1, 0:1] for hd in heads]
    i_col = [gates[:, hd:hd + 1] for hd in heads]
    b_col = [cum[:, ML_HEADS + hd:ML_HEADS + hd + 1] for hd in heads]
    qk = [_dot_nt(q[hd], k[hd]) for hd in heads]
    inter = [_dot(q[hd], cst[hd].astype(BF16)) for hd in heads]
    log_d = [jnp.where(causal, b_col[hd] - cum_t[ML_HEADS + hd:ML_HEADS + hd + 1, :] + gates_t[hd:hd + 1, :],
                       -jnp.inf) for hd in heads]
    log_inter = [b_col[hd] + m_prev[hd] for hd in heads]
    m_row = [jnp.maximum(log_inter[hd], jnp.max(log_d[hd], axis=-1, keepdims=True)) for hd in heads]
    w_inter = [jnp.exp(log_inter[hd] - m_row[hd]) for hd in heads]
    s = [qk[hd] * jnp.exp(log_d[hd] - m_row[hd]) for hd in heads]
    num = [_dot(s[hd].astype(BF16), v[hd]) + w_inter[hd] * inter[hd] for hd in heads]
    den = [jnp.sum(s[hd], axis=-1, keepdims=True) + w_inter[hd] * jnp.sum(
        q[hd].astype(F32) * n_row[hd], axis=-1, keepdims=True) for hd in heads]
    hh = [num[hd] / jnp.maximum(jnp.abs(den[hd]), jnp.exp(-m_row[hd])) for hd in heads]
    b_last = [b_col[hd][ln - 1:ln, :] for hd in heads]
    log_w = [b_last[hd] - b_col[hd] + i_col[hd] for hd in heads]
    m_new = [jnp.maximum(b_last[hd] + m_prev[hd], jnp.max(log_w[hd], axis=0, keepdims=True)) for hd in heads]
    kw = [k[hd].astype(F32) * jnp.exp(log_w[hd] - m_new[hd]) for hd in heads]
    carry = [jnp.exp(b_last[hd] + m_prev[hd] - m_new[hd]) for hd in heads]
    for hd in heads:
        cst_ref[hd] = carry[hd] * cst[hd] + _dot_tn(kw[hd].astype(BF16), v[hd])
        nst_ref[hd:hd + 1, :] = carry[hd] * n_row[hd] + jnp.sum(kw[hd], axis=0, keepdims=True)
        mst_ref[hd:hd + 1, :] = jnp.broadcast_to(m_new[hd], (1, V7X_LANES))
    for hd in heads:
        sl = sls[hd]
        hc = hh[hd] - jnp.mean(hh[hd], axis=-1, keepdims=True)
        y = hc * lax.rsqrt(jnp.mean(hc * hc, axis=-1, keepdims=True) + ML_EPS) * gn_ref[:, sl]
        y = (y + sk_ref[:, sl] * c_ref[:, sl].astype(F32)) * z_ref[:, sl].astype(F32)
        o_ref[:, sl] = y.astype(BF16)


def _ml_chunk(q, k, v, gates, c, z, gn, skip, batch, seq):
    t = q.shape[0]
    ln = min(ML_CHUNK, seq)
    nc = seq // ln
    row_spec = pl.BlockSpec((ln, ML_INNER), lambda b, ci: (b * nc + ci, 0))
    return pl.pallas_call(
        _ml_chunk_kernel,
        out_shape=jax.ShapeDtypeStruct((t, ML_INNER), BF16),
        grid=(batch, nc),
        in_specs=[
            row_spec, row_spec, row_spec,
            pl.BlockSpec((ln, V7X_LANES), lambda b, ci: (b * nc + ci, 0)),
            row_spec,
            row_spec,
            pl.BlockSpec((1, ML_INNER), lambda b, ci: (0, 0)),
            pl.BlockSpec((1, ML_INNER), lambda b, ci: (0, 0)),
        ],
        out_specs=row_spec,
        scratch_shapes=[
            pltpu.VMEM((ML_HEADS, ML_DH, ML_DH), F32),
            pltpu.VMEM((V7X_SUBLANES, ML_DH), F32),
            pltpu.VMEM((V7X_SUBLANES, V7X_LANES), F32),
        ],
        compiler_params=_params(("parallel", "arbitrary")),
        name="ml_chunk",
    )(q, k, v, gates, c, z, gn, skip)


def _dot_split(x, m):
    hi = x.astype(BF16)
    lo = (x - hi.astype(F32)).astype(BF16)
    return _dot(hi, m) + _dot(lo, m)


def _rw_pre_kernel(h_ref, hp_ref, g_ref, mu_ref, wrkv_ref, la_ref, lbw_ref, lba_ref, lbg_ref,
                   w0_ref, a0_ref, kk_ref, ka_ref, rk_ref, e_ref, et_ref,
                   r_out, lw_out, k_out, v_out, kn_out, b_out, bonus_out, g_out, *, tiles_per_seq,
                   lora_w, lora_a):
    i = pl.program_id(0)
    g = g_ref[...]
    xn = _rms(h_ref[...], g)
    first = (i % tiles_per_seq) == 0
    xp8 = jnp.where(first, 0.0, _rms(hp_ref[...], g))
    dx = _prev_rows(xn, xp8, 1) - xn

    def mix(n):
        return (xn + dx * mu_ref[n:n + 1, :]).astype(BF16)

    r = _dot(mix(0), wrkv_ref[0])
    k = _dot(mix(1), wrkv_ref[1])
    v = _dot(mix(2), wrkv_ref[2])
    hw = jnp.tanh(_dot(mix(3), la_ref[:, 0:lora_w])).astype(BF16)
    ha = _dot(mix(4), la_ref[:, lora_w:lora_w + lora_a]).astype(BF16)
    hg = jax.nn.sigmoid(_dot(mix(5), la_ref[:, lora_w + lora_a:])).astype(BF16)
    w_pre = w0_ref[...] + _dot(hw, lbw_ref[...])
    w_log = jnp.minimum(w_pre, 0.0) - jnp.log1p(jnp.exp(-jnp.abs(w_pre))) - 0.5
    lw_out[...] = -jnp.exp(w_log)
    alpha = jax.nn.sigmoid(a0_ref[...] + _dot(ha, lba_ref[...]))
    g_out[...] = _dot(hg, lbg_ref[...]).astype(BF16)

    kk = k * kk_ref[...]
    ss = _dot_split(kk * kk, e_ref[...])
    inv = lax.rsqrt(jnp.maximum(ss, 1e-24))
    kn = kk * _dot_split(inv, et_ref[...])
    k_mod = k * (1.0 + (alpha - 1.0) * ka_ref[...])
    rk = _dot_split(r * k_mod * rk_ref[...], e_ref[...])
    bonus_out[...] = (_dot_split(rk, et_ref[...]) * v).astype(BF16)
    r_out[...] = r.astype(BF16)
    k_out[...] = k_mod.astype(BF16)
    v_out[...] = v.astype(BF16)
    kn_out[...] = kn.astype(BF16)
    b_out[...] = (kn * alpha).astype(BF16)


def _rw_pre(h, g, mu, wrkv, la, lbw, lba, lbg, w0, a0, k_k, k_a, r_k, e, et, seq, lora_w, lora_a):
    t, d = h.shape
    tm = min(TM_PRE, seq)
    tps = seq // tm
    rb = tm // V7X_SUBLANES
    row_spec = pl.BlockSpec((tm, d), lambda i: (i, 0))
    act = jax.ShapeDtypeStruct((t, d), BF16)
    vec = _const_spec((1, d))
    return pl.pallas_call(
        functools.partial(_rw_pre_kernel, tiles_per_seq=tps, lora_w=lora_w, lora_a=lora_a),
        out_shape=(act, jax.ShapeDtypeStruct((t, d), F32), act, act, act, act, act, act),
        grid=(t // tm,),
        in_specs=[
            row_spec,
            pl.BlockSpec((V7X_SUBLANES, d), lambda i: (jnp.maximum(i * rb - 1, 0), 0)),
            vec,
            _const_spec(mu.shape),
            _const_spec(wrkv.shape),
            _const_spec(la.shape),
            _const_spec(lbw.shape), _const_spec(lba.shape), _const_spec(lbg.shape),
            vec, vec, vec, vec, vec,
            _const_spec(e.shape), _const_spec(et.shape),
        ],
        out_specs=(row_spec,) * 8,
        compiler_params=_params(("parallel",)),
        name="rw_pre",
    )(h, h, g, mu, wrkv, la, lbw, lba, lbg, w0, a0, k_k, k_a, r_k, e, et)


def _rw_chunk_kernel(r_ref, lw_ref, k_ref, v_ref, kn_ref, b_ref, bonus_ref, g_ref, gng_ref, gnb_ref,
                     o_ref, st_ref):
    ci = pl.program_id(1)
    nb, ln, _ = r_ref.shape
    pair = 2 * RW_DH

    @pl.when(ci == 0)
    def _():
        st_ref[...] = jnp.zeros_like(st_ref)

    trow = lax.broadcasted_iota(jnp.int32, (ln, ln), 0)
    tcol = lax.broadcasted_iota(jnp.int32, (ln, ln), 1)
    tril = (tcol <= trow).astype(F32)
    lane = lax.broadcasted_iota(jnp.int32, (ln, pair), 1)
    head0 = lane < RW_DH
    srow = lax.broadcasted_iota(jnp.int32, (2 * ln, 2 * ln), 0)
    scol = lax.broadcasted_iota(jnp.int32, (2 * ln, 2 * ln), 1)
    strict_t = srow < scol
    incl_t = srow <= scol

    def stack(x):
        return jnp.concatenate([jnp.where(head0, x, 0.0), jnp.where(head0, 0.0, x)], axis=0).astype(BF16)

    chains = [(bi, p) for bi in range(nb) for p in range(RW_HEADS // 2)]
    pairs = range(len(chains))
    sls = [slice(p * pair, (p + 1) * pair) for _, p in chains]
    ar_s, b_s, k_s, bk, v_t, g_last = [], [], [], [], [], []
    for bi in range(nb):
        lw = lw_ref[bi]
        cum = _dot_exact(tril, lw)
        cum_last = cum[ln - 1:ln, :]
        g_inv = jnp.exp(-cum)
        g_rem = jnp.exp(cum_last - cum)
        kn = kn_ref[bi].astype(F32)
        bb = b_ref[bi].astype(F32)
        kk = k_ref[bi].astype(F32)
        a_all = -kn * jnp.exp(cum - lw)
        rt_all = r_ref[bi].astype(F32) * jnp.exp(cum)
        bt_all = bb * g_inv
        kt_all = kk * g_inv
        bh_all = bb * g_rem
        kh_all = kk * g_rem
        v_all = v_ref[bi].astype(F32)
        gl = jnp.exp(cum_last)
        for p in range(RW_HEADS // 2):
            sl = slice(p * pair, (p + 1) * pair)
            ar_s.append(jnp.concatenate([stack(a_all[:, sl]), stack(rt_all[:, sl])], axis=0))
            b_s.append(stack(bt_all[:, sl]))
            k_s.append(stack(kt_all[:, sl]))
            bk.append(jnp.concatenate([stack(bh_all[:, sl]), stack(kh_all[:, sl])], axis=0))
            vt = v_all[:, sl].T
            v_t.append(jnp.concatenate([vt[:RW_DH], vt[RW_DH:]], axis=1).astype(BF16))
            g_last.append(gl[:, sl])
    st = [st_ref[bi, p] for bi, p in chains]
    m_b = [_dot_nt(b_s[p], ar_s[p]) for p in pairs]
    m_k = [_dot_nt(k_s[p], ar_s[p]) for p in pairs]
    inter = [_dot_nt(st[p].astype(BF16), ar_s[p]) for p in pairs]
    pw = [jnp.where(strict_t, m_b[p][:, :2 * ln], 0.0) for p in pairs]
    ak = [jnp.where(strict_t, m_k[p][:, :2 * ln], 0.0).astype(BF16) for p in pairs]
    rb = [jnp.where(incl_t, m_b[p][:, 2 * ln:], 0.0).astype(BF16) for p in pairs]
    rk = [jnp.where(incl_t, m_k[p][:, 2 * ln:], 0.0).astype(BF16) for p in pairs]
    u = [inter[p][:, :2 * ln] + _dot(v_t[p], ak[p]) for p in pairs]
    n_steps = int(math.log2(ln))
    for step in range(n_steps):
        p16 = [pw[p].astype(BF16) for p in pairs]
        u = [u[p] + _dot(u[p].astype(BF16), p16[p]) for p in pairs]
        if step + 1 < n_steps:
            pw = [_dot(p16[p], p16[p]) for p in pairs]
    u16 = [u[p].astype(BF16) for p in pairs]
    y_t = [inter[p][:, 2 * ln:] + _dot(u16[p], rb[p]) + _dot(v_t[p], rk[p]) for p in pairs]
    for c, (bi, p) in enumerate(chains):
        sl = sls[c]
        uv = jnp.concatenate([u16[c], v_t[c]], axis=1)
        st_ref[bi, p] = st[c] * g_last[c] + _dot(uv, bk[c])
        yc = y_t[c] - jnp.mean(y_t[c], axis=0, keepdims=True)
        yn = (yc * lax.rsqrt(jnp.mean(yc * yc, axis=0, keepdims=True) + RW_GN_EPS)).T
        y_p = jnp.concatenate([yn[:ln], yn[ln:]], axis=1)
        out = (y_p * gng_ref[:, sl] + gnb_ref[:, sl] + bonus_ref[bi, :, sl].astype(F32))
        o_ref[bi, :, sl] = (out * g_ref[bi, :, sl].astype(F32)).astype(BF16)


def _rw_chunk(r, lw, k, v, kn, b, bonus, g, gn_g, gn_b, batch, seq):
    t, d = r.shape
    ln = min(RW_CHUNK, seq)
    nc = seq // ln
    nb = math.gcd(batch, RW_BATCH_PER_STEP)
    row_spec = pl.BlockSpec((nb, ln, d), lambda bi, ci: (bi, ci, 0))
    vec = pl.BlockSpec((1, d), lambda bi, ci: (0, 0))
    acts = [x.reshape(batch, seq, d) for x in (r, lw, k, v, kn, b, bonus, g)]
    out = pl.pallas_call(
        _rw_chunk_kernel,
        out_shape=jax.ShapeDtypeStruct((batch, seq, d), BF16),
        grid=(batch // nb, nc),
        in_specs=[row_spec] * 8 + [vec, vec],
        out_specs=row_spec,
        scratch_shapes=[pltpu.VMEM((nb, RW_HEADS // 2, RW_DH, 2 * RW_DH), F32)],
        compiler_params=_params(("parallel", "arbitrary")),
        name="rw_chunk",
    )(*acts, gn_g, gn_b)
    return out.reshape(t, d)


def _rope_tables(seq):
    half = RET_DK // 2
    pos = jnp.arange(seq, dtype=F32)
    inv_freq = 1.0 / (ROPE_BASE ** jnp.linspace(0.0, 1.0, half, dtype=F32))
    ang = pos[:, None] * inv_freq[None, :]
    return jnp.cos(ang), jnp.sin(ang)


def _retention_mixer(h, norm_g, w_in16, layer, gn, batch, seq):
    cos, sin = _rope_tables(seq)
    proj = _ret_proj(h, norm_g[None, :], w_in16, layer, cos, sin, seq)
    return _ret_chunk(proj, gn[None, :].astype(F32), batch, seq)


def _block_diag_tiles(w, tile):
    nb, bs, _ = w.shape
    rows = w.reshape(nb * bs // tile, tile, bs)
    dense = jnp.tile(rows, (1, 1, tile // bs))
    idx = jnp.arange(tile) // bs
    return jnp.where(idx[:, None] == idx[None, :], dense, 0.0)


def _mlstm_mixer(h, norm_g, w_in, conv_w, conv_b, wq, wk, wv, w_gate, b_gate, gn, skip, batch, seq):
    tile = 2 * V7X_LANES
    mq = _block_diag_tiles(wq, tile).astype(BF16)
    mk = _block_diag_tiles(wk, tile).astype(BF16)
    mv = _block_diag_tiles(wv, tile).astype(BF16)
    n_gate = w_gate.shape[1]
    wgt = jnp.pad(w_gate.reshape(3, ML_INNER, n_gate), ((0, 0), (0, 0), (0, V7X_LANES - n_gate))).astype(BF16)
    bg = jnp.pad(b_gate, (0, V7X_LANES - n_gate))[None, :].astype(F32)
    c, q, k, v, z, gates = _ml_front(h, norm_g[None, :], w_in.astype(BF16), conv_w, conv_b[None, :],
                                     mq, mk, mv, wgt, bg, seq)
    return _ml_chunk(q, k, v, gates, c, z, gn[None, :], skip[None, :], batch, seq)


def _pad_to(x, axis, size):
    pad = [(0, 0)] * x.ndim
    pad[axis] = (0, size - x.shape[axis])
    return jnp.pad(x, pad)


def _rwkv_mixer(h, norm_g, mu, w_rkv, w0, w_la, w_lb, a0, a_la, a_lb, g_la, g_lb, k_k, k_a, r_k,
                gn_g, gn_b, batch, seq):
    d = h.shape[1]
    lw_ = -(-w_la.shape[1] // V7X_LANES) * V7X_LANES
    la_ = -(-a_la.shape[1] // V7X_LANES) * V7X_LANES
    lg_ = -(-g_la.shape[1] // V7X_LANES) * V7X_LANES
    la = jnp.concatenate([_pad_to(w_la, 1, lw_), _pad_to(a_la, 1, la_), _pad_to(g_la, 1, lg_)],
                         axis=1).astype(BF16)
    lbw = _pad_to(w_lb, 0, lw_).astype(BF16)
    lba = _pad_to(a_lb, 0, la_).astype(BF16)
    lbg = _pad_to(g_lb, 0, lg_).astype(BF16)
    head = jnp.arange(d) // RW_DH
    e = (head[:, None] == jnp.arange(V7X_LANES)[None, :]).astype(BF16)
    et = e.T
    outs = _rw_pre(h, norm_g[None, :], mu, w_rkv.astype(BF16), la, lbw, lba, lbg,
                   w0[None, :], a0[None, :], k_k[None, :], k_a[None, :], r_k.reshape(1, d),
                   e, et, seq, lw_, la_)
    r, lw, k, v, kn, b, bonus, g = outs
    return _rw_chunk(r, lw, k, v, kn, b, bonus, g, gn_g[None, :], gn_b[None, :], batch, seq)


def kernel(x, norm_mix, norm_ffn, norm_final, ret_w_in, ret_gn, ret_w_out, ml_w_in, ml_conv_w, ml_conv_b, ml_wq, ml_wk, ml_wv, ml_w_gate, ml_b_gate, ml_gn, ml_skip, ml_w_out, rw_mu, rw_w_rkv, rw_w0, rw_w_lora_a, rw_w_lora_b, rw_a0, rw_a_lora_a, rw_a_lora_b, rw_g_lora_a, rw_g_lora_b, rw_k_k, rw_k_a, rw_r_k, rw_gn_g, rw_gn_b, rw_w_out, ffn_w_gu, ffn_w_down):
    batch, seq, d = x.shape
    depth = norm_mix.shape[0]
    h = x.reshape(batch * seq, d)
    ret_w_in16 = ret_w_in.astype(BF16)
    w_out16 = (ret_w_out.astype(BF16), ml_w_out.astype(BF16), rw_w_out.astype(BF16))
    w_gu16 = ffn_w_gu.astype(BF16)
    w_down16 = ffn_w_down.astype(BF16)
    for i in range(depth):
        kind, j = i % 3, i // 3
        if kind == 0:
            y = _retention_mixer(h, norm_mix[i], ret_w_in16, j, ret_gn[j], batch, seq)
        elif kind == 1:
            y = _mlstm_mixer(h, norm_mix[i], ml_w_in[j], ml_conv_w[j], ml_conv_b[j], ml_wq[j], ml_wk[j],
                             ml_wv[j], ml_w_gate[j], ml_b_gate[j], ml_gn[j], ml_skip[j], batch, seq)
        else:
            y = _rwkv_mixer(h, norm_mix[i], rw_mu[j], rw_w_rkv[j], rw_w0[j], rw_w_lora_a[j],
                            rw_w_lora_b[j], rw_a0[j], rw_a_lora_a[j], rw_a_lora_b[j], rw_g_lora_a[j],
                            rw_g_lora_b[j], rw_k_k[j], rw_k_a[j], rw_r_k[j], rw_gn_g[j], rw_gn_b[j],
                            batch, seq)
        h = _mix_ffn(y, w_out16[kind], j, h, norm_ffn[i][None, :], w_gu16, w_down16, i,
                     norm_final[None, :], i == depth - 1)
    return h.reshape(batch, seq, d)
```

```python
import functools
import math

import jax
import jax.numpy as jnp
from jax import lax
from jax.experimental import pallas as pl
from jax.experimental.pallas import tpu as pltpu

F32 = jnp.float32
BF16 = jnp.bfloat16

D_MODEL = 1024
D_FF = 2816
RMS_EPS = 1e-6

RET_HEADS = 4
RET_DK = 256
RET_DV = 512
RET_EPS = 1e-6
ROPE_BASE = 10000.0

ML_INNER = 2048
ML_HEADS = 4
ML_DH = 512
ML_CONV = 4
ML_EPS = 1e-6

RW_HEADS = 16
RW_DH = 64
RW_GN_EPS = 64e-5

V7X_LANES = 128
V7X_SUBLANES = 8
V7X_VMEM_LIMIT = 56 * 1024 * 1024

TM_PROJ = 512
PROJ_CHUNK = 512
TM_FFN = 1024
FFN_CHUNK = 256
TM_PRE = 256
RET_CHUNK = 256
ML_CHUNK = 256
RW_CHUNK = 64
RW_BATCH_PER_STEP = 4


def _params(sem, vmem=V7X_VMEM_LIMIT):
    return pltpu.CompilerParams(dimension_semantics=sem, vmem_limit_bytes=vmem)


def _dot(a, b):
    return jnp.dot(a, b, preferred_element_type=F32)


def _dot_nt(a, b):
    return lax.dot_general(a, b, (((1,), (1,)), ((), ())), preferred_element_type=F32)


def _dot_tn(a, b):
    return lax.dot_general(a, b, (((0,), (0,)), ((), ())), preferred_element_type=F32)


def _dot_exact(a, b):
    hi = b.astype(BF16)
    lo = (b - hi.astype(F32)).astype(BF16)
    a16 = a.astype(BF16)
    return _dot(a16, hi) + _dot(a16, lo)


def _rms(x, g):
    ms = jnp.mean(x * x, axis=-1, keepdims=True)
    return x * lax.rsqrt(ms + RMS_EPS) * g


def _silu(x):
    return x * jax.nn.sigmoid(x)


def _const_spec(shape):
    nd = len(shape)
    return pl.BlockSpec(shape, lambda *_: (0,) * nd, pipeline_mode=pl.Buffered(1))


def _layer_spec(shape, layer):
    nd = len(shape)
    return pl.BlockSpec((None,) + tuple(shape), lambda *_: (layer,) + (0,) * nd,
                        pipeline_mode=pl.Buffered(1))


def _ret_proj_kernel(h_ref, g_ref, w_ref, cos_ref, sin_ref, o_ref):
    xn = _rms(h_ref[...], g_ref[...]).astype(BF16)
    cos = cos_ref[...]
    sin = sin_ref[...]
    half = RET_DK // 2
    hk = RET_HEADS * RET_DK
    hv = RET_HEADS * RET_DV
    for hd in range(2 * RET_HEADS):
        c0 = hd * RET_DK
        acc = _dot(xn, w_ref[:, c0:c0 + RET_DK])
        scale = RET_DK ** -0.5 if hd >= RET_HEADS else 1.0
        t1 = acc[:, :half]
        t2 = acc[:, half:]
        o_ref[:, c0:c0 + half] = ((t1 * cos - t2 * sin) * scale).astype(BF16)
        o_ref[:, c0 + half:c0 + RET_DK] = ((t2 * cos + t1 * sin) * scale).astype(BF16)
    for c0 in range(2 * hk, 2 * hk + hv, PROJ_CHUNK):
        o_ref[:, c0:c0 + PROJ_CHUNK] = _dot(xn, w_ref[:, c0:c0 + PROJ_CHUNK]).astype(BF16)
    for c0 in range(2 * hk + hv, 2 * hk + 2 * hv, PROJ_CHUNK):
        o_ref[:, c0:c0 + PROJ_CHUNK] = _silu(_dot(xn, w_ref[:, c0:c0 + PROJ_CHUNK])).astype(BF16)


def _ret_proj(h, g, w, layer, cos, sin, seq):
    t, d = h.shape
    n = w.shape[2]
    tm = min(TM_PROJ, seq)
    spt = seq // tm
    return pl.pallas_call(
        _ret_proj_kernel,
        out_shape=jax.ShapeDtypeStruct((t, n), BF16),
        grid=(t // tm,),
        in_specs=[
            pl.BlockSpec((tm, d), lambda i: (i, 0)),
            _const_spec((1, d)),
            _layer_spec((d, n), layer),
            pl.BlockSpec((tm, RET_DK // 2), lambda i: (i % spt, 0)),
            pl.BlockSpec((tm, RET_DK // 2), lambda i: (i % spt, 0)),
        ],
        out_specs=pl.BlockSpec((tm, n), lambda i: (i, 0)),
        compiler_params=_params(("parallel",)),
        name="ret_proj",
    )(h, g, w, cos, sin)


def _mix_ffn_kernel(y_ref, wo_ref, h_ref, g_ref, wgu_ref, wd_ref, gf_ref, o_ref, *, final_norm):
    h = h_ref[...] + _dot(y_ref[...], wo_ref[...])
    xn = _rms(h, g_ref[...]).astype(BF16)
    acc = h
    for f in range(0, D_FF, FFN_CHUNK):
        gate = _dot(xn, wgu_ref[:, f:f + FFN_CHUNK])
        up = _dot(xn, wgu_ref[:, D_FF + f:D_FF + f + FFN_CHUNK])
        act = (_silu(gate) * up).astype(BF16)
        acc = acc + _dot(act, wd_ref[f:f + FFN_CHUNK, :])
    if final_norm:
        acc = _rms(acc, gf_ref[...])
    o_ref[...] = acc


def _mix_ffn(y, wo, wo_layer, h, g, wgu, wd, ffn_layer, gf, final_norm):
    t, d = h.shape
    k = y.shape[1]
    tm = min(TM_FFN, t)
    return pl.pallas_call(
        functools.partial(_mix_ffn_kernel, final_norm=final_norm),
        out_shape=jax.ShapeDtypeStruct((t, d), F32),
        grid=(t // tm,),
        in_specs=[
            pl.BlockSpec((tm, k), lambda i: (i, 0)),
            _layer_spec((k, d), wo_layer),
            pl.BlockSpec((tm, d), lambda i: (i, 0)),
            _const_spec((1, d)),
            _layer_spec((d, 2 * D_FF), ffn_layer),
            _layer_spec((D_FF, d), ffn_layer),
            _const_spec((1, d)),
        ],
        out_specs=pl.BlockSpec((tm, d), lambda i: (i, 0)),
        compiler_params=_params(("parallel",)),
        name="mix_ffn",
    )(y, wo, h, g, wgu, wd, gf)


def _ret_log_gamma(head):
    return math.log1p(-2.0 ** (-5.0 - head))


def _ret_chunk_kernel(q_ref, k_ref, v_ref, g_ref, gn_ref, o_ref, st_ref):
    c = pl.program_id(1)
    ln = q_ref.shape[0]

    @pl.when(c == 0)
    def _():
        st_ref[...] = jnp.zeros_like(st_ref)

    row = lax.broadcasted_iota(jnp.int32, (ln, ln), 0)
    col = lax.broadcasted_iota(jnp.int32, (ln, ln), 1)
    rel = (row - col).astype(F32)
    pos = lax.broadcasted_iota(jnp.int32, (ln, 1), 0).astype(F32)
    heads = range(RET_HEADS)
    lg = [_ret_log_gamma(hd) for hd in heads]
    q = [q_ref[:, hd * RET_DK:(hd + 1) * RET_DK] for hd in heads]
    k = [k_ref[:, hd * RET_DK:(hd + 1) * RET_DK] for hd in heads]
    v = [v_ref[:, hd * RET_DV:(hd + 1) * RET_DV] for hd in heads]
    st = [st_ref[hd] for hd in heads]
    qk = [_dot_nt(q[hd], k[hd]) for hd in heads]
    inter = [_dot(q[hd], st[hd].astype(BF16)) for hd in heads]
    k_dec = [(k[hd].astype(F32) * jnp.exp(lg[hd] * (ln - 1.0 - pos))).astype(BF16) for hd in heads]
    scores = [(qk[hd] * jnp.where(rel >= 0, jnp.exp(lg[hd] * jnp.maximum(rel, 0.0)), 0.0)).astype(BF16)
              for hd in heads]
    out = [_dot(scores[hd], v[hd]) + inter[hd] * jnp.exp(lg[hd] * (pos + 1.0)) for hd in heads]
    for hd in heads:
        st_ref[hd] = st[hd] * math.exp(lg[hd] * ln) + _dot_tn(k_dec[hd], v[hd])
    for hd in heads:
        sl = slice(hd * RET_DV, (hd + 1) * RET_DV)
        ms = jnp.mean(out[hd] * out[hd], axis=-1, keepdims=True)
        y = out[hd] * lax.rsqrt(ms + RET_EPS) * gn_ref[:, sl]
        o_ref[:, sl] = (y * g_ref[:, sl].astype(F32)).astype(BF16)


def _ret_chunk(proj, gn, batch, seq):
    t = proj.shape[0]
    ln = min(RET_CHUNK, seq)
    nc = seq // ln
    hk = RET_HEADS * RET_DK
    hv = RET_HEADS * RET_DV
    return pl.pallas_call(
        _ret_chunk_kernel,
        out_shape=jax.ShapeDtypeStruct((t, hv), BF16),
        grid=(batch, nc),
        in_specs=[
            pl.BlockSpec((ln, hk), lambda b, c: (b * nc + c, 0)),
            pl.BlockSpec((ln, hk), lambda b, c: (b * nc + c, 1)),
            pl.BlockSpec((ln, hv), lambda b, c: (b * nc + c, 1)),
            pl.BlockSpec((ln, hv), lambda b, c: (b * nc + c, 2)),
            pl.BlockSpec((1, hv), lambda b, c: (0, 0)),
        ],
        out_specs=pl.BlockSpec((ln, hv), lambda b, c: (b * nc + c, 0)),
        scratch_shapes=[pltpu.VMEM((RET_HEADS, RET_DK, RET_DV), F32)],
        compiler_params=_params(("parallel", "arbitrary")),
        name="ret_chunk",
    )(proj, proj, proj, proj, gn)


def _prev_rows(cur, prev8, shift):
    ext = jnp.concatenate([prev8, cur], axis=0)
    return pltpu.roll(ext, shift, axis=0)[V7X_SUBLANES:, :]


def _log_sigmoid(x):
    return jnp.minimum(x, 0.0) - jnp.log1p(jnp.exp(-jnp.abs(x)))


def _ml_front_kernel(h_ref, g_ref, w_ref, cw_ref, cb_ref, mq_ref, mk_ref, mv_ref, wgt_ref, bg_ref,
                     c_ref, q_ref, k_ref, v_ref, z_ref, gt_ref, tail_ref, *, tiles_per_seq):
    i = pl.program_id(0)
    tm = h_ref.shape[0]

    @pl.when((i % tiles_per_seq) == 0)
    def _():
        tail_ref[...] = jnp.zeros_like(tail_ref)

    xn = _rms(h_ref[...], g_ref[...]).astype(BF16)
    blk = mq_ref.shape[1]
    gates = jnp.zeros((tm, V7X_LANES), F32) + bg_ref[...]
    for c0 in range(0, ML_INNER, PROJ_CHUNK):
        cs = slice(c0, c0 + PROJ_CHUNK)
        u = _dot(xn, w_ref[:, cs])
        z_ref[:, cs] = _silu(_dot(xn, w_ref[:, ML_INNER + c0:ML_INNER + c0 + PROJ_CHUNK])).astype(BF16)
        prev = tail_ref[:, cs]
        tail_ref[:, cs] = u[tm - V7X_SUBLANES:, :]
        conv = u * cw_ref[ML_CONV - 1:ML_CONV, cs] + cb_ref[:, cs]
        for s in range(1, ML_CONV):
            conv = conv + _prev_rows(u, prev, s) * cw_ref[ML_CONV - 1 - s:ML_CONV - s, cs]
        c16 = _silu(conv).astype(BF16)
        u16 = u.astype(BF16)
        c_ref[:, cs] = c16
        for n in range(PROJ_CHUNK // blk):
            ls = slice(n * blk, (n + 1) * blk)
            sl = slice(c0 + n * blk, c0 + (n + 1) * blk)
            tile = sl.start // blk
            qn = _dot(c16[:, ls], mq_ref[tile]).astype(BF16)
            kn = _dot(c16[:, ls], mk_ref[tile]).astype(BF16)
            vn = _dot(u16[:, ls], mv_ref[tile]).astype(BF16)
            q_ref[:, sl] = qn
            k_ref[:, sl] = (kn.astype(F32) * (ML_DH ** -0.5)).astype(BF16)
            v_ref[:, sl] = vn
            gates = (gates + _dot(qn, wgt_ref[0, sl, :]) + _dot(kn, wgt_ref[1, sl, :])
                     + _dot(vn, wgt_ref[2, sl, :]))
    lane = lax.broadcasted_iota(jnp.int32, gates.shape, 1)
    gt_ref[...] = jnp.where(jnp.logical_and(lane >= ML_HEADS, lane < 2 * ML_HEADS),
                            _log_sigmoid(gates), gates)


def _ml_front(h, g, w, conv_w, conv_b, mq, mk, mv, wgt, bg, seq):
    t, d = h.shape
    tm = min(TM_PROJ, seq)
    tps = seq // tm
    act = jax.ShapeDtypeStruct((t, ML_INNER), BF16)
    row_spec = pl.BlockSpec((tm, ML_INNER), lambda i: (i, 0))
    return pl.pallas_call(
        functools.partial(_ml_front_kernel, tiles_per_seq=tps),
        out_shape=(act, act, act, act, act, jax.ShapeDtypeStruct((t, V7X_LANES), F32)),
        grid=(t // tm,),
        in_specs=[
            pl.BlockSpec((tm, d), lambda i: (i, 0)),
            _const_spec((1, d)),
            _const_spec(w.shape),
            _const_spec((ML_CONV, ML_INNER)),
            _const_spec((1, ML_INNER)),
            _const_spec(mq.shape), _const_spec(mk.shape), _const_spec(mv.shape),
            _const_spec(wgt.shape),
            _const_spec((1, V7X_LANES)),
        ],
        out_specs=(row_spec, row_spec, row_spec, row_spec, row_spec,
                   pl.BlockSpec((tm, V7X_LANES), lambda i: (i, 0))),
        scratch_shapes=[pltpu.VMEM((V7X_SUBLANES, ML_INNER), F32)],
        compiler_params=_params(("arbitrary",)),
        name="ml_front",
    )(h, g, w, conv_w, conv_b, mq, mk, mv, wgt, bg)


def _ml_chunk_kernel(q_ref, k_ref, v_ref, gt_ref, c_ref, z_ref, gn_ref, sk_ref, o_ref,
                     cst_ref, nst_ref, mst_ref):
    ci = pl.program_id(1)
    ln = q_ref.shape[0]

    @pl.when(ci == 0)
    def _():
        cst_ref[...] = jnp.zeros_like(cst_ref)
        nst_ref[...] = jnp.zeros_like(nst_ref)
        mst_ref[...] = jnp.zeros_like(mst_ref)

    row = lax.broadcasted_iota(jnp.int32, (ln, ln), 0)
    col = lax.broadcasted_iota(jnp.int32, (ln, ln), 1)
    causal = col <= row
    gates = gt_ref[...]
    cum = _dot_exact(causal.astype(F32), gates)
    gates_t = gates.T
    cum_t = cum.T
    heads = range(ML_HEADS)
    sls = [slice(hd * ML_DH, (hd + 1) * ML_DH) for hd in heads]
    q = [q_ref[:, sl] for sl in sls]
    k = [k_ref[:, sl] for sl in sls]
    v = [v_ref[:, sl] for sl in sls]
    cst = [cst_ref[hd] for hd in heads]
    n_row = [nst_ref[hd:hd + 1, :] for hd in heads]
    m_prev = [mst_ref[hd:hd + 1, 0:1] for hd in heads]
    i_col = [gates[:, hd:hd + 1] for hd in heads]
    b_col = [cum[:, ML_HEADS + hd:ML_HEADS + hd + 1] for hd in heads]
    qk = [_dot_nt(q[hd], k[hd]) for hd in heads]
    inter = [_dot(q[hd], cst[hd].astype(BF16)) for hd in heads]
    log_d = [jnp.where(causal, b_col[hd] - cum_t[ML_HEADS + hd:ML_HEADS + hd + 1, :] + gates_t[hd:hd + 1, :],
                       -jnp.inf) for hd in heads]
    log_inter = [b_col[hd] + m_prev[hd] for hd in heads]
    m_row = [jnp.maximum(log_inter[hd], jnp.max(log_d[hd], axis=-1, keepdims=True)) for hd in heads]
    w_inter = [jnp.exp(log_inter[hd] - m_row[hd]) for hd in heads]
    s = [qk[hd] * jnp.exp(log_d[hd] - m_row[hd]) for hd in heads]
    num = [_dot(s[hd].astype(BF16), v[hd]) + w_inter[hd] * inter[hd] for hd in heads]
    den = [jnp.sum(s[hd], axis=-1, keepdims=True) + w_inter[hd] * jnp.sum(
        q[hd].astype(F32) * n_row[hd], axis=-1, keepdims=True) for hd in heads]
    hh = [num[hd] / jnp.maximum(jnp.abs(den[hd]), jnp.exp(-m_row[hd])) for hd in heads]
    b_last = [b_col[hd][ln - 1:ln, :] for hd in heads]
    log_w = [b_last[hd] - b_col[hd] + i_col[hd] for hd in heads]
    m_new = [jnp.maximum(b_last[hd] + m_prev[hd], jnp.max(log_w[hd], axis=0, keepdims=True)) for hd in heads]
    kw = [k[hd].astype(F32) * jnp.exp(log_w[hd] - m_new[hd]) for hd in heads]
    carry = [jnp.exp(b_last[hd] + m_prev[hd] - m_new[hd]) for hd in heads]
    for hd in heads:
        cst_ref[hd] = carry[hd] * cst[hd] + _dot_tn(kw[hd].astype(BF16), v[hd])
        nst_ref[hd:hd + 1, :] = carry[hd] * n_row[hd] + jnp.sum(kw[hd], axis=0, keepdims=True)
        mst_ref[hd:hd + 1, :] = jnp.broadcast_to(m_new[hd], (1, V7X_LANES))
    for hd in heads:
        sl = sls[hd]
        hc = hh[hd] - jnp.mean(hh[hd], axis=-1, keepdims=True)
        y = hc * lax.rsqrt(jnp.mean(hc * hc, axis=-1, keepdims=True) + ML_EPS) * gn_ref[:, sl]
        y = (y + sk_ref[:, sl] * c_ref[:, sl].astype(F32)) * z_ref[:, sl].astype(F32)
        o_ref[:, sl] = y.astype(BF16)


def _ml_chunk(q, k, v, gates, c, z, gn, skip, batch, seq):
    t = q.shape[0]
    ln = min(ML_CHUNK, seq)
    nc = seq // ln
    row_spec = pl.BlockSpec((ln, ML_INNER), lambda b, ci: (b * nc + ci, 0))
    return pl.pallas_call(
        _ml_chunk_kernel,
        out_shape=jax.ShapeDtypeStruct((t, ML_INNER), BF16),
        grid=(batch, nc),
        in_specs=[
            row_spec, row_spec, row_spec,
            pl.BlockSpec((ln, V7X_LANES), lambda b, ci: (b * nc + ci, 0)),
            row_spec,
            row_spec,
            pl.BlockSpec((1, ML_INNER), lambda b, ci: (0, 0)),
            pl.BlockSpec((1, ML_INNER), lambda b, ci: (0, 0)),
        ],
        out_specs=row_spec,
        scratch_shapes=[
            pltpu.VMEM((ML_HEADS, ML_DH, ML_DH), F32),
            pltpu.VMEM((V7X_SUBLANES, ML_DH), F32),
            pltpu.VMEM((V7X_SUBLANES, V7X_LANES), F32),
        ],
        compiler_params=_params(("parallel", "arbitrary")),
        name="ml_chunk",
    )(q, k, v, gates, c, z, gn, skip)


def _rw_pre_kernel(h_ref, hp_ref, g_ref, mu_ref, wrkv_ref, la_ref, lbw_ref, lba_ref, lbg_ref,
                   w0_ref, a0_ref, kk_ref, ka_ref,
                   r_out, lw_out, k_out, v_out, kk_out, kb_out, g_out, *, tiles_per_seq,
                   lora_w, lora_a):
    i = pl.program_id(0)
    g = g_ref[...]
    xn = _rms(h_ref[...], g)
    first = (i % tiles_per_seq) == 0
    xp8 = jnp.where(first, 0.0, _rms(hp_ref[...], g))
    dx = _prev_rows(xn, xp8, 1) - xn

    def mix(n):
        return (xn + dx * mu_ref[n:n + 1, :]).astype(BF16)

    r = _dot(mix(0), wrkv_ref[0])
    k = _dot(mix(1), wrkv_ref[1])
    v = _dot(mix(2), wrkv_ref[2])
    hw = jnp.tanh(_dot(mix(3), la_ref[:, 0:lora_w])).astype(BF16)
    ha = _dot(mix(4), la_ref[:, lora_w:lora_w + lora_a]).astype(BF16)
    hg = jax.nn.sigmoid(_dot(mix(5), la_ref[:, lora_w + lora_a:])).astype(BF16)
    w_pre = w0_ref[...] + _dot(hw, lbw_ref[...])
    w_log = jnp.minimum(w_pre, 0.0) - jnp.log1p(jnp.exp(-jnp.abs(w_pre))) - 0.5
    lw_out[...] = -jnp.exp(w_log)
    alpha = jax.nn.sigmoid(a0_ref[...] + _dot(ha, lba_ref[...]))
    g_out[...] = _dot(hg, lbg_ref[...]).astype(BF16)

    kk = k * kk_ref[...]
    r_out[...] = r.astype(BF16)
    k_out[...] = (k * (1.0 + (alpha - 1.0) * ka_ref[...])).astype(BF16)
    v_out[...] = v.astype(BF16)
    kk_out[...] = kk.astype(BF16)
    kb_out[...] = (kk * alpha).astype(BF16)


def _rw_pre(h, g, mu, wrkv, la, lbw, lba, lbg, w0, a0, k_k, k_a, seq, lora_w, lora_a):
    t, d = h.shape
    tm = min(TM_PRE, seq)
    tps = seq // tm
    rb = tm // V7X_SUBLANES
    row_spec = pl.BlockSpec((tm, d), lambda i: (i, 0))
    act = jax.ShapeDtypeStruct((t, d), BF16)
    vec = _const_spec((1, d))
    return pl.pallas_call(
        functools.partial(_rw_pre_kernel, tiles_per_seq=tps, lora_w=lora_w, lora_a=lora_a),
        out_shape=(act, jax.ShapeDtypeStruct((t, d), F32), act, act, act, act, act),
        grid=(t // tm,),
        in_specs=[
            row_spec,
            pl.BlockSpec((V7X_SUBLANES, d), lambda i: (jnp.maximum(i * rb - 1, 0), 0)),
            vec,
            _const_spec(mu.shape),
            _const_spec(wrkv.shape),
            _const_spec(la.shape),
            _const_spec(lbw.shape), _const_spec(lba.shape), _const_spec(lbg.shape),
            vec, vec, vec, vec,
        ],
        out_specs=(row_spec,) * 7,
        compiler_params=_params(("parallel",)),
        name="rw_pre",
    )(h, h, g, mu, wrkv, la, lbw, lba, lbg, w0, a0, k_k, k_a)


def _rw_chunk_kernel(r_ref, lw_ref, k_ref, v_ref, kk_ref, kb_ref, g_ref, rk_ref, gng_ref, gnb_ref,
                     o_ref, st_ref):
    ci = pl.program_id(1)
    nb, ln, _ = r_ref.shape
    pair = 2 * RW_DH

    @pl.when(ci == 0)
    def _():
        st_ref[...] = jnp.zeros_like(st_ref)

    trow = lax.broadcasted_iota(jnp.int32, (ln, ln), 0)
    tcol = lax.broadcasted_iota(jnp.int32, (ln, ln), 1)
    tril = (tcol <= trow).astype(F32)
    lane = lax.broadcasted_iota(jnp.int32, (ln, pair), 1)
    head0 = lane < RW_DH
    srow = lax.broadcasted_iota(jnp.int32, (2 * ln, 2 * ln), 0)
    scol = lax.broadcasted_iota(jnp.int32, (2 * ln, 2 * ln), 1)
    strict_t = srow < scol
    incl_t = srow <= scol

    def stack(x):
        return jnp.concatenate([jnp.where(head0, x, 0.0), jnp.where(head0, 0.0, x)], axis=0).astype(BF16)

    chains = [(bi, p) for bi in range(nb) for p in range(RW_HEADS // 2)]
    pairs = range(len(chains))
    sls = [slice(p * pair, (p + 1) * pair) for _, p in chains]
    def head_sums(x):
        s0 = jnp.sum(jnp.where(head0, x, 0.0), axis=-1, keepdims=True)
        s1 = jnp.sum(jnp.where(head0, 0.0, x), axis=-1, keepdims=True)
        return s0, s1

    pair_slices = [slice(p * pair, (p + 1) * pair) for p in range(RW_HEADS // 2)]
    ar_s, b_s, k_s, bk, v_t, g_last, bonus = [], [], [], [], [], [], []
    for bi in range(nb):
        lw = lw_ref[bi]
        cum = _dot_exact(tril, lw)
        cum_last = cum[ln - 1:ln, :]
        g_inv = jnp.exp(-cum)
        g_rem = jnp.exp(cum_last - cum)
        kk_raw = kk_ref[bi].astype(F32)
        r_all = r_ref[bi].astype(F32)
        kk = k_ref[bi].astype(F32)
        v_all = v_ref[bi].astype(F32)
        rkk = r_all * kk * rk_ref[...]
        inv_parts = []
        for sl in pair_slices:
            s0, s1 = head_sums(kk_raw[:, sl] * kk_raw[:, sl])
            inv_parts.append(jnp.where(head0, lax.rsqrt(jnp.maximum(s0, 1e-24)),
                                       lax.rsqrt(jnp.maximum(s1, 1e-24))))
            t0, t1 = head_sums(rkk[:, sl])
            bonus.append(jnp.where(head0, t0, t1) * v_all[:, sl])
        inv_all = jnp.concatenate(inv_parts, axis=1)
        kn = kk_raw * inv_all
        bb = kb_ref[bi].astype(F32) * inv_all
        a_all = -kn * jnp.exp(cum - lw)
        rt_all = r_all * jnp.exp(cum)
        bt_all = bb * g_inv
        kt_all = kk * g_inv
        bh_all = bb * g_rem
        kh_all = kk * g_rem
        gl = jnp.exp(cum_last)
        for sl in pair_slices:
            ar_s.append(jnp.concatenate([stack(a_all[:, sl]), stack(rt_all[:, sl])], axis=0))
            b_s.append(stack(bt_all[:, sl]))
            k_s.append(stack(kt_all[:, sl]))
            bk.append(jnp.concatenate([stack(bh_all[:, sl]), stack(kh_all[:, sl])], axis=0))
            vt = v_all[:, sl].T
            v_t.append(jnp.concatenate([vt[:RW_DH], vt[RW_DH:]], axis=1).astype(BF16))
            g_last.append(gl[:, sl])
    st = [st_ref[bi, p] for bi, p in chains]
    m_b = [_dot_nt(b_s[p], ar_s[p]) for p in pairs]
    m_k = [_dot_nt(k_s[p], ar_s[p]) for p in pairs]
    inter = [_dot_nt(st[p].astype(BF16), ar_s[p]) for p in pairs]
    pw = [jnp.where(strict_t, m_b[p][:, :2 * ln], 0.0) for p in pairs]
    ak = [jnp.where(strict_t, m_k[p][:, :2 * ln], 0.0).astype(BF16) for p in pairs]
    rb = [jnp.where(incl_t, m_b[p][:, 2 * ln:], 0.0).astype(BF16) for p in pairs]
    rk = [jnp.where(incl_t, m_k[p][:, 2 * ln:], 0.0).astype(BF16) for p in pairs]
    u = [inter[p][:, :2 * ln] + _dot(v_t[p], ak[p]) for p in pairs]
    n_steps = int(math.log2(ln))
    for step in range(n_steps):
        p16 = [pw[p].astype(BF16) for p in pairs]
        u = [u[p] + _dot(u[p].astype(BF16), p16[p]) for p in pairs]
        if step + 1 < n_steps:
            pw = [_dot(p16[p], p16[p]) for p in pairs]
    u16 = [u[p].astype(BF16) for p in pairs]
    y_t = [inter[p][:, 2 * ln:] + _dot(u16[p], rb[p]) + _dot(v_t[p], rk[p]) for p in pairs]
    for c, (bi, p) in enumerate(chains):
        sl = sls[c]
        uv = jnp.concatenate([u16[c], v_t[c]], axis=1)
        st_ref[bi, p] = st[c] * g_last[c] + _dot(uv, bk[c])
        yc = y_t[c] - jnp.mean(y_t[c], axis=0, keepdims=True)
        yn = (yc * lax.rsqrt(jnp.mean(yc * yc, axis=0, keepdims=True) + RW_GN_EPS)).T
        y_p = jnp.concatenate([yn[:ln], yn[ln:]], axis=1)
        out = y_p * gng_ref[:, sl] + gnb_ref[:, sl] + bonus[c]
        o_ref[bi, :, sl] = (out * g_ref[bi, :, sl].astype(F32)).astype(BF16)


def _rw_chunk(r, lw, k, v, kk, kb, g, r_k, gn_g, gn_b, batch, seq):
    t, d = r.shape
    ln = min(RW_CHUNK, seq)
    nc = seq // ln
    nb = math.gcd(batch, RW_BATCH_PER_STEP)
    row_spec = pl.BlockSpec((nb, ln, d), lambda bi, ci: (bi, ci, 0))
    vec = pl.BlockSpec((1, d), lambda bi, ci: (0, 0))
    acts = [x.reshape(batch, seq, d) for x in (r, lw, k, v, kk, kb, g)]
    out = pl.pallas_call(
        _rw_chunk_kernel,
        out_shape=jax.ShapeDtypeStruct((batch, seq, d), BF16),
        grid=(batch // nb, nc),
        in_specs=[row_spec] * 7 + [vec, vec, vec],
        out_specs=row_spec,
        scratch_shapes=[pltpu.VMEM((nb, RW_HEADS // 2, RW_DH, 2 * RW_DH), F32)],
        compiler_params=_params(("parallel", "arbitrary")),
        name="rw_chunk",
    )(*acts, r_k, gn_g, gn_b)
    return out.reshape(t, d)


def _rope_tables(seq):
    half = RET_DK // 2
    pos = jnp.arange(seq, dtype=F32)
    inv_freq = 1.0 / (ROPE_BASE ** jnp.linspace(0.0, 1.0, half, dtype=F32))
    ang = pos[:, None] * inv_freq[None, :]
    return jnp.cos(ang), jnp.sin(ang)


def _retention_mixer(h, norm_g, w_in16, layer, gn, batch, seq):
    cos, sin = _rope_tables(seq)
    proj = _ret_proj(h, norm_g[None, :], w_in16, layer, cos, sin, seq)
    return _ret_chunk(proj, gn[None, :].astype(F32), batch, seq)


def _block_diag_tiles(w, tile):
    nb, bs, _ = w.shape
    rows = w.reshape(nb * bs // tile, tile, bs)
    dense = jnp.tile(rows, (1, 1, tile // bs))
    idx = jnp.arange(tile) // bs
    return jnp.where(idx[:, None] == idx[None, :], dense, 0.0)


def _mlstm_mixer(h, norm_g, w_in, conv_w, conv_b, wq, wk, wv, w_gate, b_gate, gn, skip, batch, seq):
    tile = 2 * V7X_LANES
    mq = _block_diag_tiles(wq, tile).astype(BF16)
    mk = _block_diag_tiles(wk, tile).astype(BF16)
    mv = _block_diag_tiles(wv, tile).astype(BF16)
    n_gate = w_gate.shape[1]
    wgt = jnp.pad(w_gate.reshape(3, ML_INNER, n_gate), ((0, 0), (0, 0), (0, V7X_LANES - n_gate))).astype(BF16)
    bg = jnp.pad(b_gate, (0, V7X_LANES - n_gate))[None, :].astype(F32)
    c, q, k, v, z, gates = _ml_front(h, norm_g[None, :], w_in.astype(BF16), conv_w, conv_b[None, :],
                                     mq, mk, mv, wgt, bg, seq)
    return _ml_chunk(q, k, v, gates, c, z, gn[None, :], skip[None, :], batch, seq)


def _pad_to(x, axis, size):
    pad = [(0, 0)] * x.ndim
    pad[axis] = (0, size - x.shape[axis])
    return jnp.pad(x, pad)


def _rwkv_mixer(h, norm_g, mu, w_rkv, w0, w_la, w_lb, a0, a_la, a_lb, g_la, g_lb, k_k, k_a, r_k,
                gn_g, gn_b, batch, seq):
    d = h.shape[1]
    lw_ = -(-w_la.shape[1] // V7X_LANES) * V7X_LANES
    la_ = -(-a_la.shape[1] // V7X_LANES) * V7X_LANES
    lg_ = -(-g_la.shape[1] // V7X_LANES) * V7X_LANES
    la = jnp.concatenate([_pad_to(w_la, 1, lw_), _pad_to(a_la, 1, la_), _pad_to(g_la, 1, lg_)],
                         axis=1).astype(BF16)
    lbw = _pad_to(w_lb, 0, lw_).astype(BF16)
    lba = _pad_to(a_lb, 0, la_).astype(BF16)
    lbg = _pad_to(g_lb, 0, lg_).astype(BF16)
    r, lw, k, v, kk, kb, g = _rw_pre(h, norm_g[None, :], mu, w_rkv.astype(BF16), la, lbw, lba, lbg,
                                     w0[None, :], a0[None, :], k_k[None, :], k_a[None, :], seq, lw_, la_)
    return _rw_chunk(r, lw, k, v, kk, kb, g, r_k.reshape(1, d), gn_g[None, :], gn_b[None, :], batch, seq)


def kernel(x, norm_mix, norm_ffn, norm_final, ret_w_in, ret_gn, ret_w_out, ml_w_in, ml_conv_w, ml_conv_b, ml_wq, ml_wk, ml_wv, ml_w_gate, ml_b_gate, ml_gn, ml_skip, ml_w_out, rw_mu, rw_w_rkv, rw_w0, rw_w_lora_a, rw_w_lora_b, rw_a0, rw_a_lora_a, rw_a_lora_b, rw_g_lora_a, rw_g_lora_b, rw_k_k, rw_k_a, rw_r_k, rw_gn_g, rw_gn_b, rw_w_out, ffn_w_gu, ffn_w_down):
    batch, seq, d = x.shape
    depth = norm_mix.shape[0]
    h = x.reshape(batch * seq, d)
    ret_w_in16 = ret_w_in.astype(BF16)
    w_out16 = (ret_w_out.astype(BF16), ml_w_out.astype(BF16), rw_w_out.astype(BF16))
    w_gu16 = ffn_w_gu.astype(BF16)
    w_down16 = ffn_w_down.astype(BF16)
    for i in range(depth):
        kind, j = i % 3, i // 3
        if kind == 0:
            y = _retention_mixer(h, norm_mix[i], ret_w_in16, j, ret_gn[j], batch, seq)
        elif kind == 1:
            y = _mlstm_mixer(h, norm_mix[i], ml_w_in[j], ml_conv_w[j], ml_conv_b[j], ml_wq[j], ml_wk[j],
                             ml_wv[j], ml_w_gate[j], ml_b_gate[j], ml_gn[j], ml_skip[j], batch, seq)
        else:
            y = _rwkv_mixer(h, norm_mix[i], rw_mu[j], rw_w_rkv[j], rw_w0[j], rw_w_lora_a[j],
                            rw_w_lora_b[j], rw_a0[j], rw_a_lora_a[j], rw_a_lora_b[j], rw_g_lora_a[j],
                            rw_g_lora_b[j], rw_k_k[j], rw_k_a[j], rw_r_k[j], rw_gn_g[j], rw_gn_b[j],
                            batch, seq)
        h = _mix_ffn(y, w_out16[kind], j, h, norm_ffn[i][None, :], w_gu16, w_down16, i,
                     norm_final[None, :], i == depth - 1)
    return h.reshape(batch, seq, d)
```

```python
import functools
import math

import jax
import jax.numpy as jnp
from jax import lax
from jax.experimental import pallas as pl
from jax.experimental.pallas import tpu as pltpu

F32 = jnp.float32
BF16 = jnp.bfloat16

D_MODEL = 1024
D_FF = 2816
RMS_EPS = 1e-6

RET_HEADS = 4
RET_DK = 256
RET_DV = 512
RET_EPS = 1e-6
ROPE_BASE = 10000.0

ML_INNER = 2048
ML_HEADS = 4
ML_DH = 512
ML_CONV = 4
ML_EPS = 1e-6

RW_HEADS = 16
RW_DH = 64
RW_GN_EPS = 64e-5

V7X_LANES = 128
V7X_SUBLANES = 8
V7X_VMEM_LIMIT = 56 * 1024 * 1024

TM_PROJ = 512
PROJ_CHUNK = 512
TM_FFN = 1024
FFN_CHUNK = 256
TM_PRE = 256
RET_CHUNK = 256
ML_CHUNK = 256
RW_CHUNK = 64
RW_BATCH_PER_STEP = 4


def _params(sem, vmem=V7X_VMEM_LIMIT):
    return pltpu.CompilerParams(dimension_semantics=sem, vmem_limit_bytes=vmem)


def _dot(a, b):
    return jnp.dot(a, b, preferred_element_type=F32)


def _dot_nt(a, b):
    return lax.dot_general(a, b, (((1,), (1,)), ((), ())), preferred_element_type=F32)


def _dot_tn(a, b):
    return lax.dot_general(a, b, (((0,), (0,)), ((), ())), preferred_element_type=F32)


def _dot_exact(a, b):
    hi = b.astype(BF16)
    lo = (b - hi.astype(F32)).astype(BF16)
    a16 = a.astype(BF16)
    return _dot(a16, hi) + _dot(a16, lo)


def _rms(x, g):
    ms = jnp.mean(x * x, axis=-1, keepdims=True)
    return x * lax.rsqrt(ms + RMS_EPS) * g


def _silu(x):
    return x * jax.nn.sigmoid(x)


def _const_spec(shape):
    nd = len(shape)
    return pl.BlockSpec(shape, lambda *_: (0,) * nd, pipeline_mode=pl.Buffered(1))


def _layer_spec(shape, layer):
    nd = len(shape)
    return pl.BlockSpec((None,) + tuple(shape), lambda *_: (layer,) + (0,) * nd,
                        pipeline_mode=pl.Buffered(1))


def _ret_proj_kernel(h_ref, g_ref, w_ref, cos_ref, sin_ref, o_ref):
    xn = _rms(h_ref[...], g_ref[...]).astype(BF16)
    cos = cos_ref[...]
    sin = sin_ref[...]
    half = RET_DK // 2
    hk = RET_HEADS * RET_DK
    hv = RET_HEADS * RET_DV
    for hd in range(2 * RET_HEADS):
        c0 = hd * RET_DK
        acc = _dot(xn, w_ref[:, c0:c0 + RET_DK])
        scale = RET_DK ** -0.5 if hd >= RET_HEADS else 1.0
        t1 = acc[:, :half]
        t2 = acc[:, half:]
        o_ref[:, c0:c0 + half] = ((t1 * cos - t2 * sin) * scale).astype(BF16)
        o_ref[:, c0 + half:c0 + RET_DK] = ((t2 * cos + t1 * sin) * scale).astype(BF16)
    for c0 in range(2 * hk, 2 * hk + hv, PROJ_CHUNK):
        o_ref[:, c0:c0 + PROJ_CHUNK] = _dot(xn, w_ref[:, c0:c0 + PROJ_CHUNK]).astype(BF16)
    for c0 in range(2 * hk + hv, 2 * hk + 2 * hv, PROJ_CHUNK):
        o_ref[:, c0:c0 + PROJ_CHUNK] = _silu(_dot(xn, w_ref[:, c0:c0 + PROJ_CHUNK])).astype(BF16)


def _ret_proj(h, g, w, layer, cos, sin, seq):
    t, d = h.shape
    n = w.shape[2]
    tm = min(TM_PROJ, seq)
    spt = seq // tm
    return pl.pallas_call(
        _ret_proj_kernel,
        out_shape=jax.ShapeDtypeStruct((t, n), BF16),
        grid=(t // tm,),
        in_specs=[
            pl.BlockSpec((tm, d), lambda i: (i, 0)),
            _const_spec((1, d)),
            _layer_spec((d, n), layer),
            pl.BlockSpec((tm, RET_DK // 2), lambda i: (i % spt, 0)),
            pl.BlockSpec((tm, RET_DK // 2), lambda i: (i % spt, 0)),
        ],
        out_specs=pl.BlockSpec((tm, n), lambda i: (i, 0)),
        compiler_params=_params(("parallel",)),
        name="ret_proj",
    )(h, g, w, cos, sin)


def _mix_ffn_kernel(y_ref, wo_ref, h_ref, g_ref, wgu_ref, wd_ref, gf_ref, o_ref, *, final_norm):
    h = h_ref[...] + _dot(y_ref[...], wo_ref[...])
    xn = _rms(h, g_ref[...]).astype(BF16)
    acc = h
    for f in range(0, D_FF, FFN_CHUNK):
        gate = _dot(xn, wgu_ref[:, f:f + FFN_CHUNK])
        up = _dot(xn, wgu_ref[:, D_FF + f:D_FF + f + FFN_CHUNK])
        act = (_silu(gate) * up).astype(BF16)
        acc = acc + _dot(act, wd_ref[f:f + FFN_CHUNK, :])
    if final_norm:
        acc = _rms(acc, gf_ref[...])
    o_ref[...] = acc


def _mix_ffn(y, wo, wo_layer, h, g, wgu, wd, ffn_layer, gf, final_norm):
    t, d = h.shape
    k = y.shape[1]
    tm = min(TM_FFN, t)
    return pl.pallas_call(
        functools.partial(_mix_ffn_kernel, final_norm=final_norm),
        out_shape=jax.ShapeDtypeStruct((t, d), F32),
        grid=(t // tm,),
        in_specs=[
            pl.BlockSpec((tm, k), lambda i: (i, 0)),
            _layer_spec((k, d), wo_layer),
            pl.BlockSpec((tm, d), lambda i: (i, 0)),
            _const_spec((1, d)),
            _layer_spec((d, 2 * D_FF), ffn_layer),
            _layer_spec((D_FF, d), ffn_layer),
            _const_spec((1, d)),
        ],
        out_specs=pl.BlockSpec((tm, d), lambda i: (i, 0)),
        compiler_params=_params(("parallel",)),
        name="mix_ffn",
    )(y, wo, h, g, wgu, wd, gf)


def _ret_log_gamma(head):
    return math.log1p(-2.0 ** (-5.0 - head))


def _ret_chunk_kernel(q_ref, k_ref, v_ref, g_ref, gn_ref, o_ref, st_ref):
    c = pl.program_id(1)
    ln = q_ref.shape[0]

    @pl.when(c == 0)
    def _():
        st_ref[...] = jnp.zeros_like(st_ref)

    row = lax.broadcasted_iota(jnp.int32, (ln, ln), 0)
    col = lax.broadcasted_iota(jnp.int32, (ln, ln), 1)
    rel = (row - col).astype(F32)
    pos = lax.broadcasted_iota(jnp.int32, (ln, 1), 0).astype(F32)
    heads = range(RET_HEADS)
    lg = [_ret_log_gamma(hd) for hd in heads]
    q = [q_ref[:, hd * RET_DK:(hd + 1) * RET_DK] for hd in heads]
    k = [k_ref[:, hd * RET_DK:(hd + 1) * RET_DK] for hd in heads]
    v = [v_ref[:, hd * RET_DV:(hd + 1) * RET_DV] for hd in heads]
    st = [st_ref[hd] for hd in heads]
    qk = [_dot_nt(q[hd], k[hd]) for hd in heads]
    inter = [_dot(q[hd], st[hd].astype(BF16)) for hd in heads]
    k_dec = [(k[hd].astype(F32) * jnp.exp(lg[hd] * (ln - 1.0 - pos))).astype(BF16) for hd in heads]
    scores = [(qk[hd] * jnp.where(rel >= 0, jnp.exp(lg[hd] * jnp.maximum(rel, 0.0)), 0.0)).astype(BF16)
              for hd in heads]
    out = [_dot(scores[hd], v[hd]) + inter[hd] * jnp.exp(lg[hd] * (pos + 1.0)) for hd in heads]
    for hd in heads:
        st_ref[hd] = st[hd] * math.exp(lg[hd] * ln) + _dot_tn(k_dec[hd], v[hd])
    for hd in heads:
        sl = slice(hd * RET_DV, (hd + 1) * RET_DV)
        ms = jnp.mean(out[hd] * out[hd], axis=-1, keepdims=True)
        y = out[hd] * lax.rsqrt(ms + RET_EPS) * gn_ref[:, sl]
        o_ref[:, sl] = (y * g_ref[:, sl].astype(F32)).astype(BF16)


def _ret_chunk(proj, gn, batch, seq):
    t = proj.shape[0]
    ln = min(RET_CHUNK, seq)
    nc = seq // ln
    hk = RET_HEADS * RET_DK
    hv = RET_HEADS * RET_DV
    return pl.pallas_call(
        _ret_chunk_kernel,
        out_shape=jax.ShapeDtypeStruct((t, hv), BF16),
        grid=(batch, nc),
        in_specs=[
            pl.BlockSpec((ln, hk), lambda b, c: (b * nc + c, 0)),
            pl.BlockSpec((ln, hk), lambda b, c: (b * nc + c, 1)),
            pl.BlockSpec((ln, hv), lambda b, c: (b * nc + c, 1)),
            pl.BlockSpec((ln, hv), lambda b, c: (b * nc + c, 2)),
            pl.BlockSpec((1, hv), lambda b, c: (0, 0)),
        ],
        out_specs=pl.BlockSpec((ln, hv), lambda b, c: (b * nc + c, 0)),
        scratch_shapes=[pltpu.VMEM((RET_HEADS, RET_DK, RET_DV), F32)],
        compiler_params=_params(("parallel", "arbitrary")),
        name="ret_chunk",
    )(proj, proj, proj, proj, gn)


def _prev_rows(cur, prev8, shift):
    ext = jnp.concatenate([prev8, cur], axis=0)
    return pltpu.roll(ext, shift, axis=0)[V7X_SUBLANES:, :]


def _log_sigmoid(x):
    return jnp.minimum(x, 0.0) - jnp.log1p(jnp.exp(-jnp.abs(x)))


def _ml_front_kernel(h_ref, g_ref, w_ref, cw_ref, cb_ref, mq_ref, mk_ref, mv_ref, wgt_ref, bg_ref,
                     c_ref, q_ref, k_ref, v_ref, z_ref, gt_ref, tail_ref, *, tiles_per_seq):
    i = pl.program_id(0)
    tm = h_ref.shape[0]

    @pl.when((i % tiles_per_seq) == 0)
    def _():
        tail_ref[...] = jnp.zeros_like(tail_ref)

    xn = _rms(h_ref[...], g_ref[...]).astype(BF16)
    blk = mq_ref.shape[1]
    gates = jnp.zeros((tm, V7X_LANES), F32) + bg_ref[...]
    for c0 in range(0, ML_INNER, PROJ_CHUNK):
        cs = slice(c0, c0 + PROJ_CHUNK)
        u = _dot(xn, w_ref[:, cs])
        z_ref[:, cs] = _silu(_dot(xn, w_ref[:, ML_INNER + c0:ML_INNER + c0 + PROJ_CHUNK])).astype(BF16)
        prev = tail_ref[:, cs]
        tail_ref[:, cs] = u[tm - V7X_SUBLANES:, :]
        conv = u * cw_ref[ML_CONV - 1:ML_CONV, cs] + cb_ref[:, cs]
        for s in range(1, ML_CONV):
            conv = conv + _prev_rows(u, prev, s) * cw_ref[ML_CONV - 1 - s:ML_CONV - s, cs]
        c16 = _silu(conv).astype(BF16)
        u16 = u.astype(BF16)
        c_ref[:, cs] = c16
        for n in range(PROJ_CHUNK // blk):
            ls = slice(n * blk, (n + 1) * blk)
            sl = slice(c0 + n * blk, c0 + (n + 1) * blk)
            tile = sl.start // blk
            qn = _dot(c16[:, ls], mq_ref[tile]).astype(BF16)
            kn = _dot(c16[:, ls], mk_ref[tile]).astype(BF16)
            vn = _dot(u16[:, ls], mv_ref[tile]).astype(BF16)
            q_ref[:, sl] = qn
            k_ref[:, sl] = (kn.astype(F32) * (ML_DH ** -0.5)).astype(BF16)
            v_ref[:, sl] = vn
            gates = (gates + _dot(qn, wgt_ref[0, sl, :]) + _dot(kn, wgt_ref[1, sl, :])
                     + _dot(vn, wgt_ref[2, sl, :]))
    lane = lax.broadcasted_iota(jnp.int32, gates.shape, 1)
    gt_ref[...] = jnp.where(jnp.logical_and(lane >= ML_HEADS, lane < 2 * ML_HEADS),
                            _log_sigmoid(gates), gates)


def _ml_front(h, g, w, conv_w, conv_b, mq, mk, mv, wgt, bg, seq):
    t, d = h.shape
    tm = min(TM_PROJ, seq)
    tps = seq // tm
    act = jax.ShapeDtypeStruct((t, ML_INNER), BF16)
    row_spec = pl.BlockSpec((tm, ML_INNER), lambda i: (i, 0))
    return pl.pallas_call(
        functools.partial(_ml_front_kernel, tiles_per_seq=tps),
        out_shape=(act, act, act, act, act, jax.ShapeDtypeStruct((t, V7X_LANES), F32)),
        grid=(t // tm,),
        in_specs=[
            pl.BlockSpec((tm, d), lambda i: (i, 0)),
            _const_spec((1, d)),
            _const_spec(w.shape),
            _const_spec((ML_CONV, ML_INNER)),
            _const_spec((1, ML_INNER)),
            _const_spec(mq.shape), _const_spec(mk.shape), _const_spec(mv.shape),
            _const_spec(wgt.shape),
            _const_spec((1, V7X_LANES)),
        ],
        out_specs=(row_spec, row_spec, row_spec, row_spec, row_spec,
                   pl.BlockSpec((tm, V7X_LANES), lambda i: (i, 0))),
        scratch_shapes=[pltpu.VMEM((V7X_SUBLANES, ML_INNER), F32)],
        compiler_params=_params(("arbitrary",)),
        name="ml_front",
    )(h, g, w, conv_w, conv_b, mq, mk, mv, wgt, bg)


def _ml_chunk_kernel(q_ref, k_ref, v_ref, gt_ref, c_ref, z_ref, gn_ref, sk_ref, o_ref,
                     cst_ref, nst_ref, mst_ref):
    ci = pl.program_id(1)
    ln = q_ref.shape[0]

    @pl.when(ci == 0)
    def _():
        cst_ref[...] = jnp.zeros_like(cst_ref)
        nst_ref[...] = jnp.zeros_like(nst_ref)
        mst_ref[...] = jnp.zeros_like(mst_ref)

    row = lax.broadcasted_iota(jnp.int32, (ln, ln), 0)
    col = lax.broadcasted_iota(jnp.int32, (ln, ln), 1)
    causal = col <= row
    gates = gt_ref[...]
    cum = _dot_exact(causal.astype(F32), gates)
    gates_t = gates.T
    cum_t = cum.T
    heads = range(ML_HEADS)
    sls = [slice(hd * ML_DH, (hd + 1) * ML_DH) for hd in heads]
    q = [q_ref[:, sl] for sl in sls]
    k = [k_ref[:, sl] for sl in sls]
    v = [v_ref[:, sl] for sl in sls]
    cst = [cst_ref[hd] for hd in heads]
    n_row = [nst_ref[hd:hd + 1, :] for hd in heads]
    m_prev = [mst_ref[hd:hd + 1, 0:1] for hd in heads]
    i_col = [gates[:, hd:hd + 1] for hd in heads]
    b_col = [cum[:, ML_HEADS + hd:ML_HEADS + hd + 1] for hd in heads]
    qk = [_dot_nt(q[hd], k[hd]) for hd in heads]
    inter = [_dot(q[hd], cst[hd].astype(BF16)) for hd in heads]
    log_d = [jnp.where(causal, b_col[hd] - cum_t[ML_HEADS + hd:ML_HEADS + hd + 1, :] + gates_t[hd:hd + 1, :],
                       -jnp.inf) for hd in heads]
    log_inter = [b_col[hd] + m_prev[hd] for hd in heads]
    m_row = [jnp.maximum(log_inter[hd], jnp.max(log_d[hd], axis=-1, keepdims=True)) for hd in heads]
    w_inter = [jnp.exp(log_inter[hd] - m_row[hd]) for hd in heads]
    s = [qk[hd] * jnp.exp(log_d[hd] - m_row[hd]) for hd in heads]
    num = [_dot(s[hd].astype(BF16), v[hd]) + w_inter[hd] * inter[hd] for hd in heads]
    den = [jnp.sum(s[hd], axis=-1, keepdims=True) + w_inter[hd] * jnp.sum(
        q[hd].astype(F32) * n_row[hd], axis=-1, keepdims=True) for hd in heads]
    hh = [num[hd] / jnp.maximum(jnp.abs(den[hd]), jnp.exp(-m_row[hd])) for hd in heads]
    b_last = [b_col[hd][ln - 1:ln, :] for hd in heads]
    log_w = [b_last[hd] - b_col[hd] + i_col[hd] for hd in heads]
    m_new = [jnp.maximum(b_last[hd] + m_prev[hd], jnp.max(log_w[hd], axis=0, keepdims=True)) for hd in heads]
    kw = [k[hd].astype(F32) * jnp.exp(log_w[hd] - m_new[hd]) for hd in heads]
    carry = [jnp.exp(b_last[hd] + m_prev[hd] - m_new[hd]) for hd in heads]
    for hd in heads:
        cst_ref[hd] = carry[hd] * cst[hd] + _dot_tn(kw[hd].astype(BF16), v[hd])
        nst_ref[hd:hd + 1, :] = carry[hd] * n_row[hd] + jnp.sum(kw[hd], axis=0, keepdims=True)
        mst_ref[hd:hd + 1, :] = jnp.broadcast_to(m_new[hd], (1, V7X_LANES))
    for hd in heads:
        sl = sls[hd]
        hc = hh[hd] - jnp.mean(hh[hd], axis=-1, keepdims=True)
        y = hc * lax.rsqrt(jnp.mean(hc * hc, axis=-1, keepdims=True) + ML_EPS) * gn_ref[:, sl]
        y = (y + sk_ref[:, sl] * c_ref[:, sl].astype(F32)) * z_ref[:, sl].astype(F32)
        o_ref[:, sl] = y.astype(BF16)


def _ml_chunk(q, k, v, gates, c, z, gn, skip, batch, seq):
    t = q.shape[0]
    ln = min(ML_CHUNK, seq)
    nc = seq // ln
    row_spec = pl.BlockSpec((ln, ML_INNER), lambda b, ci: (b * nc + ci, 0))
    return pl.pallas_call(
        _ml_chunk_kernel,
        out_shape=jax.ShapeDtypeStruct((t, ML_INNER), BF16),
        grid=(batch, nc),
        in_specs=[
            row_spec, row_spec, row_spec,
            pl.BlockSpec((ln, V7X_LANES), lambda b, ci: (b * nc + ci, 0)),
            row_spec,
            row_spec,
            pl.BlockSpec((1, ML_INNER), lambda b, ci: (0, 0)),
            pl.BlockSpec((1, ML_INNER), lambda b, ci: (0, 0)),
        ],
        out_specs=row_spec,
        scratch_shapes=[
            pltpu.VMEM((ML_HEADS, ML_DH, ML_DH), F32),
            pltpu.VMEM((V7X_SUBLANES, ML_DH), F32),
            pltpu.VMEM((V7X_SUBLANES, V7X_LANES), F32),
        ],
        compiler_params=_params(("parallel", "arbitrary")),
        name="ml_chunk",
    )(q, k, v, gates, c, z, gn, skip)


def _rw_pre_kernel(h_ref, hp_ref, g_ref, mu_ref, wrkv_ref, la_ref, lbw_ref, lba_ref, lbg_ref,
                   w0_ref, a0_ref, kk_ref, ka_ref,
                   r_out, lw_out, k_out, v_out, kk_out, kb_out, g_out, *, tiles_per_seq,
                   lora_w, lora_a):
    i = pl.program_id(0)
    g = g_ref[...]
    xn = _rms(h_ref[...], g)
    first = (i % tiles_per_seq) == 0
    xp8 = jnp.where(first, 0.0, _rms(hp_ref[...], g))
    dx = _prev_rows(xn, xp8, 1) - xn

    def mix(n):
        return (xn + dx * mu_ref[n:n + 1, :]).astype(BF16)

    r = _dot(mix(0), wrkv_ref[0])
    k = _dot(mix(1), wrkv_ref[1])
    v = _dot(mix(2), wrkv_ref[2])
    hw = jnp.tanh(_dot(mix(3), la_ref[:, 0:lora_w])).astype(BF16)
    ha = _dot(mix(4), la_ref[:, lora_w:lora_w + lora_a]).astype(BF16)
    hg = jax.nn.sigmoid(_dot(mix(5), la_ref[:, lora_w + lora_a:])).astype(BF16)
    w_pre = w0_ref[...] + _dot(hw, lbw_ref[...])
    w_log = jnp.minimum(w_pre, 0.0) - jnp.log1p(jnp.exp(-jnp.abs(w_pre))) - 0.5
    lw_out[...] = -jnp.exp(w_log)
    alpha = jax.nn.sigmoid(a0_ref[...] + _dot(ha, lba_ref[...]))
    g_out[...] = _dot(hg, lbg_ref[...]).astype(BF16)

    kk = k * kk_ref[...]
    r_out[...] = r.astype(BF16)
    k_out[...] = (k * (1.0 + (alpha - 1.0) * ka_ref[...])).astype(BF16)
    v_out[...] = v.astype(BF16)
    kk_out[...] = kk.astype(BF16)
    kb_out[...] = (kk * alpha).astype(BF16)


def _rw_pre(h, g, mu, wrkv, la, lbw, lba, lbg, w0, a0, k_k, k_a, seq, lora_w, lora_a):
    t, d = h.shape
    tm = min(TM_PRE, seq)
    tps = seq // tm
    rb = tm // V7X_SUBLANES
    row_spec = pl.BlockSpec((tm, d), lambda i: (i, 0))
    act = jax.ShapeDtypeStruct((t, d), BF16)
    vec = _const_spec((1, d))
    return pl.pallas_call(
        functools.partial(_rw_pre_kernel, tiles_per_seq=tps, lora_w=lora_w, lora_a=lora_a),
        out_shape=(act, jax.ShapeDtypeStruct((t, d), F32), act, act, act, act, act),
        grid=(t // tm,),
        in_specs=[
            row_spec,
            pl.BlockSpec((V7X_SUBLANES, d), lambda i: (jnp.maximum(i * rb - 1, 0), 0)),
            vec,
            _const_spec(mu.shape),
            _const_spec(wrkv.shape),
            _const_spec(la.shape),
            _const_spec(lbw.shape), _const_spec(lba.shape), _const_spec(lbg.shape),
            vec, vec, vec, vec,
        ],
        out_specs=(row_spec,) * 7,
        compiler_params=_params(("parallel",)),
        name="rw_pre",
    )(h, h, g, mu, wrkv, la, lbw, lba, lbg, w0, a0, k_k, k_a)


def _rw_chunk_kernel(r_ref, lw_ref, k_ref, v_ref, kk_ref, kb_ref, g_ref, rk_ref, gng_ref, gnb_ref,
                     o_ref, st_ref):
    ci = pl.program_id(1)
    nb, ln, _ = r_ref.shape
    pair = 2 * RW_DH

    @pl.when(ci == 0)
    def _():
        st_ref[...] = jnp.zeros_like(st_ref)

    trow = lax.broadcasted_iota(jnp.int32, (ln, ln), 0)
    tcol = lax.broadcasted_iota(jnp.int32, (ln, ln), 1)
    tril = (tcol <= trow).astype(F32)
    lane = lax.broadcasted_iota(jnp.int32, (ln, pair), 1)
    head0 = lane < RW_DH
    srow = lax.broadcasted_iota(jnp.int32, (2 * ln, 2 * ln), 0)
    scol = lax.broadcasted_iota(jnp.int32, (2 * ln, 2 * ln), 1)
    strict_t = srow < scol
    incl_t = srow <= scol

    def stack(x):
        return jnp.concatenate([jnp.where(head0, x, 0.0), jnp.where(head0, 0.0, x)], axis=0).astype(BF16)

    chains = [(bi, p) for bi in range(nb) for p in range(RW_HEADS // 2)]
    pairs = range(len(chains))
    sls = [slice(p * pair, (p + 1) * pair) for _, p in chains]
    def head_sums(x):
        s0 = jnp.sum(jnp.where(head0, x, 0.0), axis=-1, keepdims=True)
        s1 = jnp.sum(jnp.where(head0, 0.0, x), axis=-1, keepdims=True)
        return s0, s1

    pair_slices = [slice(p * pair, (p + 1) * pair) for p in range(RW_HEADS // 2)]
    ar_s, b_s, k_s, bk, v_t, g_last, bonus = [], [], [], [], [], [], []
    for bi in range(nb):
        lw = lw_ref[bi]
        cum = _dot_exact(tril, lw)
        cum_last = cum[ln - 1:ln, :]
        g_inv = jnp.exp(-cum)
        g_rem = jnp.exp(cum_last - cum)
        kk_raw = kk_ref[bi].astype(F32)
        r_all = r_ref[bi].astype(F32)
        kk = k_ref[bi].astype(F32)
        v_all = v_ref[bi].astype(F32)
        rkk = r_all * kk * rk_ref[...]
        inv_parts = []
        for sl in pair_slices:
            s0, s1 = head_sums(kk_raw[:, sl] * kk_raw[:, sl])
            inv_parts.append(jnp.where(head0, lax.rsqrt(jnp.maximum(s0, 1e-24)),
                                       lax.rsqrt(jnp.maximum(s1, 1e-24))))
            t0, t1 = head_sums(rkk[:, sl])
            bonus.append(jnp.where(head0, t0, t1) * v_all[:, sl])
        inv_all = jnp.concatenate(inv_parts, axis=1)
        kn = kk_raw * inv_all
        bb = kb_ref[bi].astype(F32) * inv_all
        a_all = -kn * jnp.exp(cum - lw)
        rt_all = r_all * jnp.exp(cum)
        bt_all = bb * g_inv
        kt_all = kk * g_inv
        bh_all = bb * g_rem
        kh_all = kk * g_rem
        gl = jnp.exp(cum_last)
        for sl in pair_slices:
            ar_s.append(jnp.concatenate([stack(a_all[:, sl]), stack(rt_all[:, sl])], axis=0))
            b_s.append(stack(bt_all[:, sl]))
            k_s.append(stack(kt_all[:, sl]))
            bk.append(jnp.concatenate([stack(bh_all[:, sl]), stack(kh_all[:, sl])], axis=0))
            vt = v_all[:, sl].T
            v_t.append(jnp.concatenate([vt[:RW_DH], vt[RW_DH:]], axis=1).astype(BF16))
            g_last.append(gl[:, sl])
    st = [st_ref[bi, p] for bi, p in chains]
    m_all = [_dot_nt(jnp.concatenate([b_s[p], k_s[p], st[p].astype(BF16)], axis=0), ar_s[p]) for p in pairs]
    m_b = [m[:2 * ln] for m in m_all]
    m_k = [m[2 * ln:4 * ln] for m in m_all]
    inter = [m[4 * ln:] for m in m_all]
    pw = [jnp.where(strict_t, m_b[p][:, :2 * ln], 0.0) for p in pairs]
    ak = [jnp.where(strict_t, m_k[p][:, :2 * ln], 0.0).astype(BF16) for p in pairs]
    rb = [jnp.where(incl_t, m_b[p][:, 2 * ln:], 0.0).astype(BF16) for p in pairs]
    rk = [jnp.where(incl_t, m_k[p][:, 2 * ln:], 0.0).astype(BF16) for p in pairs]
    u = [inter[p][:, :2 * ln] + _dot(v_t[p], ak[p]) for p in pairs]
    n_steps = int(math.log2(ln))
    for step in range(n_steps):
        p16 = [pw[p].astype(BF16) for p in pairs]
        if step + 1 < n_steps:
            both = [_dot(jnp.concatenate([u[p].astype(BF16), p16[p]], axis=0), p16[p]) for p in pairs]
            u = [u[p] + both[p][:RW_DH] for p in pairs]
            pw = [both[p][RW_DH:] for p in pairs]
        else:
            u = [u[p] + _dot(u[p].astype(BF16), p16[p]) for p in pairs]
    uv = [jnp.concatenate([u[p].astype(BF16), v_t[p]], axis=1) for p in pairs]
    y_t = [inter[p][:, 2 * ln:] + _dot(uv[p], jnp.concatenate([rb[p], rk[p]], axis=0))
           for p in pairs]
    for c, (bi, p) in enumerate(chains):
        sl = sls[c]
        st_ref[bi, p] = st[c] * g_last[c] + _dot(uv[c], bk[c])
        yc = y_t[c] - jnp.mean(y_t[c], axis=0, keepdims=True)
        yn = (yc * lax.rsqrt(jnp.mean(yc * yc, axis=0, keepdims=True) + RW_GN_EPS)).T
        y_p = jnp.concatenate([yn[:ln], yn[ln:]], axis=1)
        out = y_p * gng_ref[:, sl] + gnb_ref[:, sl] + bonus[c]
        o_ref[bi, :, sl] = (out * g_ref[bi, :, sl].astype(F32)).astype(BF16)


def _rw_chunk(r, lw, k, v, kk, kb, g, r_k, gn_g, gn_b, batch, seq):
    t, d = r.shape
    ln = min(RW_CHUNK, seq)
    nc = seq // ln
    nb = math.gcd(batch, RW_BATCH_PER_STEP)
    row_spec = pl.BlockSpec((nb, ln, d), lambda bi, ci: (bi, ci, 0))
    vec = pl.BlockSpec((1, d), lambda bi, ci: (0, 0))
    acts = [x.reshape(batch, seq, d) for x in (r, lw, k, v, kk, kb, g)]
    out = pl.pallas_call(
        _rw_chunk_kernel,
        out_shape=jax.ShapeDtypeStruct((batch, seq, d), BF16),
        grid=(batch // nb, nc),
        in_specs=[row_spec] * 7 + [vec, vec, vec],
        out_specs=row_spec,
        scratch_shapes=[pltpu.VMEM((nb, RW_HEADS // 2, RW_DH, 2 * RW_DH), F32)],
        compiler_params=_params(("parallel", "arbitrary")),
        name="rw_chunk",
    )(*acts, r_k, gn_g, gn_b)
    return out.reshape(t, d)


def _rope_tables(seq):
    half = RET_DK // 2
    pos = jnp.arange(seq, dtype=F32)
    inv_freq = 1.0 / (ROPE_BASE ** jnp.linspace(0.0, 1.0, half, dtype=F32))
    ang = pos[:, None] * inv_freq[None, :]
    return jnp.cos(ang), jnp.sin(ang)


def _retention_mixer(h, norm_g, w_in16, layer, gn, batch, seq):
    cos, sin = _rope_tables(seq)
    proj = _ret_proj(h, norm_g[None, :], w_in16, layer, cos, sin, seq)
    return _ret_chunk(proj, gn[None, :].astype(F32), batch, seq)


def _block_diag_tiles(w, tile):
    nb, bs, _ = w.shape
    rows = w.reshape(nb * bs // tile, tile, bs)
    dense = jnp.tile(rows, (1, 1, tile // bs))
    idx = jnp.arange(tile) // bs
    return jnp.where(idx[:, None] == idx[None, :], dense, 0.0)


def _mlstm_mixer(h, norm_g, w_in, conv_w, conv_b, wq, wk, wv, w_gate, b_gate, gn, skip, batch, seq):
    tile = 2 * V7X_LANES
    mq = _block_diag_tiles(wq, tile).astype(BF16)
    mk = _block_diag_tiles(wk, tile).astype(BF16)
    mv = _block_diag_tiles(wv, tile).astype(BF16)
    n_gate = w_gate.shape[1]
    wgt = jnp.pad(w_gate.reshape(3, ML_INNER, n_gate), ((0, 0), (0, 0), (0, V7X_LANES - n_gate))).astype(BF16)
    bg = jnp.pad(b_gate, (0, V7X_LANES - n_gate))[None, :].astype(F32)
    c, q, k, v, z, gates = _ml_front(h, norm_g[None, :], w_in.astype(BF16), conv_w, conv_b[None, :],
                                     mq, mk, mv, wgt, bg, seq)
    return _ml_chunk(q, k, v, gates, c, z, gn[None, :], skip[None, :], batch, seq)


def _pad_to(x, axis, size):
    pad = [(0, 0)] * x.ndim
    pad[axis] = (0, size - x.shape[axis])
    return jnp.pad(x, pad)


def _rwkv_mixer(h, norm_g, mu, w_rkv, w0, w_la, w_lb, a0, a_la, a_lb, g_la, g_lb, k_k, k_a, r_k,
                gn_g, gn_b, batch, seq):
    d = h.shape[1]
    lw_ = -(-w_la.shape[1] // V7X_LANES) * V7X_LANES
    la_ = -(-a_la.shape[1] // V7X_LANES) * V7X_LANES
    lg_ = -(-g_la.shape[1] // V7X_LANES) * V7X_LANES
    la = jnp.concatenate([_pad_to(w_la, 1, lw_), _pad_to(a_la, 1, la_), _pad_to(g_la, 1, lg_)],
                         axis=1).astype(BF16)
    lbw = _pad_to(w_lb, 0, lw_).astype(BF16)
    lba = _pad_to(a_lb, 0, la_).astype(BF16)
    lbg = _pad_to(g_lb, 0, lg_).astype(BF16)
    r, lw, k, v, kk, kb, g = _rw_pre(h, norm_g[None, :], mu, w_rkv.astype(BF16), la, lbw, lba, lbg,
                                     w0[None, :], a0[None, :], k_k[None, :], k_a[None, :], seq, lw_, la_)
    return _rw_chunk(r, lw, k, v, kk, kb, g, r_k.reshape(1, d), gn_g[None, :], gn_b[None, :], batch, seq)


def kernel(x, norm_mix, norm_ffn, norm_final, ret_w_in, ret_gn, ret_w_out, ml_w_in, ml_conv_w, ml_conv_b, ml_wq, ml_wk, ml_wv, ml_w_gate, ml_b_gate, ml_gn, ml_skip, ml_w_out, rw_mu, rw_w_rkv, rw_w0, rw_w_lora_a, rw_w_lora_b, rw_a0, rw_a_lora_a, rw_a_lora_b, rw_g_lora_a, rw_g_lora_b, rw_k_k, rw_k_a, rw_r_k, rw_gn_g, rw_gn_b, rw_w_out, ffn_w_gu, ffn_w_down):
    batch, seq, d = x.shape
    depth = norm_mix.shape[0]
    h = x.reshape(batch * seq, d)
    ret_w_in16 = ret_w_in.astype(BF16)
    w_out16 = (ret_w_out.astype(BF16), ml_w_out.astype(BF16), rw_w_out.astype(BF16))
    w_gu16 = ffn_w_gu.astype(BF16)
    w_down16 = ffn_w_down.astype(BF16)
    for i in range(depth):
        kind, j = i % 3, i // 3
        if kind == 0:
            y = _retention_mixer(h, norm_mix[i], ret_w_in16, j, ret_gn[j], batch, seq)
        elif kind == 1:
            y = _mlstm_mixer(h, norm_mix[i], ml_w_in[j], ml_conv_w[j], ml_conv_b[j], ml_wq[j], ml_wk[j],
                             ml_wv[j], ml_w_gate[j], ml_b_gate[j], ml_gn[j], ml_skip[j], batch, seq)
        else:
            y = _rwkv_mixer(h, norm_mix[i], rw_mu[j], rw_w_rkv[j], rw_w0[j], rw_w_lora_a[j],
                            rw_w_lora_b[j], rw_a0[j], rw_a_lora_a[j], rw_a_lora_b[j], rw_g_lora_a[j],
                            rw_g_lora_b[j], rw_k_k[j], rw_k_a[j], rw_r_k[j], rw_gn_g[j], rw_gn_b[j],
                            batch, seq)
        h = _mix_ffn(y, w_out16[kind], j, h, norm_ffn[i][None, :], w_gu16, w_down16, i,
                     norm_final[None, :], i == depth - 1)
    return h.reshape(batch, seq, d)
```

```python
import functools
import math

import jax
import jax.numpy as jnp
from jax import lax
from jax.experimental import pallas as pl
from jax.experimental.pallas import tpu as pltpu

F32 = jnp.float32
BF16 = jnp.bfloat16

D_MODEL = 1024
D_FF = 2816
RMS_EPS = 1e-6

RET_HEADS = 4
RET_DK = 256
RET_DV = 512
RET_EPS = 1e-6
ROPE_BASE = 10000.0

ML_INNER = 2048
ML_HEADS = 4
ML_DH = 512
ML_CONV = 4
ML_EPS = 1e-6

RW_HEADS = 16
RW_DH = 64
RW_GN_EPS = 64e-5

V7X_LANES = 128
V7X_SUBLANES = 8
V7X_VMEM_LIMIT = 56 * 1024 * 1024

TM_PROJ = 512
TM_RET_PROJ = 1024
PROJ_CHUNK = 512
TM_FFN = 1024
FFN_CHUNK = 256
TM_PRE = 256
RET_CHUNK = 256
ML_CHUNK = 256
ML_BATCH_PER_STEP = 2
RW_CHUNK = 64
RW_BATCH_PER_STEP = 4


def _params(sem, vmem=V7X_VMEM_LIMIT):
    return pltpu.CompilerParams(dimension_semantics=sem, vmem_limit_bytes=vmem)


def _dot(a, b):
    return jnp.dot(a, b, preferred_element_type=F32)


def _dot_nt(a, b):
    return lax.dot_general(a, b, (((1,), (1,)), ((), ())), preferred_element_type=F32)


def _dot_tn(a, b):
    return lax.dot_general(a, b, (((0,), (0,)), ((), ())), preferred_element_type=F32)


def _dot_exact(a, b):
    hi = b.astype(BF16)
    lo = (b - hi.astype(F32)).astype(BF16)
    a16 = a.astype(BF16)
    return _dot(a16, hi) + _dot(a16, lo)


def _rms(x, g):
    ms = jnp.mean(x * x, axis=-1, keepdims=True)
    return x * lax.rsqrt(ms + RMS_EPS) * g


def _silu(x):
    return x * jax.nn.sigmoid(x)


def _const_spec(shape):
    nd = len(shape)
    return pl.BlockSpec(shape, lambda *_: (0,) * nd, pipeline_mode=pl.Buffered(1))


def _layer_spec(shape, layer):
    nd = len(shape)
    return pl.BlockSpec((None,) + tuple(shape), lambda *_: (layer,) + (0,) * nd,
                        pipeline_mode=pl.Buffered(1))


def _ret_proj_kernel(h_ref, g_ref, w_ref, cos_ref, sin_ref, o_ref):
    xn = _rms(h_ref[...], g_ref[...]).astype(BF16)
    cos = cos_ref[...]
    sin = sin_ref[...]
    half = RET_DK // 2
    hk = RET_HEADS * RET_DK
    hv = RET_HEADS * RET_DV
    for hd in range(2 * RET_HEADS):
        c0 = hd * RET_DK
        acc = _dot(xn, w_ref[:, c0:c0 + RET_DK])
        scale = RET_DK ** -0.5 if hd >= RET_HEADS else 1.0
        t1 = acc[:, :half]
        t2 = acc[:, half:]
        o_ref[:, c0:c0 + half] = ((t1 * cos - t2 * sin) * scale).astype(BF16)
        o_ref[:, c0 + half:c0 + RET_DK] = ((t2 * cos + t1 * sin) * scale).astype(BF16)
    for c0 in range(2 * hk, 2 * hk + hv, PROJ_CHUNK):
        o_ref[:, c0:c0 + PROJ_CHUNK] = _dot(xn, w_ref[:, c0:c0 + PROJ_CHUNK]).astype(BF16)
    for c0 in range(2 * hk + hv, 2 * hk + 2 * hv, PROJ_CHUNK):
        o_ref[:, c0:c0 + PROJ_CHUNK] = _silu(_dot(xn, w_ref[:, c0:c0 + PROJ_CHUNK])).astype(BF16)


def _ret_proj(h, g, w, layer, cos, sin, seq):
    t, d = h.shape
    n = w.shape[2]
    tm = min(TM_RET_PROJ, seq)
    spt = seq // tm
    return pl.pallas_call(
        _ret_proj_kernel,
        out_shape=jax.ShapeDtypeStruct((t, n), BF16),
        grid=(t // tm,),
        in_specs=[
            pl.BlockSpec((tm, d), lambda i: (i, 0)),
            _const_spec((1, d)),
            _layer_spec((d, n), layer),
            pl.BlockSpec((tm, RET_DK // 2), lambda i: (i % spt, 0)),
            pl.BlockSpec((tm, RET_DK // 2), lambda i: (i % spt, 0)),
        ],
        out_specs=pl.BlockSpec((tm, n), lambda i: (i, 0)),
        compiler_params=_params(("parallel",)),
        name="ret_proj",
    )(h, g, w, cos, sin)


def _mix_ffn_kernel(y_ref, wo_ref, h_ref, g_ref, wgu_ref, wd_ref, gf_ref, o_ref, *, final_norm):
    h = h_ref[...] + _dot(y_ref[...], wo_ref[...])
    xn = _rms(h, g_ref[...]).astype(BF16)
    acc = h
    for f in range(0, D_FF, FFN_CHUNK):
        gate = _dot(xn, wgu_ref[:, f:f + FFN_CHUNK])
        up = _dot(xn, wgu_ref[:, D_FF + f:D_FF + f + FFN_CHUNK])
        act = (_silu(gate) * up).astype(BF16)
        acc = acc + _dot(act, wd_ref[f:f + FFN_CHUNK, :])
    if final_norm:
        acc = _rms(acc, gf_ref[...])
    o_ref[...] = acc


def _mix_ffn(y, wo, wo_layer, h, g, wgu, wd, ffn_layer, gf, final_norm):
    t, d = h.shape
    k = y.shape[1]
    tm = min(TM_FFN, t)
    return pl.pallas_call(
        functools.partial(_mix_ffn_kernel, final_norm=final_norm),
        out_shape=jax.ShapeDtypeStruct((t, d), F32),
        grid=(t // tm,),
        in_specs=[
            pl.BlockSpec((tm, k), lambda i: (i, 0)),
            _layer_spec((k, d), wo_layer),
            pl.BlockSpec((tm, d), lambda i: (i, 0)),
            _const_spec((1, d)),
            _layer_spec((d, 2 * D_FF), ffn_layer),
            _layer_spec((D_FF, d), ffn_layer),
            _const_spec((1, d)),
        ],
        out_specs=pl.BlockSpec((tm, d), lambda i: (i, 0)),
        compiler_params=_params(("parallel",)),
        name="mix_ffn",
    )(y, wo, h, g, wgu, wd, gf)


def _ret_log_gamma(head):
    return math.log1p(-2.0 ** (-5.0 - head))


def _ret_chunk_kernel(q_ref, k_ref, v_ref, g_ref, gn_ref, o_ref, st_ref):
    c = pl.program_id(1)
    ln = q_ref.shape[0]

    @pl.when(c == 0)
    def _():
        st_ref[...] = jnp.zeros_like(st_ref)

    row = lax.broadcasted_iota(jnp.int32, (ln, ln), 0)
    col = lax.broadcasted_iota(jnp.int32, (ln, ln), 1)
    rel = (row - col).astype(F32)
    pos = lax.broadcasted_iota(jnp.int32, (ln, 1), 0).astype(F32)
    heads = range(RET_HEADS)
    lg = [_ret_log_gamma(hd) for hd in heads]
    q = [q_ref[:, hd * RET_DK:(hd + 1) * RET_DK] for hd in heads]
    k = [k_ref[:, hd * RET_DK:(hd + 1) * RET_DK] for hd in heads]
    v = [v_ref[:, hd * RET_DV:(hd + 1) * RET_DV] for hd in heads]
    st = [st_ref[hd] for hd in heads]
    qk = [_dot_nt(q[hd], k[hd]) for hd in heads]
    inter = [_dot(q[hd], st[hd].astype(BF16)) for hd in heads]
    k_dec = [(k[hd].astype(F32) * jnp.exp(lg[hd] * (ln - 1.0 - pos))).astype(BF16) for hd in heads]
    scores = [(qk[hd] * jnp.where(rel >= 0, jnp.exp(lg[hd] * jnp.maximum(rel, 0.0)), 0.0)).astype(BF16)
              for hd in heads]
    out = [_dot(scores[hd], v[hd]) + inter[hd] * jnp.exp(lg[hd] * (pos + 1.0)) for hd in heads]
    for hd in heads:
        st_ref[hd] = st[hd] * math.exp(lg[hd] * ln) + _dot_tn(k_dec[hd], v[hd])
    for hd in heads:
        sl = slice(hd * RET_DV, (hd + 1) * RET_DV)
        ms = jnp.mean(out[hd] * out[hd], axis=-1, keepdims=True)
        y = out[hd] * lax.rsqrt(ms + RET_EPS) * gn_ref[:, sl]
        o_ref[:, sl] = (y * g_ref[:, sl].astype(F32)).astype(BF16)


def _ret_chunk(proj, gn, batch, seq):
    t = proj.shape[0]
    ln = min(RET_CHUNK, seq)
    nc = seq // ln
    hk = RET_HEADS * RET_DK
    hv = RET_HEADS * RET_DV
    return pl.pallas_call(
        _ret_chunk_kernel,
        out_shape=jax.ShapeDtypeStruct((t, hv), BF16),
        grid=(batch, nc),
        in_specs=[
            pl.BlockSpec((ln, hk), lambda b, c: (b * nc + c, 0)),
            pl.BlockSpec((ln, hk), lambda b, c: (b * nc + c, 1)),
            pl.BlockSpec((ln, hv), lambda b, c: (b * nc + c, 1)),
            pl.BlockSpec((ln, hv), lambda b, c: (b * nc + c, 2)),
            pl.BlockSpec((1, hv), lambda b, c: (0, 0)),
        ],
        out_specs=pl.BlockSpec((ln, hv), lambda b, c: (b * nc + c, 0)),
        scratch_shapes=[pltpu.VMEM((RET_HEADS, RET_DK, RET_DV), F32)],
        compiler_params=_params(("parallel", "arbitrary")),
        name="ret_chunk",
    )(proj, proj, proj, proj, gn)


def _prev_rows(cur, prev8, shift):
    ext = jnp.concatenate([prev8, cur], axis=0)
    return pltpu.roll(ext, shift, axis=0)[V7X_SUBLANES:, :]


def _log_sigmoid(x):
    return jnp.minimum(x, 0.0) - jnp.log1p(jnp.exp(-jnp.abs(x)))


def _ml_front_kernel(h_ref, g_ref, w_ref, cw_ref, cb_ref, mq_ref, mk_ref, mv_ref, wgt_ref, bg_ref,
                     c_ref, q_ref, k_ref, v_ref, z_ref, gt_ref, tail_ref, *, tiles_per_seq):
    i = pl.program_id(0)
    tm = h_ref.shape[0]

    @pl.when((i % tiles_per_seq) == 0)
    def _():
        tail_ref[...] = jnp.zeros_like(tail_ref)

    xn = _rms(h_ref[...], g_ref[...]).astype(BF16)
    blk = mq_ref.shape[1]
    gates = jnp.zeros((tm, V7X_LANES), F32) + bg_ref[...]
    for c0 in range(0, ML_INNER, PROJ_CHUNK):
        cs = slice(c0, c0 + PROJ_CHUNK)
        u = _dot(xn, w_ref[:, cs])
        z_ref[:, cs] = _silu(_dot(xn, w_ref[:, ML_INNER + c0:ML_INNER + c0 + PROJ_CHUNK])).astype(BF16)
        prev = tail_ref[:, cs]
        tail_ref[:, cs] = u[tm - V7X_SUBLANES:, :]
        conv = u * cw_ref[ML_CONV - 1:ML_CONV, cs] + cb_ref[:, cs]
        for s in range(1, ML_CONV):
            conv = conv + _prev_rows(u, prev, s) * cw_ref[ML_CONV - 1 - s:ML_CONV - s, cs]
        c16 = _silu(conv).astype(BF16)
        u16 = u.astype(BF16)
        c_ref[:, cs] = c16
        for n in range(PROJ_CHUNK // blk):
            ls = slice(n * blk, (n + 1) * blk)
            sl = slice(c0 + n * blk, c0 + (n + 1) * blk)
            tile = sl.start // blk
            qn = _dot(c16[:, ls], mq_ref[tile]).astype(BF16)
            kn = _dot(c16[:, ls], mk_ref[tile]).astype(BF16)
            vn = _dot(u16[:, ls], mv_ref[tile]).astype(BF16)
            q_ref[:, sl] = qn
            k_ref[:, sl] = (kn.astype(F32) * (ML_DH ** -0.5)).astype(BF16)
            v_ref[:, sl] = vn
            gates = (gates + _dot(qn, wgt_ref[0, sl, :]) + _dot(kn, wgt_ref[1, sl, :])
                     + _dot(vn, wgt_ref[2, sl, :]))
    lane = lax.broadcasted_iota(jnp.int32, gates.shape, 1)
    gt_ref[...] = jnp.where(jnp.logical_and(lane >= ML_HEADS, lane < 2 * ML_HEADS),
                            _log_sigmoid(gates), gates)


def _ml_front(h, g, w, conv_w, conv_b, mq, mk, mv, wgt, bg, seq):
    t, d = h.shape
    tm = min(TM_PROJ, seq)
    tps = seq // tm
    act = jax.ShapeDtypeStruct((t, ML_INNER), BF16)
    row_spec = pl.BlockSpec((tm, ML_INNER), lambda i: (i, 0))
    return pl.pallas_call(
        functools.partial(_ml_front_kernel, tiles_per_seq=tps),
        out_shape=(act, act, act, act, act, jax.ShapeDtypeStruct((t, V7X_LANES), F32)),
        grid=(t // tm,),
        in_specs=[
            pl.BlockSpec((tm, d), lambda i: (i, 0)),
            _const_spec((1, d)),
            _const_spec(w.shape),
            _const_spec((ML_CONV, ML_INNER)),
            _const_spec((1, ML_INNER)),
            _const_spec(mq.shape), _const_spec(mk.shape), _const_spec(mv.shape),
            _const_spec(wgt.shape),
            _const_spec((1, V7X_LANES)),
        ],
        out_specs=(row_spec, row_spec, row_spec, row_spec, row_spec,
                   pl.BlockSpec((tm, V7X_LANES), lambda i: (i, 0))),
        scratch_shapes=[pltpu.VMEM((V7X_SUBLANES, ML_INNER), F32)],
        compiler_params=_params(("arbitrary",)),
        name="ml_front",
    )(h, g, w, conv_w, conv_b, mq, mk, mv, wgt, bg)


def _ml_chunk_kernel(q_ref, k_ref, v_ref, gt_ref, c_ref, z_ref, gn_ref, sk_ref, o_ref,
                     cst_ref, nst_ref, mst_ref):
    ci = pl.program_id(1)
    nb, ln, _ = q_ref.shape

    @pl.when(ci == 0)
    def _():
        cst_ref[...] = jnp.zeros_like(cst_ref)
        nst_ref[...] = jnp.zeros_like(nst_ref)
        mst_ref[...] = jnp.zeros_like(mst_ref)

    row = lax.broadcasted_iota(jnp.int32, (ln, ln), 0)
    col = lax.broadcasted_iota(jnp.int32, (ln, ln), 1)
    causal = col <= row
    tril = causal.astype(F32)
    heads = range(ML_HEADS)
    sls = [slice(hd * ML_DH, (hd + 1) * ML_DH) for hd in heads]

    def row_stages(bi):
        st = {}

        def load():
            gates = gt_ref[bi]
            cum = _dot_exact(tril, gates)
            st["gates"], st["cum"], st["gates_t"], st["cum_t"] = gates, cum, gates.T, cum.T
            st["q"] = [q_ref[bi, :, sl] for sl in sls]
            st["k"] = [k_ref[bi, :, sl] for sl in sls]
            st["v"] = [v_ref[bi, :, sl] for sl in sls]
            st["cst"] = [cst_ref[bi, hd] for hd in heads]
            st["n_row"] = [nst_ref[bi, hd:hd + 1, :] for hd in heads]
            st["m_prev"] = [mst_ref[bi, hd:hd + 1, 0:1] for hd in heads]
            st["b_col"] = [cum[:, ML_HEADS + hd:ML_HEADS + hd + 1] for hd in heads]

        def scores():
            st["qk"] = [_dot_nt(st["q"][hd], st["k"][hd]) for hd in heads]
            st["inter"] = [_dot(st["q"][hd], st["cst"][hd].astype(BF16)) for hd in heads]

        def decay():
            b_col, m_prev = st["b_col"], st["m_prev"]
            log_d = [jnp.where(causal, b_col[hd] - st["cum_t"][ML_HEADS + hd:ML_HEADS + hd + 1, :]
                               + st["gates_t"][hd:hd + 1, :], -jnp.inf) for hd in heads]
            log_inter = [b_col[hd] + m_prev[hd] for hd in heads]
            st["m_row"] = [jnp.maximum(log_inter[hd], jnp.max(log_d[hd], axis=-1, keepdims=True)) for hd in heads]
            st["w_inter"] = [jnp.exp(log_inter[hd] - st["m_row"][hd]) for hd in heads]
            st["s"] = [st["qk"][hd] * jnp.exp(log_d[hd] - st["m_row"][hd]) for hd in heads]

        def readout():
            s, q = st["s"], st["q"]
            num = [_dot(s[hd].astype(BF16), st["v"][hd]) + st["w_inter"][hd] * st["inter"][hd] for hd in heads]
            den = [jnp.sum(s[hd], axis=-1, keepdims=True) + st["w_inter"][hd] * jnp.sum(
                q[hd].astype(F32) * st["n_row"][hd], axis=-1, keepdims=True) for hd in heads]
            st["hh"] = [num[hd] / jnp.maximum(jnp.abs(den[hd]), jnp.exp(-st["m_row"][hd])) for hd in heads]

        def state():
            b_col, m_prev = st["b_col"], st["m_prev"]
            for hd in heads:
                b_last = b_col[hd][ln - 1:ln, :]
                log_w = b_last - b_col[hd] + st["gates"][:, hd:hd + 1]
                m_new = jnp.maximum(b_last + m_prev[hd], jnp.max(log_w, axis=0, keepdims=True))
                kw = st["k"][hd].astype(F32) * jnp.exp(log_w - m_new)
                carry = jnp.exp(b_last + m_prev[hd] - m_new)
                cst_ref[bi, hd] = carry * st["cst"][hd] + _dot_tn(kw.astype(BF16), st["v"][hd])
                nst_ref[bi, hd:hd + 1, :] = carry * st["n_row"][hd] + jnp.sum(kw, axis=0, keepdims=True)
                mst_ref[bi, hd:hd + 1, :] = jnp.broadcast_to(m_new, (1, V7X_LANES))

        def epilogue(hd):
            sl = sls[hd]
            hh = st["hh"][hd]
            hc = hh - jnp.mean(hh, axis=-1, keepdims=True)
            y = hc * lax.rsqrt(jnp.mean(hc * hc, axis=-1, keepdims=True) + ML_EPS) * gn_ref[:, sl]
            y = (y + sk_ref[:, sl] * c_ref[bi, :, sl].astype(F32)) * z_ref[bi, :, sl].astype(F32)
            o_ref[bi, :, sl] = y.astype(BF16)

        return [load, scores, decay, readout, state], [functools.partial(epilogue, hd) for hd in heads]

    pending = []
    for bi in range(nb):
        stages, epilogues = row_stages(bi)
        for i, stage in enumerate(stages):
            stage()
            if i < len(pending):
                pending[i]()
        for piece in pending[len(stages):]:
            piece()
        pending = epilogues
    for piece in pending:
        piece()


def _ml_chunk(q, k, v, gates, c, z, gn, skip, batch, seq):
    t = q.shape[0]
    ln = min(ML_CHUNK, seq)
    nc = seq // ln
    nb = math.gcd(batch, ML_BATCH_PER_STEP)
    row_spec = pl.BlockSpec((nb, ln, ML_INNER), lambda b, ci: (b, ci, 0))
    vec = pl.BlockSpec((1, ML_INNER), lambda b, ci: (0, 0))
    acts = [x.reshape(batch, seq, ML_INNER) for x in (q, k, v)]
    out = pl.pallas_call(
        _ml_chunk_kernel,
        out_shape=jax.ShapeDtypeStruct((batch, seq, ML_INNER), BF16),
        grid=(batch // nb, nc),
        in_specs=[
            row_spec, row_spec, row_spec,
            pl.BlockSpec((nb, ln, V7X_LANES), lambda b, ci: (b, ci, 0)),
            row_spec,
            row_spec,
            vec, vec,
        ],
        out_specs=row_spec,
        scratch_shapes=[
            pltpu.VMEM((nb, ML_HEADS, ML_DH, ML_DH), F32),
            pltpu.VMEM((nb, V7X_SUBLANES, ML_DH), F32),
            pltpu.VMEM((nb, V7X_SUBLANES, V7X_LANES), F32),
        ],
        compiler_params=_params(("parallel", "arbitrary")),
        name="ml_chunk",
    )(*acts, gates.reshape(batch, seq, V7X_LANES), c.reshape(batch, seq, ML_INNER),
      z.reshape(batch, seq, ML_INNER), gn, skip)
    return out.reshape(t, ML_INNER)


def _rw_pre_kernel(h_ref, hp_ref, g_ref, mu_ref, wrkv_ref, la_ref, lbw_ref, lba_ref, lbg_ref,
                   w0_ref, a0_ref, kk_ref, ka_ref,
                   r_out, lw_out, k_out, v_out, kk_out, kb_out, g_out, *, tiles_per_seq,
                   lora_w, lora_a):
    i = pl.program_id(0)
    g = g_ref[...]
    xn = _rms(h_ref[...], g)
    first = (i % tiles_per_seq) == 0
    xp8 = jnp.where(first, 0.0, _rms(hp_ref[...], g))
    dx = _prev_rows(xn, xp8, 1) - xn

    def mix(n):
        return (xn + dx * mu_ref[n:n + 1, :]).astype(BF16)

    r = _dot(mix(0), wrkv_ref[0])
    k = _dot(mix(1), wrkv_ref[1])
    v = _dot(mix(2), wrkv_ref[2])
    hw = jnp.tanh(_dot(mix(3), la_ref[:, 0:lora_w])).astype(BF16)
    ha = _dot(mix(4), la_ref[:, lora_w:lora_w + lora_a]).astype(BF16)
    hg = jax.nn.sigmoid(_dot(mix(5), la_ref[:, lora_w + lora_a:])).astype(BF16)
    w_pre = w0_ref[...] + _dot(hw, lbw_ref[...])
    w_log = jnp.minimum(w_pre, 0.0) - jnp.log1p(jnp.exp(-jnp.abs(w_pre))) - 0.5
    lw_out[...] = -jnp.exp(w_log)
    alpha = jax.nn.sigmoid(a0_ref[...] + _dot(ha, lba_ref[...]))
    g_out[...] = _dot(hg, lbg_ref[...]).astype(BF16)

    kk = k * kk_ref[...]
    r_out[...] = r.astype(BF16)
    k_out[...] = (k * (1.0 + (alpha - 1.0) * ka_ref[...])).astype(BF16)
    v_out[...] = v.astype(BF16)
    kk_out[...] = kk.astype(BF16)
    kb_out[...] = (kk * alpha).astype(BF16)


def _rw_pre(h, g, mu, wrkv, la, lbw, lba, lbg, w0, a0, k_k, k_a, seq, lora_w, lora_a):
    t, d = h.shape
    tm = min(TM_PRE, seq)
    tps = seq // tm
    rb = tm // V7X_SUBLANES
    row_spec = pl.BlockSpec((tm, d), lambda i: (i, 0))
    act = jax.ShapeDtypeStruct((t, d), BF16)
    vec = _const_spec((1, d))
    return pl.pallas_call(
        functools.partial(_rw_pre_kernel, tiles_per_seq=tps, lora_w=lora_w, lora_a=lora_a),
        out_shape=(act, jax.ShapeDtypeStruct((t, d), F32), act, act, act, act, act),
        grid=(t // tm,),
        in_specs=[
            row_spec,
            pl.BlockSpec((V7X_SUBLANES, d), lambda i: (jnp.maximum(i * rb - 1, 0), 0)),
            vec,
            _const_spec(mu.shape),
            _const_spec(wrkv.shape),
            _const_spec(la.shape),
            _const_spec(lbw.shape), _const_spec(lba.shape), _const_spec(lbg.shape),
            vec, vec, vec, vec,
        ],
        out_specs=(row_spec,) * 7,
        compiler_params=_params(("parallel",)),
        name="rw_pre",
    )(h, h, g, mu, wrkv, la, lbw, lba, lbg, w0, a0, k_k, k_a)


def _rw_chunk_kernel(r_ref, lw_ref, k_ref, v_ref, kk_ref, kb_ref, g_ref, rk_ref, gng_ref, gnb_ref,
                     o_ref, st_ref):
    ci = pl.program_id(1)
    nb, ln, _ = r_ref.shape
    pair = 2 * RW_DH

    @pl.when(ci == 0)
    def _():
        st_ref[...] = jnp.zeros_like(st_ref)

    trow = lax.broadcasted_iota(jnp.int32, (ln, ln), 0)
    tcol = lax.broadcasted_iota(jnp.int32, (ln, ln), 1)
    tril = (tcol <= trow).astype(F32)
    lane = lax.broadcasted_iota(jnp.int32, (ln, pair), 1)
    head0 = lane < RW_DH
    srow = lax.broadcasted_iota(jnp.int32, (2 * ln, 2 * ln), 0)
    scol = lax.broadcasted_iota(jnp.int32, (2 * ln, 2 * ln), 1)
    strict_t = srow < scol
    incl_t = srow <= scol

    def stack(x):
        return jnp.concatenate([jnp.where(head0, x, 0.0), jnp.where(head0, 0.0, x)], axis=0).astype(BF16)

    chains = [(bi, p) for bi in range(nb) for p in range(RW_HEADS // 2)]
    pairs = range(len(chains))
    sls = [slice(p * pair, (p + 1) * pair) for _, p in chains]
    def head_sums(x):
        s0 = jnp.sum(jnp.where(head0, x, 0.0), axis=-1, keepdims=True)
        s1 = jnp.sum(jnp.where(head0, 0.0, x), axis=-1, keepdims=True)
        return s0, s1

    pair_slices = [slice(p * pair, (p + 1) * pair) for p in range(RW_HEADS // 2)]
    ar_s, b_s, k_s, bk, v_t, g_last, bonus = [], [], [], [], [], [], []
    for bi in range(nb):
        lw = lw_ref[bi]
        cum = _dot_exact(tril, lw)
        cum_last = cum[ln - 1:ln, :]
        g_inv = jnp.exp(-cum)
        g_rem = jnp.exp(cum_last - cum)
        kk_raw = kk_ref[bi].astype(F32)
        r_all = r_ref[bi].astype(F32)
        kk = k_ref[bi].astype(F32)
        v_all = v_ref[bi].astype(F32)
        rkk = r_all * kk * rk_ref[...]
        inv_parts = []
        for sl in pair_slices:
            s0, s1 = head_sums(kk_raw[:, sl] * kk_raw[:, sl])
            inv_parts.append(jnp.where(head0, lax.rsqrt(jnp.maximum(s0, 1e-24)),
                                       lax.rsqrt(jnp.maximum(s1, 1e-24))))
            t0, t1 = head_sums(rkk[:, sl])
            bonus.append(jnp.where(head0, t0, t1) * v_all[:, sl])
        inv_all = jnp.concatenate(inv_parts, axis=1)
        kn = kk_raw * inv_all
        bb = kb_ref[bi].astype(F32) * inv_all
        a_all = -kn * jnp.exp(cum - lw)
        rt_all = r_all * jnp.exp(cum)
        bt_all = bb * g_inv
        kt_all = kk * g_inv
        bh_all = bb * g_rem
        kh_all = kk * g_rem
        gl = jnp.exp(cum_last)
        for sl in pair_slices:
            ar_s.append(jnp.concatenate([stack(a_all[:, sl]), stack(rt_all[:, sl])], axis=0))
            b_s.append(stack(bt_all[:, sl]))
            k_s.append(stack(kt_all[:, sl]))
            bk.append(jnp.concatenate([stack(bh_all[:, sl]), stack(kh_all[:, sl])], axis=0))
            vt = v_all[:, sl].T
            v_t.append(jnp.concatenate([vt[:RW_DH], vt[RW_DH:]], axis=1).astype(BF16))
            g_last.append(gl[:, sl])
    st = [st_ref[bi, p] for bi, p in chains]
    m_all = [_dot_nt(jnp.concatenate([b_s[p], k_s[p], st[p].astype(BF16)], axis=0), ar_s[p]) for p in pairs]
    m_b = [m[:2 * ln] for m in m_all]
    m_k = [m[2 * ln:4 * ln] for m in m_all]
    inter = [m[4 * ln:] for m in m_all]
    pw = [jnp.where(strict_t, m_b[p][:, :2 * ln], 0.0) for p in pairs]
    ak = [jnp.where(strict_t, m_k[p][:, :2 * ln], 0.0).astype(BF16) for p in pairs]
    rb = [jnp.where(incl_t, m_b[p][:, 2 * ln:], 0.0).astype(BF16) for p in pairs]
    rk = [jnp.where(incl_t, m_k[p][:, 2 * ln:], 0.0).astype(BF16) for p in pairs]
    u = [inter[p][:, :2 * ln] + _dot(v_t[p], ak[p]) for p in pairs]
    n_steps = int(math.log2(ln))
    for step in range(n_steps):
        p16 = [pw[p].astype(BF16) for p in pairs]
        if step + 1 < n_steps:
            both = [_dot(jnp.concatenate([u[p].astype(BF16), p16[p]], axis=0), p16[p]) for p in pairs]
            u = [u[p] + both[p][:RW_DH] for p in pairs]
            pw = [both[p][RW_DH:] for p in pairs]
        else:
            u = [u[p] + _dot(u[p].astype(BF16), p16[p]) for p in pairs]
    uv = [jnp.concatenate([u[p].astype(BF16), v_t[p]], axis=1) for p in pairs]
    y_t = [inter[p][:, 2 * ln:] + _dot(uv[p], jnp.concatenate([rb[p], rk[p]], axis=0))
           for p in pairs]
    for c, (bi, p) in enumerate(chains):
        sl = sls[c]
        st_ref[bi, p] = st[c] * g_last[c] + _dot(uv[c], bk[c])
        yc = y_t[c] - jnp.mean(y_t[c], axis=0, keepdims=True)
        yn = (yc * lax.rsqrt(jnp.mean(yc * yc, axis=0, keepdims=True) + RW_GN_EPS)).T
        y_p = jnp.concatenate([yn[:ln], yn[ln:]], axis=1)
        out = y_p * gng_ref[:, sl] + gnb_ref[:, sl] + bonus[c]
        o_ref[bi, :, sl] = (out * g_ref[bi, :, sl].astype(F32)).astype(BF16)


def _rw_chunk(r, lw, k, v, kk, kb, g, r_k, gn_g, gn_b, batch, seq):
    t, d = r.shape
    ln = min(RW_CHUNK, seq)
    nc = seq // ln
    nb = math.gcd(batch, RW_BATCH_PER_STEP)
    row_spec = pl.BlockSpec((nb, ln, d), lambda bi, ci: (bi, ci, 0))
    vec = pl.BlockSpec((1, d), lambda bi, ci: (0, 0))
    acts = [x.reshape(batch, seq, d) for x in (r, lw, k, v, kk, kb, g)]
    out = pl.pallas_call(
        _rw_chunk_kernel,
        out_shape=jax.ShapeDtypeStruct((batch, seq, d), BF16),
        grid=(batch // nb, nc),
        in_specs=[row_spec] * 7 + [vec, vec, vec],
        out_specs=row_spec,
        scratch_shapes=[pltpu.VMEM((nb, RW_HEADS // 2, RW_DH, 2 * RW_DH), F32)],
        compiler_params=_params(("parallel", "arbitrary")),
        name="rw_chunk",
    )(*acts, r_k, gn_g, gn_b)
    return out.reshape(t, d)


def _rope_tables(seq):
    half = RET_DK // 2
    pos = jnp.arange(seq, dtype=F32)
    inv_freq = 1.0 / (ROPE_BASE ** jnp.linspace(0.0, 1.0, half, dtype=F32))
    ang = pos[:, None] * inv_freq[None, :]
    return jnp.cos(ang), jnp.sin(ang)


def _retention_mixer(h, norm_g, w_in16, layer, gn, batch, seq):
    cos, sin = _rope_tables(seq)
    proj = _ret_proj(h, norm_g[None, :], w_in16, layer, cos, sin, seq)
    return _ret_chunk(proj, gn[None, :].astype(F32), batch, seq)


def _block_diag_tiles(w, tile):
    nb, bs, _ = w.shape
    rows = w.reshape(nb * bs // tile, tile, bs)
    dense = jnp.tile(rows, (1, 1, tile // bs))
    idx = jnp.arange(tile) // bs
    return jnp.where(idx[:, None] == idx[None, :], dense, 0.0)


def _mlstm_mixer(h, norm_g, w_in, conv_w, conv_b, wq, wk, wv, w_gate, b_gate, gn, skip, batch, seq):
    tile = 2 * V7X_LANES
    mq = _block_diag_tiles(wq, tile).astype(BF16)
    mk = _block_diag_tiles(wk, tile).astype(BF16)
    mv = _block_diag_tiles(wv, tile).astype(BF16)
    n_gate = w_gate.shape[1]
    wgt = jnp.pad(w_gate.reshape(3, ML_INNER, n_gate), ((0, 0), (0, 0), (0, V7X_LANES - n_gate))).astype(BF16)
    bg = jnp.pad(b_gate, (0, V7X_LANES - n_gate))[None, :].astype(F32)
    c, q, k, v, z, gates = _ml_front(h, norm_g[None, :], w_in.astype(BF16), conv_w, conv_b[None, :],
                                     mq, mk, mv, wgt, bg, seq)
    return _ml_chunk(q, k, v, gates, c, z, gn[None, :], skip[None, :], batch, seq)


def _pad_to(x, axis, size):
    pad = [(0, 0)] * x.ndim
    pad[axis] = (0, size - x.shape[axis])
    return jnp.pad(x, pad)


def _rwkv_mixer(h, norm_g, mu, w_rkv, w0, w_la, w_lb, a0, a_la, a_lb, g_la, g_lb, k_k, k_a, r_k,
                gn_g, gn_b, batch, seq):
    d = h.shape[1]
    lw_ = -(-w_la.shape[1] // V7X_LANES) * V7X_LANES
    la_ = -(-a_la.shape[1] // V7X_LANES) * V7X_LANES
    lg_ = -(-g_la.shape[1] // V7X_LANES) * V7X_LANES
    la = jnp.concatenate([_pad_to(w_la, 1, lw_), _pad_to(a_la, 1, la_), _pad_to(g_la, 1, lg_)],
                         axis=1).astype(BF16)
    lbw = _pad_to(w_lb, 0, lw_).astype(BF16)
    lba = _pad_to(a_lb, 0, la_).astype(BF16)
    lbg = _pad_to(g_lb, 0, lg_).astype(BF16)
    r, lw, k, v, kk, kb, g = _rw_pre(h, norm_g[None, :], mu, w_rkv.astype(BF16), la, lbw, lba, lbg,
                                     w0[None, :], a0[None, :], k_k[None, :], k_a[None, :], seq, lw_, la_)
    return _rw_chunk(r, lw, k, v, kk, kb, g, r_k.reshape(1, d), gn_g[None, :], gn_b[None, :], batch, seq)


def kernel(x, norm_mix, norm_ffn, norm_final, ret_w_in, ret_gn, ret_w_out, ml_w_in, ml_conv_w, ml_conv_b, ml_wq, ml_wk, ml_wv, ml_w_gate, ml_b_gate, ml_gn, ml_skip, ml_w_out, rw_mu, rw_w_rkv, rw_w0, rw_w_lora_a, rw_w_lora_b, rw_a0, rw_a_lora_a, rw_a_lora_b, rw_g_lora_a, rw_g_lora_b, rw_k_k, rw_k_a, rw_r_k, rw_gn_g, rw_gn_b, rw_w_out, ffn_w_gu, ffn_w_down):
    batch, seq, d = x.shape
    depth = norm_mix.shape[0]
    h = x.reshape(batch * seq, d)
    ret_w_in16 = ret_w_in.astype(BF16)
    w_out16 = (ret_w_out.astype(BF16), ml_w_out.astype(BF16), rw_w_out.astype(BF16))
    w_gu16 = ffn_w_gu.astype(BF16)
    w_down16 = ffn_w_down.astype(BF16)
    for i in range(depth):
        kind, j = i % 3, i // 3
        if kind == 0:
            y = _retention_mixer(h, norm_mix[i], ret_w_in16, j, ret_gn[j], batch, seq)
        elif kind == 1:
            y = _mlstm_mixer(h, norm_mix[i], ml_w_in[j], ml_conv_w[j], ml_conv_b[j], ml_wq[j], ml_wk[j],
                             ml_wv[j], ml_w_gate[j], ml_b_gate[j], ml_gn[j], ml_skip[j], batch, seq)
        else:
            y = _rwkv_mixer(h, norm_mix[i], rw_mu[j], rw_w_rkv[j], rw_w0[j], rw_w_lora_a[j],
                            rw_w_lora_b[j], rw_a0[j], rw_a_lora_a[j], rw_a_lora_b[j], rw_g_lora_a[j],
                            rw_g_lora_b[j], rw_k_k[j], rw_k_a[j], rw_r_k[j], rw_gn_g[j], rw_gn_b[j],
                            batch, seq)
        h = _mix_ffn(y, w_out16[kind], j, h, norm_ffn[i][None, :], w_gu16, w_down16, i,
                     norm_final[None, :], i == depth - 1)
    return h.reshape(batch, seq, d)
```

```python
import functools
import math

import jax
import jax.numpy as jnp
from jax import lax
from jax.experimental import pallas as pl
from jax.experimental.pallas import tpu as pltpu

F32 = jnp.float32
BF16 = jnp.bfloat16

D_MODEL = 1024
D_FF = 2816
RMS_EPS = 1e-6

RET_HEADS = 4
RET_DK = 256
RET_DV = 512
RET_EPS = 1e-6
ROPE_BASE = 10000.0

ML_INNER = 2048
ML_HEADS = 4
ML_DH = 512
ML_CONV = 4
ML_EPS = 1e-6

RW_HEADS = 16
RW_DH = 64
RW_GN_EPS = 64e-5

V7X_LANES = 128
V7X_SUBLANES = 8
V7X_VMEM_LIMIT = 56 * 1024 * 1024

TM_PROJ = 512
TM_RET_PROJ = 512
PROJ_CHUNK = 512
TM_FFN = 1024
FFN_CHUNK = 256
TM_PRE = 512
RET_CHUNK = 256
ML_CHUNK = 256
ML_BATCH_PER_STEP = 2
RW_CHUNK = 64
RW_BATCH_PER_STEP = 4


def _params(sem, vmem=V7X_VMEM_LIMIT):
    return pltpu.CompilerParams(dimension_semantics=sem, vmem_limit_bytes=vmem)


def _dot(a, b):
    return jnp.dot(a, b, preferred_element_type=F32)


def _dot_nt(a, b):
    return lax.dot_general(a, b, (((1,), (1,)), ((), ())), preferred_element_type=F32)


def _dot_tn(a, b):
    return lax.dot_general(a, b, (((0,), (0,)), ((), ())), preferred_element_type=F32)


def _dot_exact(a, b):
    hi = b.astype(BF16)
    lo = (b - hi.astype(F32)).astype(BF16)
    a16 = a.astype(BF16)
    return _dot(a16, hi) + _dot(a16, lo)


def _rms(x, g):
    ms = jnp.mean(x * x, axis=-1, keepdims=True)
    return x * lax.rsqrt(ms + RMS_EPS) * g


def _silu(x):
    return x * jax.nn.sigmoid(x)


def _const_spec(shape):
    nd = len(shape)
    return pl.BlockSpec(shape, lambda *_: (0,) * nd, pipeline_mode=pl.Buffered(1))


def _layer_spec(shape, layer):
    nd = len(shape)
    return pl.BlockSpec((None,) + tuple(shape), lambda *_: (layer,) + (0,) * nd,
                        pipeline_mode=pl.Buffered(1))


def _ret_proj_kernel(h_ref, g_ref, w_ref, cos_ref, sin_ref, o_ref):
    xn = _rms(h_ref[...], g_ref[...]).astype(BF16)
    half = RET_DK // 2
    hk = RET_HEADS * RET_DK
    hv = RET_HEADS * RET_DV
    for hd in range(2 * RET_HEADS):
        c0 = hd * RET_DK
        acc = _dot(xn, w_ref[:, c0:c0 + RET_DK])
        cos = cos_ref[hd]
        sin = sin_ref[hd]
        t1 = acc[:, :half]
        t2 = acc[:, half:]
        o_ref[:, c0:c0 + half] = (t1 * cos - t2 * sin).astype(BF16)
        o_ref[:, c0 + half:c0 + RET_DK] = (t2 * cos + t1 * sin).astype(BF16)
    for c0 in range(2 * hk, 2 * hk + hv, PROJ_CHUNK):
        o_ref[:, c0:c0 + PROJ_CHUNK] = _dot(xn, w_ref[:, c0:c0 + PROJ_CHUNK]).astype(BF16)
    for c0 in range(2 * hk + hv, 2 * hk + 2 * hv, PROJ_CHUNK):
        o_ref[:, c0:c0 + PROJ_CHUNK] = _silu(_dot(xn, w_ref[:, c0:c0 + PROJ_CHUNK])).astype(BF16)


def _ret_proj(h, g, w, layer, cos, sin, seq):
    t, d = h.shape
    n = w.shape[2]
    tm = min(TM_RET_PROJ, seq)
    spt = seq // tm
    return pl.pallas_call(
        _ret_proj_kernel,
        out_shape=jax.ShapeDtypeStruct((t, n), BF16),
        grid=(t // tm,),
        in_specs=[
            pl.BlockSpec((tm, d), lambda i: (i, 0)),
            _const_spec((1, d)),
            _layer_spec((d, n), layer),
            pl.BlockSpec((2 * RET_HEADS, tm, RET_DK // 2), lambda i: (0, i % spt, 0)),
            pl.BlockSpec((2 * RET_HEADS, tm, RET_DK // 2), lambda i: (0, i % spt, 0)),
        ],
        out_specs=pl.BlockSpec((tm, n), lambda i: (i, 0)),
        compiler_params=_params(("parallel",)),
        name="ret_proj",
    )(h, g, w, cos, sin)


def _mix_ffn_kernel(y_ref, wo_ref, h_ref, g_ref, wgu_ref, wd_ref, gf_ref, o_ref, *, final_norm):
    h = h_ref[...] + _dot(y_ref[...], wo_ref[...])
    xn = _rms(h, g_ref[...]).astype(BF16)
    acc = h
    for f in range(0, D_FF, FFN_CHUNK):
        gate = _dot(xn, wgu_ref[:, f:f + FFN_CHUNK])
        up = _dot(xn, wgu_ref[:, D_FF + f:D_FF + f + FFN_CHUNK])
        act = (_silu(gate) * up).astype(BF16)
        acc = acc + _dot(act, wd_ref[f:f + FFN_CHUNK, :])
    if final_norm:
        acc = _rms(acc, gf_ref[...])
    o_ref[...] = acc


def _mix_ffn(y, wo, wo_layer, h, g, wgu, wd, ffn_layer, gf, final_norm):
    t, d = h.shape
    k = y.shape[1]
    tm = min(TM_FFN, t)
    return pl.pallas_call(
        functools.partial(_mix_ffn_kernel, final_norm=final_norm),
        out_shape=jax.ShapeDtypeStruct((t, d), F32),
        grid=(t // tm,),
        in_specs=[
            pl.BlockSpec((tm, k), lambda i: (i, 0)),
            _layer_spec((k, d), wo_layer),
            pl.BlockSpec((tm, d), lambda i: (i, 0)),
            _const_spec((1, d)),
            _layer_spec((d, 2 * D_FF), ffn_layer),
            _layer_spec((D_FF, d), ffn_layer),
            _const_spec((1, d)),
        ],
        out_specs=pl.BlockSpec((tm, d), lambda i: (i, 0)),
        compiler_params=_params(("parallel",)),
        name="mix_ffn",
    )(y, wo, h, g, wgu, wd, gf)


def _ret_log_gamma(head):
    return math.log1p(-2.0 ** (-5.0 - head))


def _ret_chunk_kernel(q_ref, k_ref, v_ref, g_ref, gn_ref, o_ref, st_ref):
    c = pl.program_id(1)
    ln = q_ref.shape[0]

    @pl.when(c == 0)
    def _():
        st_ref[...] = jnp.zeros_like(st_ref)

    causal = (lax.broadcasted_iota(jnp.int32, (ln, ln), 1) <= lax.broadcasted_iota(jnp.int32, (ln, ln), 0))
    heads = range(RET_HEADS)
    q = [q_ref[:, hd * RET_DK:(hd + 1) * RET_DK] for hd in heads]
    k = [k_ref[:, hd * RET_DK:(hd + 1) * RET_DK] for hd in heads]
    v = [v_ref[:, hd * RET_DV:(hd + 1) * RET_DV] for hd in heads]
    st = [st_ref[hd] for hd in heads]
    qk = [_dot_nt(q[hd], k[hd]) for hd in heads]
    inter = [_dot(q[hd], st[hd].astype(BF16)) for hd in heads]
    scores = [jnp.where(causal, qk[hd], 0.0).astype(BF16) for hd in heads]
    out = [_dot(scores[hd], v[hd]) + inter[hd] for hd in heads]
    for hd in heads:
        st_ref[hd] = (st[hd] + _dot_tn(k[hd], v[hd])) * math.exp(_ret_log_gamma(hd) * ln)
    for hd in heads:
        sl = slice(hd * RET_DV, (hd + 1) * RET_DV)
        ms = jnp.mean(out[hd] * out[hd], axis=-1, keepdims=True)
        y = out[hd] * lax.rsqrt(ms + RET_EPS) * gn_ref[:, sl]
        o_ref[:, sl] = (y * g_ref[:, sl].astype(F32)).astype(BF16)


def _ret_chunk(proj, gn, batch, seq):
    t = proj.shape[0]
    ln = min(RET_CHUNK, seq)
    nc = seq // ln
    hk = RET_HEADS * RET_DK
    hv = RET_HEADS * RET_DV
    return pl.pallas_call(
        _ret_chunk_kernel,
        out_shape=jax.ShapeDtypeStruct((t, hv), BF16),
        grid=(batch, nc),
        in_specs=[
            pl.BlockSpec((ln, hk), lambda b, c: (b * nc + c, 0)),
            pl.BlockSpec((ln, hk), lambda b, c: (b * nc + c, 1)),
            pl.BlockSpec((ln, hv), lambda b, c: (b * nc + c, 1)),
            pl.BlockSpec((ln, hv), lambda b, c: (b * nc + c, 2)),
            _const_spec((1, hv)),
        ],
        out_specs=pl.BlockSpec((ln, hv), lambda b, c: (b * nc + c, 0)),
        scratch_shapes=[pltpu.VMEM((RET_HEADS, RET_DK, RET_DV), F32)],
        compiler_params=_params(("parallel", "arbitrary")),
        name="ret_chunk",
    )(proj, proj, proj, proj, gn)


def _prev_rows(cur, prev8, shift):
    ext = jnp.concatenate([prev8, cur], axis=0)
    return pltpu.roll(ext, shift, axis=0)[V7X_SUBLANES:, :]


def _log_sigmoid(x):
    return jnp.minimum(x, 0.0) - jnp.log1p(jnp.exp(-jnp.abs(x)))


def _ml_front_kernel(h_ref, g_ref, w_ref, cw_ref, cb_ref, mq_ref, mk_ref, mv_ref, wgt_ref, bg_ref,
                     c_ref, q_ref, k_ref, v_ref, z_ref, gt_ref, tail_ref, *, tiles_per_seq):
    i = pl.program_id(0)
    tm = h_ref.shape[0]

    @pl.when((i % tiles_per_seq) == 0)
    def _():
        tail_ref[...] = jnp.zeros_like(tail_ref)

    xn = _rms(h_ref[...], g_ref[...]).astype(BF16)
    blk = mq_ref.shape[1]
    groups = list(range(0, ML_INNER, PROJ_CHUNK))
    st = {"gates": jnp.zeros((tm, V7X_LANES), F32) + bg_ref[...]}

    def project(c0):
        cs = slice(c0, c0 + PROJ_CHUNK)
        st["u", c0] = _dot(xn, w_ref[:, cs])
        z_ref[:, cs] = _silu(_dot(xn, w_ref[:, ML_INNER + c0:ML_INNER + c0 + PROJ_CHUNK])).astype(BF16)

    def conv(c0):
        cs = slice(c0, c0 + PROJ_CHUNK)
        u = st.pop(("u", c0))
        prev = tail_ref[:, cs]
        tail_ref[:, cs] = u[tm - V7X_SUBLANES:, :]
        acc = u * cw_ref[ML_CONV - 1:ML_CONV, cs] + cb_ref[:, cs]
        for s in range(1, ML_CONV):
            acc = acc + _prev_rows(u, prev, s) * cw_ref[ML_CONV - 1 - s:ML_CONV - s, cs]
        c16 = _silu(acc).astype(BF16)
        c_ref[:, cs] = c16
        st["c16", c0] = c16
        st["u16", c0] = u.astype(BF16)

    def maps(c0):
        c16 = st.pop(("c16", c0))
        u16 = st.pop(("u16", c0))
        gates = st["gates"]
        for n in range(PROJ_CHUNK // blk):
            ls = slice(n * blk, (n + 1) * blk)
            sl = slice(c0 + n * blk, c0 + (n + 1) * blk)
            tile = sl.start // blk
            qn = _dot(c16[:, ls], mq_ref[tile]).astype(BF16)
            kn = _dot(c16[:, ls], mk_ref[tile]).astype(BF16)
            vn = _dot(u16[:, ls], mv_ref[tile]).astype(BF16)
            q_ref[:, sl] = qn
            k_ref[:, sl] = (kn.astype(F32) * (ML_DH ** -0.5)).astype(BF16)
            v_ref[:, sl] = vn
            gates = (gates + _dot(qn, wgt_ref[0, sl, :]) + _dot(kn, wgt_ref[1, sl, :])
                     + _dot(vn, wgt_ref[2, sl, :]))
        st["gates"] = gates

    project(groups[0])
    for gi, c0 in enumerate(groups):
        if gi + 1 < len(groups):
            project(groups[gi + 1])
        conv(c0)
        maps(c0)
    gates = st["gates"]
    lane = lax.broadcasted_iota(jnp.int32, gates.shape, 1)
    gt_ref[...] = jnp.where(jnp.logical_and(lane >= ML_HEADS, lane < 2 * ML_HEADS),
                            _log_sigmoid(gates), gates)


def _ml_front(h, g, w, conv_w, conv_b, mq, mk, mv, wgt, bg, seq):
    t, d = h.shape
    tm = min(TM_PROJ, seq)
    tps = seq // tm
    act = jax.ShapeDtypeStruct((t, ML_INNER), BF16)
    row_spec = pl.BlockSpec((tm, ML_INNER), lambda i: (i, 0))
    return pl.pallas_call(
        functools.partial(_ml_front_kernel, tiles_per_seq=tps),
        out_shape=(act, act, act, act, act, jax.ShapeDtypeStruct((t, V7X_LANES), F32)),
        grid=(t // tm,),
        in_specs=[
            pl.BlockSpec((tm, d), lambda i: (i, 0)),
            _const_spec((1, d)),
            _const_spec(w.shape),
            _const_spec((ML_CONV, ML_INNER)),
            _const_spec((1, ML_INNER)),
            _const_spec(mq.shape), _const_spec(mk.shape), _const_spec(mv.shape),
            _const_spec(wgt.shape),
            _const_spec((1, V7X_LANES)),
        ],
        out_specs=(row_spec, row_spec, row_spec, row_spec, row_spec,
                   pl.BlockSpec((tm, V7X_LANES), lambda i: (i, 0))),
        scratch_shapes=[pltpu.VMEM((V7X_SUBLANES, ML_INNER), F32)],
        compiler_params=_params(("arbitrary",)),
        name="ml_front",
    )(h, g, w, conv_w, conv_b, mq, mk, mv, wgt, bg)


def _ml_chunk_kernel(q_ref, k_ref, v_ref, gt_ref, c_ref, z_ref, gn_ref, sk_ref, o_ref,
                     cst_ref, nst_ref, mst_ref):
    ci = pl.program_id(1)
    nb, ln, _ = q_ref.shape

    @pl.when(ci == 0)
    def _():
        cst_ref[...] = jnp.zeros_like(cst_ref)
        nst_ref[...] = jnp.zeros_like(nst_ref)
        mst_ref[...] = jnp.zeros_like(mst_ref)

    row = lax.broadcasted_iota(jnp.int32, (ln, ln), 0)
    col = lax.broadcasted_iota(jnp.int32, (ln, ln), 1)
    causal = col <= row
    tril = causal.astype(F32)
    heads = range(ML_HEADS)
    sls = [slice(hd * ML_DH, (hd + 1) * ML_DH) for hd in heads]

    def row_stages(bi):
        st = {}

        def load():
            gates = gt_ref[bi]
            cum = _dot_exact(tril, gates)
            st["gates"], st["cum"], st["gates_t"], st["cum_t"] = gates, cum, gates.T, cum.T
            st["q"] = [q_ref[bi, :, sl] for sl in sls]
            st["k"] = [k_ref[bi, :, sl] for sl in sls]
            st["v"] = [v_ref[bi, :, sl] for sl in sls]
            st["cst"] = [cst_ref[bi, hd] for hd in heads]
            st["n_row"] = [nst_ref[bi, hd:hd + 1, :] for hd in heads]
            st["m_prev"] = [mst_ref[bi, hd:hd + 1, 0:1] for hd in heads]
            st["b_col"] = [cum[:, ML_HEADS + hd:ML_HEADS + hd + 1] for hd in heads]

        def scores():
            st["qk"] = [_dot_nt(st["q"][hd], st["k"][hd]) for hd in heads]
            st["inter"] = [_dot(st["q"][hd], st["cst"][hd].astype(BF16)) for hd in heads]

        def decay():
            b_col, m_prev = st["b_col"], st["m_prev"]
            log_d = [jnp.where(causal, b_col[hd] - st["cum_t"][ML_HEADS + hd:ML_HEADS + hd + 1, :]
                               + st["gates_t"][hd:hd + 1, :], -jnp.inf) for hd in heads]
            log_inter = [b_col[hd] + m_prev[hd] for hd in heads]
            st["m_row"] = [jnp.maximum(log_inter[hd], jnp.max(log_d[hd], axis=-1, keepdims=True)) for hd in heads]
            st["w_inter"] = [jnp.exp(log_inter[hd] - st["m_row"][hd]) for hd in heads]
            st["s"] = [st["qk"][hd] * jnp.exp(log_d[hd] - st["m_row"][hd]) for hd in heads]

        def readout():
            s, q = st["s"], st["q"]
            num = [_dot(s[hd].astype(BF16), st["v"][hd]) + st["w_inter"][hd] * st["inter"][hd] for hd in heads]
            den = [jnp.sum(s[hd], axis=-1, keepdims=True) + st["w_inter"][hd] * jnp.sum(
                q[hd].astype(F32) * st["n_row"][hd], axis=-1, keepdims=True) for hd in heads]
            st["hh"] = [num[hd] / jnp.maximum(jnp.abs(den[hd]), jnp.exp(-st["m_row"][hd])) for hd in heads]

        def state():
            b_col, m_prev = st["b_col"], st["m_prev"]
            for hd in heads:
                b_last = b_col[hd][ln - 1:ln, :]
                log_w = b_last - b_col[hd] + st["gates"][:, hd:hd + 1]
                m_new = jnp.maximum(b_last + m_prev[hd], jnp.max(log_w, axis=0, keepdims=True))
                kw = st["k"][hd].astype(F32) * jnp.exp(log_w - m_new)
                carry = jnp.exp(b_last + m_prev[hd] - m_new)
                cst_ref[bi, hd] = carry * st["cst"][hd] + _dot_tn(kw.astype(BF16), st["v"][hd])
                nst_ref[bi, hd:hd + 1, :] = carry * st["n_row"][hd] + jnp.sum(kw, axis=0, keepdims=True)
                mst_ref[bi, hd:hd + 1, :] = jnp.broadcast_to(m_new, (1, V7X_LANES))

        def epilogue(hd):
            sl = sls[hd]
            hh = st["hh"][hd]
            hc = hh - jnp.mean(hh, axis=-1, keepdims=True)
            y = hc * lax.rsqrt(jnp.mean(hc * hc, axis=-1, keepdims=True) + ML_EPS) * gn_ref[:, sl]
            y = (y + sk_ref[:, sl] * c_ref[bi, :, sl].astype(F32)) * z_ref[bi, :, sl].astype(F32)
            o_ref[bi, :, sl] = y.astype(BF16)

        return [load, scores, decay, readout, state], [functools.partial(epilogue, hd) for hd in heads]

    pending = []
    for bi in range(nb):
        stages, epilogues = row_stages(bi)
        for i, stage in enumerate(stages):
            stage()
            if i < len(pending):
                pending[i]()
        for piece in pending[len(stages):]:
            piece()
        pending = epilogues
    for piece in pending:
        piece()


def _ml_chunk(q, k, v, gates, c, z, gn, skip, batch, seq):
    t = q.shape[0]
    ln = min(ML_CHUNK, seq)
    nc = seq // ln
    nb = math.gcd(batch, ML_BATCH_PER_STEP)
    row_spec = pl.BlockSpec((nb, ln, ML_INNER), lambda b, ci: (b, ci, 0))
    vec = pl.BlockSpec((1, ML_INNER), lambda b, ci: (0, 0))
    acts = [x.reshape(batch, seq, ML_INNER) for x in (q, k, v)]
    out = pl.pallas_call(
        _ml_chunk_kernel,
        out_shape=jax.ShapeDtypeStruct((batch, seq, ML_INNER), BF16),
        grid=(batch // nb, nc),
        in_specs=[
            row_spec, row_spec, row_spec,
            pl.BlockSpec((nb, ln, V7X_LANES), lambda b, ci: (b, ci, 0)),
            row_spec,
            row_spec,
            vec, vec,
        ],
        out_specs=row_spec,
        scratch_shapes=[
            pltpu.VMEM((nb, ML_HEADS, ML_DH, ML_DH), F32),
            pltpu.VMEM((nb, V7X_SUBLANES, ML_DH), F32),
            pltpu.VMEM((nb, V7X_SUBLANES, V7X_LANES), F32),
        ],
        compiler_params=_params(("parallel", "arbitrary")),
        name="ml_chunk",
    )(*acts, gates.reshape(batch, seq, V7X_LANES), c.reshape(batch, seq, ML_INNER),
      z.reshape(batch, seq, ML_INNER), gn, skip)
    return out.reshape(t, ML_INNER)


def _rw_pre_kernel(h_ref, hp_ref, g_ref, mu_ref, wrkv_ref, la_ref, lbw_ref, lba_ref, lbg_ref,
                   w0_ref, a0_ref, kk_ref, ka_ref,
                   r_out, lw_out, k_out, v_out, kk_out, kb_out, g_out, *, tiles_per_seq,
                   lora_w, lora_a):
    i = pl.program_id(0)
    tm = h_ref.shape[0]
    g = g_ref[...]
    xn = _rms(h_ref[...], g)
    first = (i % tiles_per_seq) == 0
    xp8 = jnp.where(first, 0.0, _rms(hp_ref[...], g))
    dx = _prev_rows(xn, xp8, 1) - xn

    def matmuls(rows):
        def mix(n):
            return (xn[rows] + dx[rows] * mu_ref[n:n + 1, :]).astype(BF16)

        out = {"r": _dot(mix(0), wrkv_ref[0]), "k": _dot(mix(1), wrkv_ref[1]), "v": _dot(mix(2), wrkv_ref[2])}
        hw = jnp.tanh(_dot(mix(3), la_ref[:, 0:lora_w])).astype(BF16)
        ha = _dot(mix(4), la_ref[:, lora_w:lora_w + lora_a]).astype(BF16)
        hg = jax.nn.sigmoid(_dot(mix(5), la_ref[:, lora_w + lora_a:])).astype(BF16)
        out["w_pre"] = _dot(hw, lbw_ref[...])
        out["a_pre"] = _dot(ha, lba_ref[...])
        g_out[rows, :] = _dot(hg, lbg_ref[...]).astype(BF16)
        return out

    def tail(rows, m):
        lw_out[rows, :] = -math.exp(-0.5) * jax.nn.sigmoid(w0_ref[...] + m["w_pre"])
        alpha = jax.nn.sigmoid(a0_ref[...] + m["a_pre"])
        kk = m["k"] * kk_ref[...]
        r_out[rows, :] = m["r"].astype(BF16)
        k_out[rows, :] = (m["k"] * (1.0 + (alpha - 1.0) * ka_ref[...])).astype(BF16)
        v_out[rows, :] = m["v"].astype(BF16)
        kk_out[rows, :] = kk.astype(BF16)
        kb_out[rows, :] = (kk * alpha).astype(BF16)

    halves = [slice(0, tm // 2), slice(tm // 2, tm)]
    first_half = matmuls(halves[0])
    second_half = matmuls(halves[1])
    tail(halves[0], first_half)
    tail(halves[1], second_half)


def _rw_pre(h, g, mu, wrkv, la, lbw, lba, lbg, w0, a0, k_k, k_a, seq, lora_w, lora_a):
    t, d = h.shape
    tm = min(TM_PRE, seq)
    tps = seq // tm
    rb = tm // V7X_SUBLANES
    row_spec = pl.BlockSpec((tm, d), lambda i: (i, 0))
    act = jax.ShapeDtypeStruct((t, d), BF16)
    vec = _const_spec((1, d))
    return pl.pallas_call(
        functools.partial(_rw_pre_kernel, tiles_per_seq=tps, lora_w=lora_w, lora_a=lora_a),
        out_shape=(act, jax.ShapeDtypeStruct((t, d), F32), act, act, act, act, act),
        grid=(t // tm,),
        in_specs=[
            row_spec,
            pl.BlockSpec((V7X_SUBLANES, d), lambda i: (jnp.maximum(i * rb - 1, 0), 0)),
            vec,
            _const_spec(mu.shape),
            _const_spec(wrkv.shape),
            _const_spec(la.shape),
            _const_spec(lbw.shape), _const_spec(lba.shape), _const_spec(lbg.shape),
            vec, vec, vec, vec,
        ],
        out_specs=(row_spec,) * 7,
        compiler_params=_params(("parallel",)),
        name="rw_pre",
    )(h, h, g, mu, wrkv, la, lbw, lba, lbg, w0, a0, k_k, k_a)


def _rw_chunk_kernel(r_ref, lw_ref, k_ref, v_ref, kk_ref, kb_ref, g_ref, rk_ref, gng_ref, gnb_ref,
                     o_ref, st_ref):
    ci = pl.program_id(1)
    nb, ln, _ = r_ref.shape
    pair = 2 * RW_DH

    @pl.when(ci == 0)
    def _():
        st_ref[...] = jnp.zeros_like(st_ref)

    trow = lax.broadcasted_iota(jnp.int32, (ln, ln), 0)
    tcol = lax.broadcasted_iota(jnp.int32, (ln, ln), 1)
    tril = (tcol <= trow).astype(F32)
    lane = lax.broadcasted_iota(jnp.int32, (ln, pair), 1)
    head0 = lane < RW_DH
    srow = lax.broadcasted_iota(jnp.int32, (2 * ln, 2 * ln), 0)
    scol = lax.broadcasted_iota(jnp.int32, (2 * ln, 2 * ln), 1)
    strict_t = srow < scol
    incl_t = srow <= scol

    def stack(x):
        return jnp.concatenate([jnp.where(head0, x, 0.0), jnp.where(head0, 0.0, x)], axis=0).astype(BF16)

    pair_slices = [slice(p * pair, (p + 1) * pair) for p in range(RW_HEADS // 2)]
    n_steps = int(math.log2(ln))

    def head_sums(x):
        s0 = jnp.sum(jnp.where(head0, x, 0.0), axis=-1, keepdims=True)
        s1 = jnp.sum(jnp.where(head0, 0.0, x), axis=-1, keepdims=True)
        return s0, s1

    def make_group(rows):
        chains = [(bi, p) for bi in rows for p in range(RW_HEADS // 2)]
        ids = range(len(chains))
        gs = {k_: [] for k_ in ("ar_s", "b_s", "k_s", "bk", "v_t", "g_last", "bonus")}

        def prologue(bi):
            lw = lw_ref[bi]
            cum = _dot_exact(tril, lw)
            cum_last = cum[ln - 1:ln, :]
            g_inv = jnp.exp(-cum)
            g_rem = jnp.exp(cum_last - cum)
            kk_raw = kk_ref[bi].astype(F32)
            r_all = r_ref[bi].astype(F32)
            kk = k_ref[bi].astype(F32)
            v_all = v_ref[bi].astype(F32)
            rkk = r_all * kk * rk_ref[...]
            inv_parts = []
            for sl in pair_slices:
                s0, s1 = head_sums(kk_raw[:, sl] * kk_raw[:, sl])
                inv_parts.append(jnp.where(head0, lax.rsqrt(jnp.maximum(s0, 1e-24)),
                                           lax.rsqrt(jnp.maximum(s1, 1e-24))))
                t0, t1 = head_sums(rkk[:, sl])
                gs["bonus"].append(jnp.where(head0, t0, t1) * v_all[:, sl])
            inv_all = jnp.concatenate(inv_parts, axis=1)
            kn = kk_raw * inv_all
            bb = kb_ref[bi].astype(F32) * inv_all
            a_all = -kn * jnp.exp(cum - lw)
            rt_all = r_all * jnp.exp(cum)
            bt_all = bb * g_inv
            kt_all = kk * g_inv
            bh_all = bb * g_rem
            kh_all = kk * g_rem
            gl = jnp.exp(cum_last)
            for sl in pair_slices:
                gs["ar_s"].append(jnp.concatenate([stack(a_all[:, sl]), stack(rt_all[:, sl])], axis=0))
                gs["b_s"].append(stack(bt_all[:, sl]))
                gs["k_s"].append(stack(kt_all[:, sl]))
                gs["bk"].append(jnp.concatenate([stack(bh_all[:, sl]), stack(kh_all[:, sl])], axis=0))
                vt = v_all[:, sl].T
                gs["v_t"].append(jnp.concatenate([vt[:RW_DH], vt[RW_DH:]], axis=1).astype(BF16))
                gs["g_last"].append(gl[:, sl])

        def products():
            gs["st"] = [st_ref[bi, p] for bi, p in chains]
            gs["m"] = [_dot_nt(jnp.concatenate([gs["b_s"][c], gs["k_s"][c], gs["st"][c].astype(BF16)], axis=0),
                               gs["ar_s"][c]) for c in ids]

        def masks():
            m = gs["m"]
            gs["pw"] = [jnp.where(strict_t, m[c][:2 * ln, :2 * ln], 0.0) for c in ids]
            ak = [jnp.where(strict_t, m[c][2 * ln:4 * ln, :2 * ln], 0.0).astype(BF16) for c in ids]
            gs["rbk"] = [jnp.concatenate([jnp.where(incl_t, m[c][:2 * ln, 2 * ln:], 0.0).astype(BF16),
                                          jnp.where(incl_t, m[c][2 * ln:4 * ln, 2 * ln:], 0.0).astype(BF16)],
                                         axis=0) for c in ids]
            gs["inter_r"] = [m[c][4 * ln:, 2 * ln:] for c in ids]
            gs["u"] = [m[c][4 * ln:, :2 * ln] + _dot(gs["v_t"][c], ak[c]) for c in ids]

        def neumann(step):
            u, pw = gs["u"], gs["pw"]
            p16 = [pw[c].astype(BF16) for c in ids]
            if step + 1 < n_steps:
                both = [_dot(jnp.concatenate([u[c].astype(BF16), p16[c]], axis=0), p16[c]) for c in ids]
                gs["u"] = [u[c] + both[c][:RW_DH] for c in ids]
                gs["pw"] = [both[c][RW_DH:] for c in ids]
            else:
                gs["u"] = [u[c] + _dot(u[c].astype(BF16), p16[c]) for c in ids]

        def readout():
            uv = [jnp.concatenate([gs["u"][c].astype(BF16), gs["v_t"][c]], axis=1) for c in ids]
            gs["y_t"] = [gs["inter_r"][c] + _dot(uv[c], gs["rbk"][c]) for c in ids]
            for c, (bi, p) in enumerate(chains):
                st_ref[bi, p] = gs["st"][c] * gs["g_last"][c] + _dot(uv[c], gs["bk"][c])

        def epilogue(c):
            bi, p = chains[c]
            sl = pair_slices[p]
            y_t = gs["y_t"][c]
            yc = y_t - jnp.mean(y_t, axis=0, keepdims=True)
            yn = (yc * lax.rsqrt(jnp.mean(yc * yc, axis=0, keepdims=True) + RW_GN_EPS)).T
            y_p = jnp.concatenate([yn[:ln], yn[ln:]], axis=1)
            out = y_p * gng_ref[:, sl] + gnb_ref[:, sl] + gs["bonus"][c]
            o_ref[bi, :, sl] = (out * g_ref[bi, :, sl].astype(F32)).astype(BF16)

        stages = [products, masks] + [functools.partial(neumann, s_) for s_ in range(n_steps)] + [readout]
        return ([functools.partial(prologue, bi) for bi in rows], stages,
                [functools.partial(epilogue, c) for c in ids])

    def interleave(stages, fillers):
        per = -(-len(fillers) // max(len(stages), 1)) if fillers else 0
        fillers = list(fillers)
        for stage in stages:
            stage()
            for piece in fillers[:per]:
                piece()
            fillers = fillers[per:]
        for piece in fillers:
            piece()

    half = max(nb // 2, 1)
    groups = [make_group(range(g0, min(g0 + half, nb))) for g0 in range(0, nb, half)]
    for piece in groups[0][0]:
        piece()
    pending = []
    for gi, (_, stages, epilogues) in enumerate(groups):
        nxt = groups[gi + 1][0] if gi + 1 < len(groups) else []
        interleave(stages, list(pending) + list(nxt))
        pending = epilogues
    for piece in pending:
        piece()


def _rw_chunk(r, lw, k, v, kk, kb, g, r_k, gn_g, gn_b, batch, seq):
    t, d = r.shape
    ln = min(RW_CHUNK, seq)
    nc = seq // ln
    nb = math.gcd(batch, RW_BATCH_PER_STEP)
    row_spec = pl.BlockSpec((nb, ln, d), lambda bi, ci: (bi, ci, 0))
    vec = pl.BlockSpec((1, d), lambda bi, ci: (0, 0))
    acts = [x.reshape(batch, seq, d) for x in (r, lw, k, v, kk, kb, g)]
    out = pl.pallas_call(
        _rw_chunk_kernel,
        out_shape=jax.ShapeDtypeStruct((batch, seq, d), BF16),
        grid=(batch // nb, nc),
        in_specs=[row_spec] * 7 + [vec, vec, vec],
        out_specs=row_spec,
        scratch_shapes=[pltpu.VMEM((nb, RW_HEADS // 2, RW_DH, 2 * RW_DH), F32)],
        compiler_params=_params(("parallel", "arbitrary")),
        name="rw_chunk",
    )(*acts, r_k, gn_g, gn_b)
    return out.reshape(t, d)


def _rope_tables(seq):
    half = RET_DK // 2
    pos = jnp.arange(seq, dtype=F32)
    inv_freq = 1.0 / (ROPE_BASE ** jnp.linspace(0.0, 1.0, half, dtype=F32))
    ang = pos[:, None] * inv_freq[None, :]
    local = (jnp.arange(seq) % min(RET_CHUNK, seq)).astype(F32) + 1.0
    log_gamma = jnp.asarray([_ret_log_gamma(hd) for hd in range(RET_HEADS)], F32)
    q_scale = jnp.exp(log_gamma[:, None] * local[None, :])
    k_scale = jnp.exp(-log_gamma[:, None] * local[None, :]) * RET_DK ** -0.5
    scale = jnp.concatenate([q_scale, k_scale], axis=0)[:, :, None]
    return jnp.cos(ang)[None] * scale, jnp.sin(ang)[None] * scale


def _retention_mixer(h, norm_g, w_in16, layer, gn, batch, seq):
    cos, sin = _rope_tables(seq)
    proj = _ret_proj(h, norm_g[None, :], w_in16, layer, cos, sin, seq)
    return _ret_chunk(proj, gn[None, :].astype(F32), batch, seq)


def _block_diag_tiles(w, tile):
    nb, bs, _ = w.shape
    rows = w.reshape(nb * bs // tile, tile, bs)
    dense = jnp.tile(rows, (1, 1, tile // bs))
    idx = jnp.arange(tile) // bs
    return jnp.where(idx[:, None] == idx[None, :], dense, 0.0)


def _mlstm_mixer(h, norm_g, w_in, conv_w, conv_b, wq, wk, wv, w_gate, b_gate, gn, skip, batch, seq):
    tile = 2 * V7X_LANES
    mq = _block_diag_tiles(wq, tile).astype(BF16)
    mk = _block_diag_tiles(wk, tile).astype(BF16)
    mv = _block_diag_tiles(wv, tile).astype(BF16)
    n_gate = w_gate.shape[1]
    wgt = jnp.pad(w_gate.reshape(3, ML_INNER, n_gate), ((0, 0), (0, 0), (0, V7X_LANES - n_gate))).astype(BF16)
    bg = jnp.pad(b_gate, (0, V7X_LANES - n_gate))[None, :].astype(F32)
    c, q, k, v, z, gates = _ml_front(h, norm_g[None, :], w_in.astype(BF16), conv_w, conv_b[None, :],
                                     mq, mk, mv, wgt, bg, seq)
    return _ml_chunk(q, k, v, gates, c, z, gn[None, :], skip[None, :], batch, seq)


def _pad_to(x, axis, size):
    pad = [(0, 0)] * x.ndim
    pad[axis] = (0, size - x.shape[axis])
    return jnp.pad(x, pad)


def _rwkv_mixer(h, norm_g, mu, w_rkv, w0, w_la, w_lb, a0, a_la, a_lb, g_la, g_lb, k_k, k_a, r_k,
                gn_g, gn_b, batch, seq):
    d = h.shape[1]
    lw_ = -(-w_la.shape[1] // V7X_LANES) * V7X_LANES
    la_ = -(-a_la.shape[1] // V7X_LANES) * V7X_LANES
    lg_ = -(-g_la.shape[1] // V7X_LANES) * V7X_LANES
    la = jnp.concatenate([_pad_to(w_la, 1, lw_), _pad_to(a_la, 1, la_), _pad_to(g_la, 1, lg_)],
                         axis=1).astype(BF16)
    lbw = _pad_to(w_lb, 0, lw_).astype(BF16)
    lba = _pad_to(a_lb, 0, la_).astype(BF16)
    lbg = _pad_to(g_lb, 0, lg_).astype(BF16)
    r, lw, k, v, kk, kb, g = _rw_pre(h, norm_g[None, :], mu, w_rkv.astype(BF16), la, lbw, lba, lbg,
                                     w0[None, :], a0[None, :], k_k[None, :], k_a[None, :], seq, lw_, la_)
    return _rw_chunk(r, lw, k, v, kk, kb, g, r_k.reshape(1, d), gn_g[None, :], gn_b[None, :], batch, seq)


def kernel(x, norm_mix, norm_ffn, norm_final, ret_w_in, ret_gn, ret_w_out, ml_w_in, ml_conv_w, ml_conv_b, ml_wq, ml_wk, ml_wv, ml_w_gate, ml_b_gate, ml_gn, ml_skip, ml_w_out, rw_mu, rw_w_rkv, rw_w0, rw_w_lora_a, rw_w_lora_b, rw_a0, rw_a_lora_a, rw_a_lora_b, rw_g_lora_a, rw_g_lora_b, rw_k_k, rw_k_a, rw_r_k, rw_gn_g, rw_gn_b, rw_w_out, ffn_w_gu, ffn_w_down):
    batch, seq, d = x.shape
    depth = norm_mix.shape[0]
    h = x.reshape(batch * seq, d)
    ret_w_in16 = ret_w_in.astype(BF16)
    w_out16 = (ret_w_out.astype(BF16), ml_w_out.astype(BF16), rw_w_out.astype(BF16))
    w_gu16 = ffn_w_gu.astype(BF16)
    w_down16 = ffn_w_down.astype(BF16)
    for i in range(depth):
        kind, j = i % 3, i // 3
        if kind == 0:
            y = _retention_mixer(h, norm_mix[i], ret_w_in16, j, ret_gn[j], batch, seq)
        elif kind == 1:
            y = _mlstm_mixer(h, norm_mix[i], ml_w_in[j], ml_conv_w[j], ml_conv_b[j], ml_wq[j], ml_wk[j],
                             ml_wv[j], ml_w_gate[j], ml_b_gate[j], ml_gn[j], ml_skip[j], batch, seq)
        else:
            y = _rwkv_mixer(h, norm_mix[i], rw_mu[j], rw_w_rkv[j], rw_w0[j], rw_w_lora_a[j],
                            rw_w_lora_b[j], rw_a0[j], rw_a_lora_a[j], rw_a_lora_b[j], rw_g_lora_a[j],
                            rw_g_lora_b[j], rw_k_k[j], rw_k_a[j], rw_r_k[j], rw_gn_g[j], rw_gn_b[j],
                            batch, seq)
        h = _mix_ffn(y, w_out16[kind], j, h, norm_ffn[i][None, :], w_gu16, w_down16, i,
                     norm_final[None, :], i == depth - 1)
    return h.reshape(batch, seq, d)
```

```python
import functools
import math

import jax
import jax.numpy as jnp
from jax import lax
from jax.experimental import pallas as pl
from jax.experimental.pallas import tpu as pltpu

F32 = jnp.float32
BF16 = jnp.bfloat16

D_MODEL = 1024
D_FF = 2816
RMS_EPS = 1e-6

RET_HEADS = 4
RET_DK = 256
RET_DV = 512
RET_EPS = 1e-6
ROPE_BASE = 10000.0

ML_INNER = 2048
ML_HEADS = 4
ML_DH = 512
ML_CONV = 4
ML_EPS = 1e-6

RW_HEADS = 16
RW_DH = 64
RW_GN_EPS = 64e-5

V7X_LANES = 128
V7X_SUBLANES = 8
V7X_VMEM_LIMIT = 56 * 1024 * 1024

TM_PROJ = 512
TM_RET_PROJ = 512
PROJ_CHUNK = 512
TM_FFN = 1024
FFN_CHUNK = 256
TM_PRE = 512
RET_CHUNK = 256
ML_CHUNK = 256
ML_BATCH_PER_STEP = 2
RW_CHUNK = 64
RW_BATCH_PER_STEP = 4


def _params(sem, vmem=V7X_VMEM_LIMIT):
    return pltpu.CompilerParams(dimension_semantics=sem, vmem_limit_bytes=vmem)


def _dot(a, b):
    return jnp.dot(a, b, preferred_element_type=F32)


def _dot_nt(a, b):
    return lax.dot_general(a, b, (((1,), (1,)), ((), ())), preferred_element_type=F32)


def _dot_tn(a, b):
    return lax.dot_general(a, b, (((0,), (0,)), ((), ())), preferred_element_type=F32)


def _dot_exact(a, b):
    hi = b.astype(BF16)
    lo = (b - hi.astype(F32)).astype(BF16)
    a16 = a.astype(BF16)
    return _dot(a16, hi) + _dot(a16, lo)


def _rms(x, g):
    ms = jnp.mean(x * x, axis=-1, keepdims=True)
    return x * lax.rsqrt(ms + RMS_EPS) * g


def _silu(x):
    return x * jax.nn.sigmoid(x)


def _const_spec(shape):
    nd = len(shape)
    return pl.BlockSpec(shape, lambda *_: (0,) * nd, pipeline_mode=pl.Buffered(1))


def _layer_spec(shape, layer):
    nd = len(shape)
    return pl.BlockSpec((None,) + tuple(shape), lambda *_: (layer,) + (0,) * nd,
                        pipeline_mode=pl.Buffered(1))


def _ret_proj_kernel(h_ref, g_ref, w_ref, cos_ref, sin_ref, o_ref):
    xn = _rms(h_ref[...], g_ref[...]).astype(BF16)
    half = RET_DK // 2
    hk = RET_HEADS * RET_DK
    hv = RET_HEADS * RET_DV
    for hd in range(2 * RET_HEADS):
        c0 = hd * RET_DK
        acc = _dot(xn, w_ref[:, c0:c0 + RET_DK])
        cos = cos_ref[hd]
        sin = sin_ref[hd]
        t1 = acc[:, :half]
        t2 = acc[:, half:]
        o_ref[:, c0:c0 + half] = (t1 * cos - t2 * sin).astype(BF16)
        o_ref[:, c0 + half:c0 + RET_DK] = (t2 * cos + t1 * sin).astype(BF16)
    for c0 in range(2 * hk, 2 * hk + hv, PROJ_CHUNK):
        o_ref[:, c0:c0 + PROJ_CHUNK] = _dot(xn, w_ref[:, c0:c0 + PROJ_CHUNK]).astype(BF16)
    for c0 in range(2 * hk + hv, 2 * hk + 2 * hv, PROJ_CHUNK):
        o_ref[:, c0:c0 + PROJ_CHUNK] = _silu(_dot(xn, w_ref[:, c0:c0 + PROJ_CHUNK])).astype(BF16)


def _ret_proj(h, g, w, layer, cos, sin, seq):
    t, d = h.shape
    n = w.shape[2]
    tm = min(TM_RET_PROJ, seq)
    spt = seq // tm
    return pl.pallas_call(
        _ret_proj_kernel,
        out_shape=jax.ShapeDtypeStruct((t, n), BF16),
        grid=(t // tm,),
        in_specs=[
            pl.BlockSpec((tm, d), lambda i: (i, 0)),
            _const_spec((1, d)),
            _layer_spec((d, n), layer),
            pl.BlockSpec((2 * RET_HEADS, tm, RET_DK // 2), lambda i: (0, i % spt, 0)),
            pl.BlockSpec((2 * RET_HEADS, tm, RET_DK // 2), lambda i: (0, i % spt, 0)),
        ],
        out_specs=pl.BlockSpec((tm, n), lambda i: (i, 0)),
        compiler_params=_params(("parallel",)),
        name="ret_proj",
    )(h, g, w, cos, sin)


def _mix_ffn_kernel(y_ref, wo_ref, h_ref, g_ref, wgu_ref, wd_ref, gf_ref, o_ref, *, final_norm):
    h = h_ref[...] + _dot(y_ref[...], wo_ref[...])
    xn = _rms(h, g_ref[...]).astype(BF16)
    acc = h
    for f in range(0, D_FF, FFN_CHUNK):
        gate = _dot(xn, wgu_ref[:, f:f + FFN_CHUNK])
        up = _dot(xn, wgu_ref[:, D_FF + f:D_FF + f + FFN_CHUNK])
        act = (_silu(gate) * up).astype(BF16)
        acc = acc + _dot(act, wd_ref[f:f + FFN_CHUNK, :])
    if final_norm:
        acc = _rms(acc, gf_ref[...])
    o_ref[...] = acc


def _mix_ffn(y, wo, wo_layer, h, g, wgu, wd, ffn_layer, gf, final_norm):
    t, d = h.shape
    k = y.shape[1]
    tm = min(TM_FFN, t)
    return pl.pallas_call(
        functools.partial(_mix_ffn_kernel, final_norm=final_norm),
        out_shape=jax.ShapeDtypeStruct((t, d), F32),
        grid=(t // tm,),
        in_specs=[
            pl.BlockSpec((tm, k), lambda i: (i, 0)),
            _layer_spec((k, d), wo_layer),
            pl.BlockSpec((tm, d), lambda i: (i, 0)),
            _const_spec((1, d)),
            _layer_spec((d, 2 * D_FF), ffn_layer),
            _layer_spec((D_FF, d), ffn_layer),
            _const_spec((1, d)),
        ],
        out_specs=pl.BlockSpec((tm, d), lambda i: (i, 0)),
        compiler_params=_params(("parallel",)),
        name="mix_ffn",
    )(y, wo, h, g, wgu, wd, gf)


def _ret_log_gamma(head):
    return math.log1p(-2.0 ** (-5.0 - head))


def _ret_chunk_kernel(q_ref, k_ref, v_ref, g_ref, o_ref, st_ref):
    c = pl.program_id(1)
    ln = q_ref.shape[0]

    @pl.when(c == 0)
    def _():
        st_ref[...] = jnp.zeros_like(st_ref)

    causal = (lax.broadcasted_iota(jnp.int32, (ln, ln), 1) <= lax.broadcasted_iota(jnp.int32, (ln, ln), 0))
    heads = range(RET_HEADS)
    q = [q_ref[:, hd * RET_DK:(hd + 1) * RET_DK] for hd in heads]
    k = [k_ref[:, hd * RET_DK:(hd + 1) * RET_DK] for hd in heads]
    v = [v_ref[:, hd * RET_DV:(hd + 1) * RET_DV] for hd in heads]
    st = [st_ref[hd] for hd in heads]
    qk = [_dot_nt(q[hd], k[hd]) for hd in heads]
    inter = [_dot(q[hd], st[hd].astype(BF16)) for hd in heads]
    scores = [jnp.where(causal, qk[hd], 0.0).astype(BF16) for hd in heads]
    out = [_dot(scores[hd], v[hd]) + inter[hd] for hd in heads]
    for hd in heads:
        st_ref[hd] = (st[hd] + _dot_tn(k[hd], v[hd])) * math.exp(_ret_log_gamma(hd) * ln)
    for hd in heads:
        sl = slice(hd * RET_DV, (hd + 1) * RET_DV)
        ms = jnp.mean(out[hd] * out[hd], axis=-1, keepdims=True)
        o_ref[:, sl] = (out[hd] * lax.rsqrt(ms + RET_EPS) * g_ref[:, sl].astype(F32)).astype(BF16)


def _ret_chunk(proj, batch, seq):
    t = proj.shape[0]
    ln = min(RET_CHUNK, seq)
    nc = seq // ln
    hk = RET_HEADS * RET_DK
    hv = RET_HEADS * RET_DV
    return pl.pallas_call(
        _ret_chunk_kernel,
        out_shape=jax.ShapeDtypeStruct((t, hv), BF16),
        grid=(batch, nc),
        in_specs=[
            pl.BlockSpec((ln, hk), lambda b, c: (b * nc + c, 0)),
            pl.BlockSpec((ln, hk), lambda b, c: (b * nc + c, 1)),
            pl.BlockSpec((ln, hv), lambda b, c: (b * nc + c, 1)),
            pl.BlockSpec((ln, hv), lambda b, c: (b * nc + c, 2)),
        ],
        out_specs=pl.BlockSpec((ln, hv), lambda b, c: (b * nc + c, 0)),
        scratch_shapes=[pltpu.VMEM((RET_HEADS, RET_DK, RET_DV), F32)],
        compiler_params=_params(("parallel", "arbitrary")),
        name="ret_chunk",
    )(proj, proj, proj, proj)


def _prev_rows(cur, prev8, shift):
    ext = jnp.concatenate([prev8, cur], axis=0)
    return pltpu.roll(ext, shift, axis=0)[V7X_SUBLANES:, :]


def _log_sigmoid(x):
    return jnp.minimum(x, 0.0) - jnp.log1p(jnp.exp(-jnp.abs(x)))


def _ml_front_kernel(h_ref, g_ref, w_ref, cw_ref, cb_ref, mq_ref, mk_ref, mv_ref, wgt_ref, bg_ref,
                     c_ref, q_ref, k_ref, v_ref, z_ref, gt_ref, tail_ref, *, tiles_per_seq):
    i = pl.program_id(0)
    tm = h_ref.shape[0]

    @pl.when((i % tiles_per_seq) == 0)
    def _():
        tail_ref[...] = jnp.zeros_like(tail_ref)

    xn = _rms(h_ref[...], g_ref[...]).astype(BF16)
    blk = mq_ref.shape[1]
    groups = list(range(0, ML_INNER, PROJ_CHUNK))
    st = {"gates": jnp.zeros((tm, V7X_LANES), F32) + bg_ref[...]}

    def project(c0):
        cs = slice(c0, c0 + PROJ_CHUNK)
        st["u", c0] = _dot(xn, w_ref[:, cs])
        z_ref[:, cs] = _silu(_dot(xn, w_ref[:, ML_INNER + c0:ML_INNER + c0 + PROJ_CHUNK])).astype(BF16)

    def conv(c0):
        cs = slice(c0, c0 + PROJ_CHUNK)
        u = st.pop(("u", c0))
        prev = tail_ref[:, cs]
        tail_ref[:, cs] = u[tm - V7X_SUBLANES:, :]
        acc = u * cw_ref[ML_CONV - 1:ML_CONV, cs] + cb_ref[:, cs]
        for s in range(1, ML_CONV):
            acc = acc + _prev_rows(u, prev, s) * cw_ref[ML_CONV - 1 - s:ML_CONV - s, cs]
        c16 = _silu(acc).astype(BF16)
        c_ref[:, cs] = c16
        st["c16", c0] = c16
        st["u16", c0] = u.astype(BF16)

    def maps(c0):
        c16 = st.pop(("c16", c0))
        u16 = st.pop(("u16", c0))
        gates = st["gates"]
        for n in range(PROJ_CHUNK // blk):
            ls = slice(n * blk, (n + 1) * blk)
            sl = slice(c0 + n * blk, c0 + (n + 1) * blk)
            tile = sl.start // blk
            qn = _dot(c16[:, ls], mq_ref[tile]).astype(BF16)
            kn = _dot(c16[:, ls], mk_ref[tile]).astype(BF16)
            vn = _dot(u16[:, ls], mv_ref[tile]).astype(BF16)
            q_ref[:, sl] = qn
            k_ref[:, sl] = (kn.astype(F32) * (ML_DH ** -0.5)).astype(BF16)
            v_ref[:, sl] = vn
            gates = (gates + _dot(qn, wgt_ref[0, sl, :]) + _dot(kn, wgt_ref[1, sl, :])
                     + _dot(vn, wgt_ref[2, sl, :]))
        st["gates"] = gates

    project(groups[0])
    for gi, c0 in enumerate(groups):
        if gi + 1 < len(groups):
            project(groups[gi + 1])
        conv(c0)
        maps(c0)
    gates = st["gates"]
    lane = lax.broadcasted_iota(jnp.int32, gates.shape, 1)
    gt_ref[...] = jnp.where(jnp.logical_and(lane >= ML_HEADS, lane < 2 * ML_HEADS),
                            _log_sigmoid(gates), gates)


def _ml_front(h, g, w, conv_w, conv_b, mq, mk, mv, wgt, bg, seq):
    t, d = h.shape
    tm = min(TM_PROJ, seq)
    tps = seq // tm
    act = jax.ShapeDtypeStruct((t, ML_INNER), BF16)
    row_spec = pl.BlockSpec((tm, ML_INNER), lambda i: (i, 0))
    return pl.pallas_call(
        functools.partial(_ml_front_kernel, tiles_per_seq=tps),
        out_shape=(act, act, act, act, act, jax.ShapeDtypeStruct((t, V7X_LANES), F32)),
        grid=(t // tm,),
        in_specs=[
            pl.BlockSpec((tm, d), lambda i: (i, 0)),
            _const_spec((1, d)),
            _const_spec(w.shape),
            _const_spec((ML_CONV, ML_INNER)),
            _const_spec((1, ML_INNER)),
            _const_spec(mq.shape), _const_spec(mk.shape), _const_spec(mv.shape),
            _const_spec(wgt.shape),
            _const_spec((1, V7X_LANES)),
        ],
        out_specs=(row_spec, row_spec, row_spec, row_spec, row_spec,
                   pl.BlockSpec((tm, V7X_LANES), lambda i: (i, 0))),
        scratch_shapes=[pltpu.VMEM((V7X_SUBLANES, ML_INNER), F32)],
        compiler_params=_params(("arbitrary",)),
        name="ml_front",
    )(h, g, w, conv_w, conv_b, mq, mk, mv, wgt, bg)


def _ml_chunk_kernel(q_ref, k_ref, v_ref, gt_ref, c_ref, z_ref, gn_ref, sk_ref, o_ref,
                     cst_ref, nst_ref, mst_ref):
    ci = pl.program_id(1)
    nb, ln, _ = q_ref.shape

    @pl.when(ci == 0)
    def _():
        cst_ref[...] = jnp.zeros_like(cst_ref)
        nst_ref[...] = jnp.zeros_like(nst_ref)
        mst_ref[...] = jnp.zeros_like(mst_ref)

    row = lax.broadcasted_iota(jnp.int32, (ln, ln), 0)
    col = lax.broadcasted_iota(jnp.int32, (ln, ln), 1)
    causal = col <= row
    tril = causal.astype(F32)
    heads = range(ML_HEADS)
    sls = [slice(hd * ML_DH, (hd + 1) * ML_DH) for hd in heads]

    def row_stages(bi):
        st = {}

        def load():
            gates = gt_ref[bi]
            cum = _dot_exact(tril, gates)
            st["gates"], st["cum"], st["gates_t"], st["cum_t"] = gates, cum, gates.T, cum.T
            st["q"] = [q_ref[bi, :, sl] for sl in sls]
            st["k"] = [k_ref[bi, :, sl] for sl in sls]
            st["v"] = [v_ref[bi, :, sl] for sl in sls]
            st["cst"] = [cst_ref[bi, hd] for hd in heads]
            st["n_row"] = [nst_ref[bi, hd:hd + 1, :] for hd in heads]
            st["m_prev"] = [mst_ref[bi, hd:hd + 1, 0:1] for hd in heads]
            st["b_col"] = [cum[:, ML_HEADS + hd:ML_HEADS + hd + 1] for hd in heads]

        def scores():
            st["qk"] = [_dot_nt(st["q"][hd], st["k"][hd]) for hd in heads]
            st["inter"] = [_dot(st["q"][hd], st["cst"][hd].astype(BF16)) for hd in heads]

        def decay():
            b_col, m_prev = st["b_col"], st["m_prev"]
            log_d = [jnp.where(causal, b_col[hd] - st["cum_t"][ML_HEADS + hd:ML_HEADS + hd + 1, :]
                               + st["gates_t"][hd:hd + 1, :], -jnp.inf) for hd in heads]
            log_inter = [b_col[hd] + m_prev[hd] for hd in heads]
            st["m_row"] = [jnp.maximum(log_inter[hd], jnp.max(log_d[hd], axis=-1, keepdims=True)) for hd in heads]
            st["w_inter"] = [jnp.exp(log_inter[hd] - st["m_row"][hd]) for hd in heads]
            st["s"] = [st["qk"][hd] * jnp.exp(log_d[hd] - st["m_row"][hd]) for hd in heads]

        def readout():
            s, q = st["s"], st["q"]
            num = [_dot(s[hd].astype(BF16), st["v"][hd]) + st["w_inter"][hd] * st["inter"][hd] for hd in heads]
            den = [jnp.sum(s[hd], axis=-1, keepdims=True) + st["w_inter"][hd] * jnp.sum(
                q[hd].astype(F32) * st["n_row"][hd], axis=-1, keepdims=True) for hd in heads]
            st["hh"] = [num[hd] / jnp.maximum(jnp.abs(den[hd]), jnp.exp(-st["m_row"][hd])) for hd in heads]

        def state():
            b_col, m_prev = st["b_col"], st["m_prev"]
            for hd in heads:
                b_last = b_col[hd][ln - 1:ln, :]
                log_w = b_last - b_col[hd] + st["gates"][:, hd:hd + 1]
                m_new = jnp.maximum(b_last + m_prev[hd], jnp.max(log_w, axis=0, keepdims=True))
                kw = st["k"][hd].astype(F32) * jnp.exp(log_w - m_new)
                carry = jnp.exp(b_last + m_prev[hd] - m_new)
                cst_ref[bi, hd] = carry * st["cst"][hd] + _dot_tn(kw.astype(BF16), st["v"][hd])
                nst_ref[bi, hd:hd + 1, :] = carry * st["n_row"][hd] + jnp.sum(kw, axis=0, keepdims=True)
                mst_ref[bi, hd:hd + 1, :] = jnp.broadcast_to(m_new, (1, V7X_LANES))

        def epilogue(hd):
            sl = sls[hd]
            hh = st["hh"][hd]
            hc = hh - jnp.mean(hh, axis=-1, keepdims=True)
            y = hc * lax.rsqrt(jnp.mean(hc * hc, axis=-1, keepdims=True) + ML_EPS) * gn_ref[:, sl]
            y = (y + sk_ref[:, sl] * c_ref[bi, :, sl].astype(F32)) * z_ref[bi, :, sl].astype(F32)
            o_ref[bi, :, sl] = y.astype(BF16)

        return [load, scores, decay, readout, state], [functools.partial(epilogue, hd) for hd in heads]

    pending = []
    for bi in range(nb):
        stages, epilogues = row_stages(bi)
        for i, stage in enumerate(stages):
            stage()
            if i < len(pending):
                pending[i]()
        for piece in pending[len(stages):]:
            piece()
        pending = epilogues
    for piece in pending:
        piece()


def _ml_chunk(q, k, v, gates, c, z, gn, skip, batch, seq):
    t = q.shape[0]
    ln = min(ML_CHUNK, seq)
    nc = seq // ln
    nb = math.gcd(batch, ML_BATCH_PER_STEP)
    row_spec = pl.BlockSpec((nb, ln, ML_INNER), lambda b, ci: (b, ci, 0))
    vec = pl.BlockSpec((1, ML_INNER), lambda b, ci: (0, 0))
    acts = [x.reshape(batch, seq, ML_INNER) for x in (q, k, v)]
    out = pl.pallas_call(
        _ml_chunk_kernel,
        out_shape=jax.ShapeDtypeStruct((batch, seq, ML_INNER), BF16),
        grid=(batch // nb, nc),
        in_specs=[
            row_spec, row_spec, row_spec,
            pl.BlockSpec((nb, ln, V7X_LANES), lambda b, ci: (b, ci, 0)),
            row_spec,
            row_spec,
            vec, vec,
        ],
        out_specs=row_spec,
        scratch_shapes=[
            pltpu.VMEM((nb, ML_HEADS, ML_DH, ML_DH), F32),
            pltpu.VMEM((nb, V7X_SUBLANES, ML_DH), F32),
            pltpu.VMEM((nb, V7X_SUBLANES, V7X_LANES), F32),
        ],
        compiler_params=_params(("parallel", "arbitrary")),
        name="ml_chunk",
    )(*acts, gates.reshape(batch, seq, V7X_LANES), c.reshape(batch, seq, ML_INNER),
      z.reshape(batch, seq, ML_INNER), gn, skip)
    return out.reshape(t, ML_INNER)


def _rw_pre_kernel(h_ref, hp_ref, g_ref, mu_ref, wrkv_ref, la_ref, lbw_ref, lba_ref, lbg_ref,
                   w0_ref, a0_ref, kk_ref, ka_ref,
                   r_out, lw_out, k_out, v_out, kk_out, kb_out, g_out, *, tiles_per_seq,
                   lora_w, lora_a):
    i = pl.program_id(0)
    tm = h_ref.shape[0]
    g = g_ref[...]
    xn = _rms(h_ref[...], g)
    first = (i % tiles_per_seq) == 0
    xp8 = jnp.where(first, 0.0, _rms(hp_ref[...], g))
    dx = _prev_rows(xn, xp8, 1) - xn

    def matmuls(rows):
        def mix(n):
            return (xn[rows] + dx[rows] * mu_ref[n:n + 1, :]).astype(BF16)

        out = {"r": _dot(mix(0), wrkv_ref[0]), "k": _dot(mix(1), wrkv_ref[1]), "v": _dot(mix(2), wrkv_ref[2])}
        hw = jnp.tanh(_dot(mix(3), la_ref[:, 0:lora_w])).astype(BF16)
        ha = _dot(mix(4), la_ref[:, lora_w:lora_w + lora_a]).astype(BF16)
        hg = jax.nn.sigmoid(_dot(mix(5), la_ref[:, lora_w + lora_a:])).astype(BF16)
        out["w_pre"] = _dot(hw, lbw_ref[...])
        out["a_pre"] = _dot(ha, lba_ref[...])
        g_out[rows, :] = _dot(hg, lbg_ref[...]).astype(BF16)
        return out

    def tail(rows, m):
        lw_out[rows, :] = -math.exp(-0.5) * jax.nn.sigmoid(w0_ref[...] + m["w_pre"])
        alpha = jax.nn.sigmoid(a0_ref[...] + m["a_pre"])
        kk = m["k"] * kk_ref[...]
        r_out[rows, :] = m["r"].astype(BF16)
        k_out[rows, :] = (m["k"] * (1.0 + (alpha - 1.0) * ka_ref[...])).astype(BF16)
        v_out[rows, :] = m["v"].astype(BF16)
        kk_out[rows, :] = kk.astype(BF16)
        kb_out[rows, :] = (kk * alpha).astype(BF16)

    halves = [slice(0, tm // 2), slice(tm // 2, tm)]
    first_half = matmuls(halves[0])
    second_half = matmuls(halves[1])
    tail(halves[0], first_half)
    tail(halves[1], second_half)


def _rw_pre(h, g, mu, wrkv, la, lbw, lba, lbg, w0, a0, k_k, k_a, seq, lora_w, lora_a):
    t, d = h.shape
    tm = min(TM_PRE, seq)
    tps = seq // tm
    rb = tm // V7X_SUBLANES
    row_spec = pl.BlockSpec((tm, d), lambda i: (i, 0))
    act = jax.ShapeDtypeStruct((t, d), BF16)
    vec = _const_spec((1, d))
    return pl.pallas_call(
        functools.partial(_rw_pre_kernel, tiles_per_seq=tps, lora_w=lora_w, lora_a=lora_a),
        out_shape=(act, jax.ShapeDtypeStruct((t, d), F32), act, act, act, act, act),
        grid=(t // tm,),
        in_specs=[
            row_spec,
            pl.BlockSpec((V7X_SUBLANES, d), lambda i: (jnp.maximum(i * rb - 1, 0), 0)),
            vec,
            _const_spec(mu.shape),
            _const_spec(wrkv.shape),
            _const_spec(la.shape),
            _const_spec(lbw.shape), _const_spec(lba.shape), _const_spec(lbg.shape),
            vec, vec, vec, vec,
        ],
        out_specs=(row_spec,) * 7,
        compiler_params=_params(("parallel",)),
        name="rw_pre",
    )(h, h, g, mu, wrkv, la, lbw, lba, lbg, w0, a0, k_k, k_a)


def _rw_chunk_kernel(r_ref, lw_ref, k_ref, v_ref, kk_ref, kb_ref, g_ref, rk_ref, gng_ref, gnb_ref,
                     o_ref, st_ref):
    ci = pl.program_id(1)
    nb, ln, _ = r_ref.shape
    pair = 2 * RW_DH

    @pl.when(ci == 0)
    def _():
        st_ref[...] = jnp.zeros_like(st_ref)

    trow = lax.broadcasted_iota(jnp.int32, (ln, ln), 0)
    tcol = lax.broadcasted_iota(jnp.int32, (ln, ln), 1)
    tril = (tcol <= trow).astype(F32)
    lane = lax.broadcasted_iota(jnp.int32, (ln, pair), 1)
    head0 = lane < RW_DH
    srow = lax.broadcasted_iota(jnp.int32, (2 * ln, 2 * ln), 0)
    scol = lax.broadcasted_iota(jnp.int32, (2 * ln, 2 * ln), 1)
    strict_t = srow < scol
    incl_t = srow <= scol

    def stack(x):
        return jnp.concatenate([jnp.where(head0, x, 0.0), jnp.where(head0, 0.0, x)], axis=0).astype(BF16)

    pair_slices = [slice(p * pair, (p + 1) * pair) for p in range(RW_HEADS // 2)]
    n_steps = int(math.log2(ln))

    def head_sums(x):
        s0 = jnp.sum(jnp.where(head0, x, 0.0), axis=-1, keepdims=True)
        s1 = jnp.sum(jnp.where(head0, 0.0, x), axis=-1, keepdims=True)
        return s0, s1

    def make_group(rows):
        chains = [(bi, p) for bi in rows for p in range(RW_HEADS // 2)]
        ids = range(len(chains))
        gs = {k_: [] for k_ in ("ar_s", "b_s", "k_s", "bk", "v_t", "g_last", "bonus")}

        def prologue(bi):
            lw = lw_ref[bi]
            cum = _dot_exact(tril, lw)
            cum_last = cum[ln - 1:ln, :]
            g_inv = jnp.exp(-cum)
            g_rem = jnp.exp(cum_last - cum)
            kk_raw = kk_ref[bi].astype(F32)
            r_all = r_ref[bi].astype(F32)
            kk = k_ref[bi].astype(F32)
            v_all = v_ref[bi].astype(F32)
            rkk = r_all * kk * rk_ref[...]
            inv_parts = []
            for sl in pair_slices:
                s0, s1 = head_sums(kk_raw[:, sl] * kk_raw[:, sl])
                inv_parts.append(jnp.where(head0, lax.rsqrt(jnp.maximum(s0, 1e-24)),
                                           lax.rsqrt(jnp.maximum(s1, 1e-24))))
                t0, t1 = head_sums(rkk[:, sl])
                gs["bonus"].append(jnp.where(head0, t0, t1) * v_all[:, sl])
            inv_all = jnp.concatenate(inv_parts, axis=1)
            kn = kk_raw * inv_all
            bb = kb_ref[bi].astype(F32) * inv_all
            a_all = -kn * jnp.exp(cum - lw)
            rt_all = r_all * jnp.exp(cum)
            bt_all = bb * g_inv
            kt_all = kk * g_inv
            bh_all = bb * g_rem
            kh_all = kk * g_rem
            gl = jnp.exp(cum_last)
            for sl in pair_slices:
                gs["ar_s"].append(jnp.concatenate([stack(a_all[:, sl]), stack(rt_all[:, sl])], axis=0))
                gs["b_s"].append(stack(bt_all[:, sl]))
                gs["k_s"].append(stack(kt_all[:, sl]))
                gs["bk"].append(jnp.concatenate([stack(bh_all[:, sl]), stack(kh_all[:, sl])], axis=0))
                vt = v_all[:, sl].T
                gs["v_t"].append(jnp.concatenate([vt[:RW_DH], vt[RW_DH:]], axis=1).astype(BF16))
                gs["g_last"].append(gl[:, sl])

        def products():
            gs["st"] = [st_ref[bi, p] for bi, p in chains]
            gs["m"] = [_dot_nt(jnp.concatenate([gs["b_s"][c], gs["k_s"][c], gs["st"][c].astype(BF16)], axis=0),
                               gs["ar_s"][c]) for c in ids]

        def masks():
            m = gs["m"]
            gs["pw"] = [jnp.where(strict_t, m[c][:2 * ln, :2 * ln], 0.0) for c in ids]
            ak = [jnp.where(strict_t, m[c][2 * ln:4 * ln, :2 * ln], 0.0).astype(BF16) for c in ids]
            gs["rbk"] = [jnp.concatenate([jnp.where(incl_t, m[c][:2 * ln, 2 * ln:], 0.0).astype(BF16),
                                          jnp.where(incl_t, m[c][2 * ln:4 * ln, 2 * ln:], 0.0).astype(BF16)],
                                         axis=0) for c in ids]
            gs["inter_r"] = [m[c][4 * ln:, 2 * ln:] for c in ids]
            gs["u"] = [m[c][4 * ln:, :2 * ln] + _dot(gs["v_t"][c], ak[c]) for c in ids]

        def neumann(step):
            u, pw = gs["u"], gs["pw"]
            p16 = [pw[c].astype(BF16) for c in ids]
            if step + 1 < n_steps:
                both = [_dot(jnp.concatenate([u[c].astype(BF16), p16[c]], axis=0), p16[c]) for c in ids]
                gs["u"] = [u[c] + both[c][:RW_DH] for c in ids]
                gs["pw"] = [both[c][RW_DH:] for c in ids]
            else:
                gs["u"] = [u[c] + _dot(u[c].astype(BF16), p16[c]) for c in ids]

        def readout():
            uv = [jnp.concatenate([gs["u"][c].astype(BF16), gs["v_t"][c]], axis=1) for c in ids]
            gs["y_t"] = [gs["inter_r"][c] + _dot(uv[c], gs["rbk"][c]) for c in ids]
            for c, (bi, p) in enumerate(chains):
                st_ref[bi, p] = gs["st"][c] * gs["g_last"][c] + _dot(uv[c], gs["bk"][c])

        def epilogue(c):
            bi, p = chains[c]
            sl = pair_slices[p]
            y_t = gs["y_t"][c]
            yc = y_t - jnp.mean(y_t, axis=0, keepdims=True)
            yn = (yc * lax.rsqrt(jnp.mean(yc * yc, axis=0, keepdims=True) + RW_GN_EPS)).T
            y_p = jnp.concatenate([yn[:ln], yn[ln:]], axis=1)
            out = y_p * gng_ref[:, sl] + gnb_ref[:, sl] + gs["bonus"][c]
            o_ref[bi, :, sl] = (out * g_ref[bi, :, sl].astype(F32)).astype(BF16)

        stages = [products, masks] + [functools.partial(neumann, s_) for s_ in range(n_steps)] + [readout]
        return ([functools.partial(prologue, bi) for bi in rows], stages,
                [functools.partial(epilogue, c) for c in ids])

    def interleave(stages, fillers):
        per = -(-len(fillers) // max(len(stages), 1)) if fillers else 0
        fillers = list(fillers)
        for stage in stages:
            stage()
            for piece in fillers[:per]:
                piece()
            fillers = fillers[per:]
        for piece in fillers:
            piece()

    half = max(nb // 2, 1)
    groups = [make_group(range(g0, min(g0 + half, nb))) for g0 in range(0, nb, half)]
    for piece in groups[0][0]:
        piece()
    pending = []
    for gi, (_, stages, epilogues) in enumerate(groups):
        nxt = groups[gi + 1][0] if gi + 1 < len(groups) else []
        interleave(stages, list(pending) + list(nxt))
        pending = epilogues
    for piece in pending:
        piece()


def _rw_chunk(r, lw, k, v, kk, kb, g, r_k, gn_g, gn_b, batch, seq):
    t, d = r.shape
    ln = min(RW_CHUNK, seq)
    nc = seq // ln
    nb = math.gcd(batch, RW_BATCH_PER_STEP)
    row_spec = pl.BlockSpec((nb, ln, d), lambda bi, ci: (bi, ci, 0))
    vec = pl.BlockSpec((1, d), lambda bi, ci: (0, 0))
    acts = [x.reshape(batch, seq, d) for x in (r, lw, k, v, kk, kb, g)]
    out = pl.pallas_call(
        _rw_chunk_kernel,
        out_shape=jax.ShapeDtypeStruct((batch, seq, d), BF16),
        grid=(batch // nb, nc),
        in_specs=[row_spec] * 7 + [vec, vec, vec],
        out_specs=row_spec,
        scratch_shapes=[pltpu.VMEM((nb, RW_HEADS // 2, RW_DH, 2 * RW_DH), F32)],
        compiler_params=_params(("parallel", "arbitrary")),
        name="rw_chunk",
    )(*acts, r_k, gn_g, gn_b)
    return out.reshape(t, d)


def _rope_tables(seq):
    half = RET_DK // 2
    pos = jnp.arange(seq, dtype=F32)
    inv_freq = 1.0 / (ROPE_BASE ** jnp.linspace(0.0, 1.0, half, dtype=F32))
    ang = pos[:, None] * inv_freq[None, :]
    local = (jnp.arange(seq) % min(RET_CHUNK, seq)).astype(F32) + 1.0
    log_gamma = jnp.asarray([_ret_log_gamma(hd) for hd in range(RET_HEADS)], F32)
    q_scale = jnp.exp(log_gamma[:, None] * local[None, :])
    k_scale = jnp.exp(-log_gamma[:, None] * local[None, :]) * RET_DK ** -0.5
    scale = jnp.concatenate([q_scale, k_scale], axis=0)[:, :, None]
    return jnp.cos(ang)[None] * scale, jnp.sin(ang)[None] * scale


def _retention_mixer(h, norm_g, w_in16, layer, batch, seq):
    cos, sin = _rope_tables(seq)
    proj = _ret_proj(h, norm_g[None, :], w_in16, layer, cos, sin, seq)
    return _ret_chunk(proj, batch, seq)


def _block_diag_tiles(w, tile):
    nb, bs, _ = w.shape
    rows = w.reshape(nb * bs // tile, tile, bs)
    dense = jnp.tile(rows, (1, 1, tile // bs))
    idx = jnp.arange(tile) // bs
    return jnp.where(idx[:, None] == idx[None, :], dense, 0.0)


def _mlstm_mixer(h, norm_g, w_in, conv_w, conv_b, wq, wk, wv, w_gate, b_gate, gn, skip, batch, seq):
    tile = 2 * V7X_LANES
    mq = _block_diag_tiles(wq, tile).astype(BF16)
    mk = _block_diag_tiles(wk, tile).astype(BF16)
    mv = _block_diag_tiles(wv, tile).astype(BF16)
    n_gate = w_gate.shape[1]
    wgt = jnp.pad(w_gate.reshape(3, ML_INNER, n_gate), ((0, 0), (0, 0), (0, V7X_LANES - n_gate))).astype(BF16)
    bg = jnp.pad(b_gate, (0, V7X_LANES - n_gate))[None, :].astype(F32)
    c, q, k, v, z, gates = _ml_front(h, norm_g[None, :], w_in.astype(BF16), conv_w, conv_b[None, :],
                                     mq, mk, mv, wgt, bg, seq)
    return _ml_chunk(q, k, v, gates, c, z, gn[None, :], skip[None, :], batch, seq)


def _pad_to(x, axis, size):
    pad = [(0, 0)] * x.ndim
    pad[axis] = (0, size - x.shape[axis])
    return jnp.pad(x, pad)


def _rwkv_mixer(h, norm_g, mu, w_rkv, w0, w_la, w_lb, a0, a_la, a_lb, g_la, g_lb, k_k, k_a, r_k,
                gn_g, gn_b, batch, seq):
    d = h.shape[1]
    lw_ = -(-w_la.shape[1] // V7X_LANES) * V7X_LANES
    la_ = -(-a_la.shape[1] // V7X_LANES) * V7X_LANES
    lg_ = -(-g_la.shape[1] // V7X_LANES) * V7X_LANES
    la = jnp.concatenate([_pad_to(w_la, 1, lw_), _pad_to(a_la, 1, la_), _pad_to(g_la, 1, lg_)],
                         axis=1).astype(BF16)
    lbw = _pad_to(w_lb, 0, lw_).astype(BF16)
    lba = _pad_to(a_lb, 0, la_).astype(BF16)
    lbg = _pad_to(g_lb, 0, lg_).astype(BF16)
    r, lw, k, v, kk, kb, g = _rw_pre(h, norm_g[None, :], mu, w_rkv.astype(BF16), la, lbw, lba, lbg,
                                     w0[None, :], a0[None, :], k_k[None, :], k_a[None, :], seq, lw_, la_)
    return _rw_chunk(r, lw, k, v, kk, kb, g, r_k.reshape(1, d), gn_g[None, :], gn_b[None, :], batch, seq)


def kernel(x, norm_mix, norm_ffn, norm_final, ret_w_in, ret_gn, ret_w_out, ml_w_in, ml_conv_w, ml_conv_b, ml_wq, ml_wk, ml_wv, ml_w_gate, ml_b_gate, ml_gn, ml_skip, ml_w_out, rw_mu, rw_w_rkv, rw_w0, rw_w_lora_a, rw_w_lora_b, rw_a0, rw_a_lora_a, rw_a_lora_b, rw_g_lora_a, rw_g_lora_b, rw_k_k, rw_k_a, rw_r_k, rw_gn_g, rw_gn_b, rw_w_out, ffn_w_gu, ffn_w_down):
    batch, seq, d = x.shape
    depth = norm_mix.shape[0]
    h = x.reshape(batch * seq, d)
    ret_w_in16 = ret_w_in.astype(BF16)
    w_out16 = ((ret_gn[:, :, None] * ret_w_out).astype(BF16), ml_w_out.astype(BF16), rw_w_out.astype(BF16))
    w_gu16 = ffn_w_gu.astype(BF16)
    w_down16 = ffn_w_down.astype(BF16)
    for i in range(depth):
        kind, j = i % 3, i // 3
        if kind == 0:
            y = _retention_mixer(h, norm_mix[i], ret_w_in16, j, batch, seq)
        elif kind == 1:
            y = _mlstm_mixer(h, norm_mix[i], ml_w_in[j], ml_conv_w[j], ml_conv_b[j], ml_wq[j], ml_wk[j],
                             ml_wv[j], ml_w_gate[j], ml_b_gate[j], ml_gn[j], ml_skip[j], batch, seq)
        else:
            y = _rwkv_mixer(h, norm_mix[i], rw_mu[j], rw_w_rkv[j], rw_w0[j], rw_w_lora_a[j],
                            rw_w_lora_b[j], rw_a0[j], rw_a_lora_a[j], rw_a_lora_b[j], rw_g_lora_a[j],
                            rw_g_lora_b[j], rw_k_k[j], rw_k_a[j], rw_r_k[j], rw_gn_g[j], rw_gn_b[j],
                            batch, seq)
        h = _mix_ffn(y, w_out16[kind], j, h, norm_ffn[i][None, :], w_gu16, w_down16, i,
                     norm_final[None, :], i == depth - 1)
    return h.reshape(batch, seq, d)
```

```python
import functools
import math

import jax
import jax.numpy as jnp
from jax import lax
from jax.experimental import pallas as pl
from jax.experimental.pallas import tpu as pltpu

F32 = jnp.float32
BF16 = jnp.bfloat16

D_MODEL = 1024
D_FF = 2816
RMS_EPS = 1e-6

RET_HEADS = 4
RET_DK = 256
RET_DV = 512
RET_EPS = 1e-6
ROPE_BASE = 10000.0

ML_INNER = 2048
ML_HEADS = 4
ML_DH = 512
ML_CONV = 4
ML_EPS = 1e-6

RW_HEADS = 16
RW_DH = 64
RW_GN_EPS = 64e-5

V7X_LANES = 128
V7X_SUBLANES = 8
V7X_VMEM_LIMIT = 56 * 1024 * 1024

TM_PROJ = 512
TM_RET_PROJ = 512
PROJ_CHUNK = 512
TM_FFN = 1024
FFN_CHUNK = 256
TM_PRE = 512
RET_CHUNK = 256
ML_CHUNK = 256
ML_BATCH_PER_STEP = 2
RW_CHUNK = 64
RW_BLOCK = 16
RW_BATCH_PER_STEP = 4


def _params(sem, vmem=V7X_VMEM_LIMIT):
    return pltpu.CompilerParams(dimension_semantics=sem, vmem_limit_bytes=vmem)


def _dot(a, b):
    return jnp.dot(a, b, preferred_element_type=F32)


def _dot_nt(a, b):
    return lax.dot_general(a, b, (((1,), (1,)), ((), ())), preferred_element_type=F32)


def _dot_tn(a, b):
    return lax.dot_general(a, b, (((0,), (0,)), ((), ())), preferred_element_type=F32)


def _dot_exact(a, b):
    hi = b.astype(BF16)
    lo = (b - hi.astype(F32)).astype(BF16)
    a16 = a.astype(BF16)
    return _dot(a16, hi) + _dot(a16, lo)


def _rms(x, g):
    ms = jnp.mean(x * x, axis=-1, keepdims=True)
    return x * lax.rsqrt(ms + RMS_EPS) * g


def _silu(x):
    return x * jax.nn.sigmoid(x)


def _const_spec(shape):
    nd = len(shape)
    return pl.BlockSpec(shape, lambda *_: (0,) * nd, pipeline_mode=pl.Buffered(1))


def _layer_spec(shape, layer):
    nd = len(shape)
    return pl.BlockSpec((None,) + tuple(shape), lambda *_: (layer,) + (0,) * nd,
                        pipeline_mode=pl.Buffered(1))


def _ret_proj_kernel(h_ref, g_ref, w_ref, cos_ref, sin_ref, o_ref):
    xn = _rms(h_ref[...], g_ref[...]).astype(BF16)
    half = RET_DK // 2
    hk = RET_HEADS * RET_DK
    hv = RET_HEADS * RET_DV
    for hd in range(2 * RET_HEADS):
        c0 = hd * RET_DK
        acc = _dot(xn, w_ref[:, c0:c0 + RET_DK])
        cos = cos_ref[hd]
        sin = sin_ref[hd]
        t1 = acc[:, :half]
        t2 = acc[:, half:]
        o_ref[:, c0:c0 + half] = (t1 * cos - t2 * sin).astype(BF16)
        o_ref[:, c0 + half:c0 + RET_DK] = (t2 * cos + t1 * sin).astype(BF16)
    for c0 in range(2 * hk, 2 * hk + hv, PROJ_CHUNK):
        o_ref[:, c0:c0 + PROJ_CHUNK] = _dot(xn, w_ref[:, c0:c0 + PROJ_CHUNK]).astype(BF16)
    for c0 in range(2 * hk + hv, 2 * hk + 2 * hv, PROJ_CHUNK):
        o_ref[:, c0:c0 + PROJ_CHUNK] = _silu(_dot(xn, w_ref[:, c0:c0 + PROJ_CHUNK])).astype(BF16)


def _ret_proj(h, g, w, layer, cos, sin, seq):
    t, d = h.shape
    n = w.shape[2]
    tm = min(TM_RET_PROJ, seq)
    spt = seq // tm
    return pl.pallas_call(
        _ret_proj_kernel,
        out_shape=jax.ShapeDtypeStruct((t, n), BF16),
        grid=(t // tm,),
        in_specs=[
            pl.BlockSpec((tm, d), lambda i: (i, 0)),
            _const_spec((1, d)),
            _layer_spec((d, n), layer),
            pl.BlockSpec((2 * RET_HEADS, tm, RET_DK // 2), lambda i: (0, i % spt, 0)),
            pl.BlockSpec((2 * RET_HEADS, tm, RET_DK // 2), lambda i: (0, i % spt, 0)),
        ],
        out_specs=pl.BlockSpec((tm, n), lambda i: (i, 0)),
        compiler_params=_params(("parallel",)),
        name="ret_proj",
    )(h, g, w, cos, sin)


def _mix_ffn_kernel(y_ref, wo_ref, h_ref, g_ref, wgu_ref, wd_ref, gf_ref, o_ref, *, final_norm):
    h = h_ref[...] + _dot(y_ref[...], wo_ref[...])
    xn = _rms(h, g_ref[...]).astype(BF16)
    acc = h
    for f in range(0, D_FF, FFN_CHUNK):
        gate = _dot(xn, wgu_ref[:, f:f + FFN_CHUNK])
        up = _dot(xn, wgu_ref[:, D_FF + f:D_FF + f + FFN_CHUNK])
        act = (_silu(gate) * up).astype(BF16)
        acc = acc + _dot(act, wd_ref[f:f + FFN_CHUNK, :])
    if final_norm:
        acc = _rms(acc, gf_ref[...])
    o_ref[...] = acc


def _mix_ffn(y, wo, wo_layer, h, g, wgu, wd, ffn_layer, gf, final_norm):
    t, d = h.shape
    k = y.shape[1]
    tm = min(TM_FFN, t)
    return pl.pallas_call(
        functools.partial(_mix_ffn_kernel, final_norm=final_norm),
        out_shape=jax.ShapeDtypeStruct((t, d), F32),
        grid=(t // tm,),
        in_specs=[
            pl.BlockSpec((tm, k), lambda i: (i, 0)),
            _layer_spec((k, d), wo_layer),
            pl.BlockSpec((tm, d), lambda i: (i, 0)),
            _const_spec((1, d)),
            _layer_spec((d, 2 * D_FF), ffn_layer),
            _layer_spec((D_FF, d), ffn_layer),
            _const_spec((1, d)),
        ],
        out_specs=pl.BlockSpec((tm, d), lambda i: (i, 0)),
        compiler_params=_params(("parallel",)),
        name="mix_ffn",
    )(y, wo, h, g, wgu, wd, gf)


def _ret_log_gamma(head):
    return math.log1p(-2.0 ** (-5.0 - head))


def _ret_chunk_kernel(q_ref, k_ref, v_ref, g_ref, gn_ref, o_ref, st_ref):
    c = pl.program_id(1)
    ln = q_ref.shape[0]

    @pl.when(c == 0)
    def _():
        st_ref[...] = jnp.zeros_like(st_ref)

    causal = (lax.broadcasted_iota(jnp.int32, (ln, ln), 1) <= lax.broadcasted_iota(jnp.int32, (ln, ln), 0))
    heads = range(RET_HEADS)
    q = [q_ref[:, hd * RET_DK:(hd + 1) * RET_DK] for hd in heads]
    k = [k_ref[:, hd * RET_DK:(hd + 1) * RET_DK] for hd in heads]
    v = [v_ref[:, hd * RET_DV:(hd + 1) * RET_DV] for hd in heads]
    st = [st_ref[hd] for hd in heads]
    qk = [_dot_nt(q[hd], k[hd]) for hd in heads]
    inter = [_dot(q[hd], st[hd].astype(BF16)) for hd in heads]
    scores = [jnp.where(causal, qk[hd], 0.0).astype(BF16) for hd in heads]
    out = [_dot(scores[hd], v[hd]) + inter[hd] for hd in heads]
    for hd in heads:
        st_ref[hd] = (st[hd] + _dot_tn(k[hd], v[hd])) * math.exp(_ret_log_gamma(hd) * ln)
    for hd in heads:
        sl = slice(hd * RET_DV, (hd + 1) * RET_DV)
        ms = jnp.mean(out[hd] * out[hd], axis=-1, keepdims=True)
        y = out[hd] * lax.rsqrt(ms + RET_EPS) * gn_ref[:, sl]
        o_ref[:, sl] = (y * g_ref[:, sl].astype(F32)).astype(BF16)


def _ret_chunk(proj, gn, batch, seq):
    t = proj.shape[0]
    ln = min(RET_CHUNK, seq)
    nc = seq // ln
    hk = RET_HEADS * RET_DK
    hv = RET_HEADS * RET_DV
    return pl.pallas_call(
        _ret_chunk_kernel,
        out_shape=jax.ShapeDtypeStruct((t, hv), BF16),
        grid=(batch, nc),
        in_specs=[
            pl.BlockSpec((ln, hk), lambda b, c: (b * nc + c, 0)),
            pl.BlockSpec((ln, hk), lambda b, c: (b * nc + c, 1)),
            pl.BlockSpec((ln, hv), lambda b, c: (b * nc + c, 1)),
            pl.BlockSpec((ln, hv), lambda b, c: (b * nc + c, 2)),
            _const_spec((1, hv)),
        ],
        out_specs=pl.BlockSpec((ln, hv), lambda b, c: (b * nc + c, 0)),
        scratch_shapes=[pltpu.VMEM((RET_HEADS, RET_DK, RET_DV), F32)],
        compiler_params=_params(("parallel", "arbitrary")),
        name="ret_chunk",
    )(proj, proj, proj, proj, gn)


def _prev_rows(cur, prev8, shift):
    ext = jnp.concatenate([prev8, cur], axis=0)
    return pltpu.roll(ext, shift, axis=0)[V7X_SUBLANES:, :]


def _log_sigmoid(x):
    return jnp.minimum(x, 0.0) - jnp.log1p(jnp.exp(-jnp.abs(x)))


def _ml_front_kernel(h_ref, g_ref, w_ref, cw_ref, cb_ref, mq_ref, mk_ref, mv_ref, wgt_ref, bg_ref,
                     c_ref, q_ref, k_ref, v_ref, z_ref, gt_ref, tail_ref, *, tiles_per_seq):
    i = pl.program_id(0)
    tm = h_ref.shape[0]

    @pl.when((i % tiles_per_seq) == 0)
    def _():
        tail_ref[...] = jnp.zeros_like(tail_ref)

    xn = _rms(h_ref[...], g_ref[...]).astype(BF16)
    blk = mq_ref.shape[1]
    groups = list(range(0, ML_INNER, PROJ_CHUNK))
    st = {"gates": jnp.zeros((tm, V7X_LANES), F32) + bg_ref[...]}

    def project(c0):
        cs = slice(c0, c0 + PROJ_CHUNK)
        st["u", c0] = _dot(xn, w_ref[:, cs])
        z_ref[:, cs] = _silu(_dot(xn, w_ref[:, ML_INNER + c0:ML_INNER + c0 + PROJ_CHUNK])).astype(BF16)

    def conv(c0):
        cs = slice(c0, c0 + PROJ_CHUNK)
        u = st.pop(("u", c0))
        prev = tail_ref[:, cs]
        tail_ref[:, cs] = u[tm - V7X_SUBLANES:, :]
        acc = u * cw_ref[ML_CONV - 1:ML_CONV, cs] + cb_ref[:, cs]
        for s in range(1, ML_CONV):
            acc = acc + _prev_rows(u, prev, s) * cw_ref[ML_CONV - 1 - s:ML_CONV - s, cs]
        c16 = _silu(acc).astype(BF16)
        c_ref[:, cs] = c16
        st["c16", c0] = c16
        st["u16", c0] = u.astype(BF16)

    def maps(c0):
        c16 = st.pop(("c16", c0))
        u16 = st.pop(("u16", c0))
        gates = st["gates"]
        for n in range(PROJ_CHUNK // blk):
            ls = slice(n * blk, (n + 1) * blk)
            sl = slice(c0 + n * blk, c0 + (n + 1) * blk)
            tile = sl.start // blk
            qn = _dot(c16[:, ls], mq_ref[tile]).astype(BF16)
            kn = _dot(c16[:, ls], mk_ref[tile]).astype(BF16)
            vn = _dot(u16[:, ls], mv_ref[tile]).astype(BF16)
            q_ref[:, sl] = qn
            k_ref[:, sl] = (kn.astype(F32) * (ML_DH ** -0.5)).astype(BF16)
            v_ref[:, sl] = vn
            gates = (gates + _dot(qn, wgt_ref[0, sl, :]) + _dot(kn, wgt_ref[1, sl, :])
                     + _dot(vn, wgt_ref[2, sl, :]))
        st["gates"] = gates

    project(groups[0])
    for gi, c0 in enumerate(groups):
        if gi + 1 < len(groups):
            project(groups[gi + 1])
        conv(c0)
        maps(c0)
    gates = st["gates"]
    lane = lax.broadcasted_iota(jnp.int32, gates.shape, 1)
    gt_ref[...] = jnp.where(jnp.logical_and(lane >= ML_HEADS, lane < 2 * ML_HEADS),
                            _log_sigmoid(gates), gates)


def _ml_front(h, g, w, conv_w, conv_b, mq, mk, mv, wgt, bg, seq):
    t, d = h.shape
    tm = min(TM_PROJ, seq)
    tps = seq // tm
    act = jax.ShapeDtypeStruct((t, ML_INNER), BF16)
    row_spec = pl.BlockSpec((tm, ML_INNER), lambda i: (i, 0))
    return pl.pallas_call(
        functools.partial(_ml_front_kernel, tiles_per_seq=tps),
        out_shape=(act, act, act, act, act, jax.ShapeDtypeStruct((t, V7X_LANES), F32)),
        grid=(t // tm,),
        in_specs=[
            pl.BlockSpec((tm, d), lambda i: (i, 0)),
            _const_spec((1, d)),
            _const_spec(w.shape),
            _const_spec((ML_CONV, ML_INNER)),
            _const_spec((1, ML_INNER)),
            _const_spec(mq.shape), _const_spec(mk.shape), _const_spec(mv.shape),
            _const_spec(wgt.shape),
            _const_spec((1, V7X_LANES)),
        ],
        out_specs=(row_spec, row_spec, row_spec, row_spec, row_spec,
                   pl.BlockSpec((tm, V7X_LANES), lambda i: (i, 0))),
        scratch_shapes=[pltpu.VMEM((V7X_SUBLANES, ML_INNER), F32)],
        compiler_params=_params(("arbitrary",)),
        name="ml_front",
    )(h, g, w, conv_w, conv_b, mq, mk, mv, wgt, bg)


def _ml_chunk_kernel(q_ref, k_ref, v_ref, gt_ref, c_ref, z_ref, gn_ref, sk_ref, o_ref,
                     cst_ref, nst_ref, mst_ref):
    ci = pl.program_id(1)
    nb, ln, _ = q_ref.shape

    @pl.when(ci == 0)
    def _():
        cst_ref[...] = jnp.zeros_like(cst_ref)
        nst_ref[...] = jnp.zeros_like(nst_ref)
        mst_ref[...] = jnp.zeros_like(mst_ref)

    row = lax.broadcasted_iota(jnp.int32, (ln, ln), 0)
    col = lax.broadcasted_iota(jnp.int32, (ln, ln), 1)
    causal = col <= row
    tril = causal.astype(F32)
    heads = range(ML_HEADS)
    sls = [slice(hd * ML_DH, (hd + 1) * ML_DH) for hd in heads]

    def row_stages(bi):
        st = {}

        def load():
            gates = gt_ref[bi]
            cum = _dot_exact(tril, gates)
            st["gates"], st["cum"], st["gates_t"], st["cum_t"] = gates, cum, gates.T, cum.T
            st["q"] = [q_ref[bi, :, sl] for sl in sls]
            st["k"] = [k_ref[bi, :, sl] for sl in sls]
            st["v"] = [v_ref[bi, :, sl] for sl in sls]
            st["cst"] = [cst_ref[bi, hd] for hd in heads]
            st["n_row"] = [nst_ref[bi, hd:hd + 1, :] for hd in heads]
            st["m_prev"] = [mst_ref[bi, hd:hd + 1, 0:1] for hd in heads]
            st["b_col"] = [cum[:, ML_HEADS + hd:ML_HEADS + hd + 1] for hd in heads]

        def scores():
            st["qk"] = [_dot_nt(st["q"][hd], st["k"][hd]) for hd in heads]
            st["inter"] = [_dot(st["q"][hd], st["cst"][hd].astype(BF16)) for hd in heads]

        def decay():
            b_col, m_prev = st["b_col"], st["m_prev"]
            log_d = [jnp.where(causal, b_col[hd] - st["cum_t"][ML_HEADS + hd:ML_HEADS + hd + 1, :]
                               + st["gates_t"][hd:hd + 1, :], -jnp.inf) for hd in heads]
            log_inter = [b_col[hd] + m_prev[hd] for hd in heads]
            st["m_row"] = [jnp.maximum(log_inter[hd], jnp.max(log_d[hd], axis=-1, keepdims=True)) for hd in heads]
            st["w_inter"] = [jnp.exp(log_inter[hd] - st["m_row"][hd]) for hd in heads]
            st["s"] = [st["qk"][hd] * jnp.exp(log_d[hd] - st["m_row"][hd]) for hd in heads]

        def readout():
            s, q = st["s"], st["q"]
            num = [_dot(s[hd].astype(BF16), st["v"][hd]) + st["w_inter"][hd] * st["inter"][hd] for hd in heads]
            den = [jnp.sum(s[hd], axis=-1, keepdims=True) + st["w_inter"][hd] * jnp.sum(
                q[hd].astype(F32) * st["n_row"][hd], axis=-1, keepdims=True) for hd in heads]
            st["hh"] = [num[hd] / jnp.maximum(jnp.abs(den[hd]), jnp.exp(-st["m_row"][hd])) for hd in heads]

        def state():
            b_col, m_prev = st["b_col"], st["m_prev"]
            for hd in heads:
                b_last = b_col[hd][ln - 1:ln, :]
                log_w = b_last - b_col[hd] + st["gates"][:, hd:hd + 1]
                m_new = jnp.maximum(b_last + m_prev[hd], jnp.max(log_w, axis=0, keepdims=True))
                kw = st["k"][hd].astype(F32) * jnp.exp(log_w - m_new)
                carry = jnp.exp(b_last + m_prev[hd] - m_new)
                cst_ref[bi, hd] = carry * st["cst"][hd] + _dot_tn(kw.astype(BF16), st["v"][hd])
                nst_ref[bi, hd:hd + 1, :] = carry * st["n_row"][hd] + jnp.sum(kw, axis=0, keepdims=True)
                mst_ref[bi, hd:hd + 1, :] = jnp.broadcast_to(m_new, (1, V7X_LANES))

        def epilogue(hd):
            sl = sls[hd]
            hh = st["hh"][hd]
            hc = hh - jnp.mean(hh, axis=-1, keepdims=True)
            y = hc * lax.rsqrt(jnp.mean(hc * hc, axis=-1, keepdims=True) + ML_EPS) * gn_ref[:, sl]
            y = (y + sk_ref[:, sl] * c_ref[bi, :, sl].astype(F32)) * z_ref[bi, :, sl].astype(F32)
            o_ref[bi, :, sl] = y.astype(BF16)

        return [load, scores, decay, readout, state], [functools.partial(epilogue, hd) for hd in heads]

    pending = []
    for bi in range(nb):
        stages, epilogues = row_stages(bi)
        for i, stage in enumerate(stages):
            stage()
            if i < len(pending):
                pending[i]()
        for piece in pending[len(stages):]:
            piece()
        pending = epilogues
    for piece in pending:
        piece()


def _ml_chunk(q, k, v, gates, c, z, gn, skip, batch, seq):
    t = q.shape[0]
    ln = min(ML_CHUNK, seq)
    nc = seq // ln
    nb = math.gcd(batch, ML_BATCH_PER_STEP)
    row_spec = pl.BlockSpec((nb, ln, ML_INNER), lambda b, ci: (b, ci, 0))
    vec = pl.BlockSpec((1, ML_INNER), lambda b, ci: (0, 0))
    acts = [x.reshape(batch, seq, ML_INNER) for x in (q, k, v)]
    out = pl.pallas_call(
        _ml_chunk_kernel,
        out_shape=jax.ShapeDtypeStruct((batch, seq, ML_INNER), BF16),
        grid=(batch // nb, nc),
        in_specs=[
            row_spec, row_spec, row_spec,
            pl.BlockSpec((nb, ln, V7X_LANES), lambda b, ci: (b, ci, 0)),
            row_spec,
            row_spec,
            vec, vec,
        ],
        out_specs=row_spec,
        scratch_shapes=[
            pltpu.VMEM((nb, ML_HEADS, ML_DH, ML_DH), F32),
            pltpu.VMEM((nb, V7X_SUBLANES, ML_DH), F32),
            pltpu.VMEM((nb, V7X_SUBLANES, V7X_LANES), F32),
        ],
        compiler_params=_params(("parallel", "arbitrary")),
        name="ml_chunk",
    )(*acts, gates.reshape(batch, seq, V7X_LANES), c.reshape(batch, seq, ML_INNER),
      z.reshape(batch, seq, ML_INNER), gn, skip)
    return out.reshape(t, ML_INNER)


def _rw_pre_kernel(h_ref, hp_ref, g_ref, mu_ref, wrkv_ref, la_ref, lbw_ref, lba_ref, lbg_ref,
                   w0_ref, a0_ref, kk_ref, ka_ref,
                   r_out, lw_out, k_out, v_out, kk_out, kb_out, g_out, *, tiles_per_seq,
                   lora_w, lora_a):
    i = pl.program_id(0)
    tm = h_ref.shape[0]
    g = g_ref[...]
    xn = _rms(h_ref[...], g)
    first = (i % tiles_per_seq) == 0
    xp8 = jnp.where(first, 0.0, _rms(hp_ref[...], g))
    dx = _prev_rows(xn, xp8, 1) - xn

    def matmuls(rows):
        def mix(n):
            return (xn[rows] + dx[rows] * mu_ref[n:n + 1, :]).astype(BF16)

        out = {"r": _dot(mix(0), wrkv_ref[0]), "k": _dot(mix(1), wrkv_ref[1]), "v": _dot(mix(2), wrkv_ref[2])}
        hw = jnp.tanh(_dot(mix(3), la_ref[:, 0:lora_w])).astype(BF16)
        ha = _dot(mix(4), la_ref[:, lora_w:lora_w + lora_a]).astype(BF16)
        hg = jax.nn.sigmoid(_dot(mix(5), la_ref[:, lora_w + lora_a:])).astype(BF16)
        out["w_pre"] = _dot(hw, lbw_ref[...])
        out["a_pre"] = _dot(ha, lba_ref[...])
        g_out[rows, :] = _dot(hg, lbg_ref[...]).astype(BF16)
        return out

    def tail(rows, m):
        lw_out[rows, :] = -math.exp(-0.5) * jax.nn.sigmoid(w0_ref[...] + m["w_pre"])
        alpha = jax.nn.sigmoid(a0_ref[...] + m["a_pre"])
        kk = m["k"] * kk_ref[...]
        r_out[rows, :] = m["r"].astype(BF16)
        k_out[rows, :] = (m["k"] * (1.0 + (alpha - 1.0) * ka_ref[...])).astype(BF16)
        v_out[rows, :] = m["v"].astype(BF16)
        kk_out[rows, :] = kk.astype(BF16)
        kb_out[rows, :] = (kk * alpha).astype(BF16)

    halves = [slice(0, tm // 2), slice(tm // 2, tm)]
    first_half = matmuls(halves[0])
    second_half = matmuls(halves[1])
    tail(halves[0], first_half)
    tail(halves[1], second_half)


def _rw_pre(h, g, mu, wrkv, la, lbw, lba, lbg, w0, a0, k_k, k_a, seq, lora_w, lora_a):
    t, d = h.shape
    tm = min(TM_PRE, seq)
    tps = seq // tm
    rb = tm // V7X_SUBLANES
    row_spec = pl.BlockSpec((tm, d), lambda i: (i, 0))
    act = jax.ShapeDtypeStruct((t, d), BF16)
    vec = _const_spec((1, d))
    return pl.pallas_call(
        functools.partial(_rw_pre_kernel, tiles_per_seq=tps, lora_w=lora_w, lora_a=lora_a),
        out_shape=(act, jax.ShapeDtypeStruct((t, d), F32), act, act, act, act, act),
        grid=(t // tm,),
        in_specs=[
            row_spec,
            pl.BlockSpec((V7X_SUBLANES, d), lambda i: (jnp.maximum(i * rb - 1, 0), 0)),
            vec,
            _const_spec(mu.shape),
            _const_spec(wrkv.shape),
            _const_spec(la.shape),
            _const_spec(lbw.shape), _const_spec(lba.shape), _const_spec(lbg.shape),
            vec, vec, vec, vec,
        ],
        out_specs=(row_spec,) * 7,
        compiler_params=_params(("parallel",)),
        name="rw_pre",
    )(h, h, g, mu, wrkv, la, lbw, lba, lbg, w0, a0, k_k, k_a)


def _rw_chunk_kernel(r_ref, lw_ref, k_ref, v_ref, kk_ref, kb_ref, g_ref, rk_ref, gng_ref, gnb_ref,
                     o_ref, st_ref):
    ci = pl.program_id(1)
    nb, ln, _ = r_ref.shape
    pair = 2 * RW_DH

    @pl.when(ci == 0)
    def _():
        st_ref[...] = jnp.zeros_like(st_ref)

    trow = lax.broadcasted_iota(jnp.int32, (ln, ln), 0)
    tcol = lax.broadcasted_iota(jnp.int32, (ln, ln), 1)
    tril = (tcol <= trow).astype(F32)
    lane = lax.broadcasted_iota(jnp.int32, (ln, pair), 1)
    head0 = lane < RW_DH
    srow = lax.broadcasted_iota(jnp.int32, (2 * ln, 2 * ln), 0)
    scol = lax.broadcasted_iota(jnp.int32, (2 * ln, 2 * ln), 1)
    strict_t = srow < scol
    incl_t = srow <= scol
    eye = (srow == scol).astype(F32)
    same_blk = (srow // RW_BLOCK) == (scol // RW_BLOCK)

    def stack(x):
        return jnp.concatenate([jnp.where(head0, x, 0.0), jnp.where(head0, 0.0, x)], axis=0).astype(BF16)

    pair_slices = [slice(p * pair, (p + 1) * pair) for p in range(RW_HEADS // 2)]

    def head_sums(x):
        s0 = jnp.sum(jnp.where(head0, x, 0.0), axis=-1, keepdims=True)
        s1 = jnp.sum(jnp.where(head0, 0.0, x), axis=-1, keepdims=True)
        return s0, s1

    def make_group(rows):
        chains = [(bi, p) for bi in rows for p in range(RW_HEADS // 2)]
        ids = range(len(chains))
        gs = {k_: [] for k_ in ("ar_s", "b_s", "k_s", "bk", "v_t", "g_last", "bonus")}

        def prologue(bi):
            lw = lw_ref[bi]
            cum = _dot_exact(tril, lw)
            cum_last = cum[ln - 1:ln, :]
            g_inv = jnp.exp(-cum)
            g_rem = jnp.exp(cum_last - cum)
            kk_raw = kk_ref[bi].astype(F32)
            r_all = r_ref[bi].astype(F32)
            kk = k_ref[bi].astype(F32)
            v_all = v_ref[bi].astype(F32)
            rkk = r_all * kk * rk_ref[...]
            inv_parts = []
            for sl in pair_slices:
                s0, s1 = head_sums(kk_raw[:, sl] * kk_raw[:, sl])
                inv_parts.append(jnp.where(head0, lax.rsqrt(jnp.maximum(s0, 1e-24)),
                                           lax.rsqrt(jnp.maximum(s1, 1e-24))))
                t0, t1 = head_sums(rkk[:, sl])
                gs["bonus"].append(jnp.where(head0, t0, t1) * v_all[:, sl])
            inv_all = jnp.concatenate(inv_parts, axis=1)
            kn = kk_raw * inv_all
            bb = kb_ref[bi].astype(F32) * inv_all
            a_all = -kn * jnp.exp(cum - lw)
            rt_all = r_all * jnp.exp(cum)
            bt_all = bb * g_inv
            kt_all = kk * g_inv
            bh_all = bb * g_rem
            kh_all = kk * g_rem
            gl = jnp.exp(cum_last)
            for sl in pair_slices:
                gs["ar_s"].append(jnp.concatenate([stack(a_all[:, sl]), stack(rt_all[:, sl])], axis=0))
                gs["b_s"].append(stack(bt_all[:, sl]))
                gs["k_s"].append(stack(kt_all[:, sl]))
                gs["bk"].append(jnp.concatenate([stack(bh_all[:, sl]), stack(kh_all[:, sl])], axis=0))
                vt = v_all[:, sl].T
                gs["v_t"].append(jnp.concatenate([vt[:RW_DH], vt[RW_DH:]], axis=1).astype(BF16))
                gs["g_last"].append(gl[:, sl])

        def products():
            gs["st"] = [st_ref[bi, p] for bi, p in chains]
            gs["m"] = [_dot_nt(jnp.concatenate([gs["b_s"][c], gs["k_s"][c], gs["st"][c].astype(BF16)], axis=0),
                               gs["ar_s"][c]) for c in ids]

        def masks():
            m = gs["m"]
            nt = [jnp.where(strict_t, m[c][:2 * ln, :2 * ln], 0.0) for c in ids]
            gs["pw"] = [jnp.where(same_blk, nt[c], 0.0) for c in ids]
            gs["noff"] = [jnp.where(same_blk, 0.0, nt[c]).astype(BF16) for c in ids]
            ak = [jnp.where(strict_t, m[c][2 * ln:4 * ln, :2 * ln], 0.0).astype(BF16) for c in ids]
            gs["rbk"] = [jnp.concatenate([jnp.where(incl_t, m[c][:2 * ln, 2 * ln:], 0.0).astype(BF16),
                                          jnp.where(incl_t, m[c][2 * ln:4 * ln, 2 * ln:], 0.0).astype(BF16)],
                                         axis=0) for c in ids]
            gs["inter_r"] = [m[c][4 * ln:, 2 * ln:] for c in ids]
            gs["u"] = [m[c][4 * ln:, :2 * ln] + _dot(gs["v_t"][c], ak[c]) for c in ids]

        def diag_first():
            p16 = [gs["pw"][c].astype(BF16) for c in ids]
            gs["e"] = [gs["pw"][c] + eye for c in ids]
            gs["pw"] = [_dot(p16[c], p16[c]) for c in ids]

        def diag_mid():
            q16 = [gs["pw"][c].astype(BF16) for c in ids]
            both = [_dot(jnp.concatenate([gs["e"][c].astype(BF16), q16[c]], axis=0), q16[c]) for c in ids]
            gs["e"] = [gs["e"][c] + both[c][:2 * ln] for c in ids]
            gs["pw"] = [both[c][2 * ln:] for c in ids]

        def diag_last():
            gs["e"] = [gs["e"][c] + _dot(gs["e"][c].astype(BF16), gs["pw"][c].astype(BF16)) for c in ids]

        def off_first():
            d16 = [gs["e"][c].astype(BF16) for c in ids]
            both = [_dot(jnp.concatenate([gs["noff"][c], gs["u"][c].astype(BF16)], axis=0), d16[c]) for c in ids]
            gs["pw"] = [both[c][:2 * ln] for c in ids]
            gs["u"] = [both[c][2 * ln:] for c in ids]

        def off_mid():
            m16 = [gs["pw"][c].astype(BF16) for c in ids]
            both = [_dot(jnp.concatenate([gs["u"][c].astype(BF16), m16[c]], axis=0), m16[c]) for c in ids]
            gs["u"] = [gs["u"][c] + both[c][:RW_DH] for c in ids]
            gs["pw"] = [both[c][RW_DH:] for c in ids]

        def off_last():
            gs["u"] = [gs["u"][c] + _dot(gs["u"][c].astype(BF16), gs["pw"][c].astype(BF16)) for c in ids]

        diag_levels = int(math.log2(RW_BLOCK))
        off_levels = max(int(math.ceil(math.log2(ln // RW_BLOCK))), 1)
        solve = ([diag_first] + [diag_mid] * (diag_levels - 2) + [diag_last]
                 + [off_first] + [off_mid] * (off_levels - 1) + [off_last])

        def readout():
            uv = [jnp.concatenate([gs["u"][c].astype(BF16), gs["v_t"][c]], axis=1) for c in ids]
            gs["y_t"] = [gs["inter_r"][c] + _dot(uv[c], gs["rbk"][c]) for c in ids]
            for c, (bi, p) in enumerate(chains):
                st_ref[bi, p] = gs["st"][c] * gs["g_last"][c] + _dot(uv[c], gs["bk"][c])

        def epilogue(c):
            bi, p = chains[c]
            sl = pair_slices[p]
            y_t = gs["y_t"][c]
            yc = y_t - jnp.mean(y_t, axis=0, keepdims=True)
            yn = (yc * lax.rsqrt(jnp.mean(yc * yc, axis=0, keepdims=True) + RW_GN_EPS)).T
            y_p = jnp.concatenate([yn[:ln], yn[ln:]], axis=1)
            out = y_p * gng_ref[:, sl] + gnb_ref[:, sl] + gs["bonus"][c]
            o_ref[bi, :, sl] = (out * g_ref[bi, :, sl].astype(F32)).astype(BF16)

        stages = [products, masks] + solve + [readout]
        return ([functools.partial(prologue, bi) for bi in rows], stages,
                [functools.partial(epilogue, c) for c in ids])

    def interleave(stages, fillers):
        per = -(-len(fillers) // max(len(stages), 1)) if fillers else 0
        fillers = list(fillers)
        for stage in stages:
            stage()
            for piece in fillers[:per]:
                piece()
            fillers = fillers[per:]
        for piece in fillers:
            piece()

    half = max(nb // 2, 1)
    groups = [make_group(range(g0, min(g0 + half, nb))) for g0 in range(0, nb, half)]
    for piece in groups[0][0]:
        piece()
    pending = []
    for gi, (_, stages, epilogues) in enumerate(groups):
        nxt = groups[gi + 1][0] if gi + 1 < len(groups) else []
        interleave(stages, list(pending) + list(nxt))
        pending = epilogues
    for piece in pending:
        piece()


def _rw_chunk(r, lw, k, v, kk, kb, g, r_k, gn_g, gn_b, batch, seq):
    t, d = r.shape
    ln = min(RW_CHUNK, seq)
    nc = seq // ln
    nb = math.gcd(batch, RW_BATCH_PER_STEP)
    row_spec = pl.BlockSpec((nb, ln, d), lambda bi, ci: (bi, ci, 0))
    vec = pl.BlockSpec((1, d), lambda bi, ci: (0, 0))
    acts = [x.reshape(batch, seq, d) for x in (r, lw, k, v, kk, kb, g)]
    out = pl.pallas_call(
        _rw_chunk_kernel,
        out_shape=jax.ShapeDtypeStruct((batch, seq, d), BF16),
        grid=(batch // nb, nc),
        in_specs=[row_spec] * 7 + [vec, vec, vec],
        out_specs=row_spec,
        scratch_shapes=[pltpu.VMEM((nb, RW_HEADS // 2, RW_DH, 2 * RW_DH), F32)],
        compiler_params=_params(("parallel", "arbitrary")),
        name="rw_chunk",
    )(*acts, r_k, gn_g, gn_b)
    return out.reshape(t, d)


def _rope_tables(seq):
    half = RET_DK // 2
    pos = jnp.arange(seq, dtype=F32)
    inv_freq = 1.0 / (ROPE_BASE ** jnp.linspace(0.0, 1.0, half, dtype=F32))
    ang = pos[:, None] * inv_freq[None, :]
    local = (jnp.arange(seq) % min(RET_CHUNK, seq)).astype(F32) + 1.0
    log_gamma = jnp.asarray([_ret_log_gamma(hd) for hd in range(RET_HEADS)], F32)
    q_scale = jnp.exp(log_gamma[:, None] * local[None, :])
    k_scale = jnp.exp(-log_gamma[:, None] * local[None, :]) * RET_DK ** -0.5
    scale = jnp.concatenate([q_scale, k_scale], axis=0)[:, :, None]
    return jnp.cos(ang)[None] * scale, jnp.sin(ang)[None] * scale


def _retention_mixer(h, norm_g, w_in16, layer, gn, batch, seq):
    cos, sin = _rope_tables(seq)
    proj = _ret_proj(h, norm_g[None, :], w_in16, layer, cos, sin, seq)
    return _ret_chunk(proj, gn[None, :].astype(F32), batch, seq)


def _block_diag_tiles(w, tile):
    nb, bs, _ = w.shape
    rows = w.reshape(nb * bs // tile, tile, bs)
    dense = jnp.tile(rows, (1, 1, tile // bs))
    idx = jnp.arange(tile) // bs
    return jnp.where(idx[:, None] == idx[None, :], dense, 0.0)


def _mlstm_mixer(h, norm_g, w_in, conv_w, conv_b, wq, wk, wv, w_gate, b_gate, gn, skip, batch, seq):
    tile = 2 * V7X_LANES
    mq = _block_diag_tiles(wq, tile).astype(BF16)
    mk = _block_diag_tiles(wk, tile).astype(BF16)
    mv = _block_diag_tiles(wv, tile).astype(BF16)
    n_gate = w_gate.shape[1]
    wgt = jnp.pad(w_gate.reshape(3, ML_INNER, n_gate), ((0, 0), (0, 0), (0, V7X_LANES - n_gate))).astype(BF16)
    bg = jnp.pad(b_gate, (0, V7X_LANES - n_gate))[None, :].astype(F32)
    c, q, k, v, z, gates = _ml_front(h, norm_g[None, :], w_in.astype(BF16), conv_w, conv_b[None, :],
                                     mq, mk, mv, wgt, bg, seq)
    return _ml_chunk(q, k, v, gates, c, z, gn[None, :], skip[None, :], batch, seq)


def _pad_to(x, axis, size):
    pad = [(0, 0)] * x.ndim
    pad[axis] = (0, size - x.shape[axis])
    return jnp.pad(x, pad)


def _rwkv_mixer(h, norm_g, mu, w_rkv, w0, w_la, w_lb, a0, a_la, a_lb, g_la, g_lb, k_k, k_a, r_k,
                gn_g, gn_b, batch, seq):
    d = h.shape[1]
    lw_ = -(-w_la.shape[1] // V7X_LANES) * V7X_LANES
    la_ = -(-a_la.shape[1] // V7X_LANES) * V7X_LANES
    lg_ = -(-g_la.shape[1] // V7X_LANES) * V7X_LANES
    la = jnp.concatenate([_pad_to(w_la, 1, lw_), _pad_to(a_la, 1, la_), _pad_to(g_la, 1, lg_)],
                         axis=1).astype(BF16)
    lbw = _pad_to(w_lb, 0, lw_).astype(BF16)
    lba = _pad_to(a_lb, 0, la_).astype(BF16)
    lbg = _pad_to(g_lb, 0, lg_).astype(BF16)
    r, lw, k, v, kk, kb, g = _rw_pre(h, norm_g[None, :], mu, w_rkv.astype(BF16), la, lbw, lba, lbg,
                                     w0[None, :], a0[None, :], k_k[None, :], k_a[None, :], seq, lw_, la_)
    return _rw_chunk(r, lw, k, v, kk, kb, g, r_k.reshape(1, d), gn_g[None, :], gn_b[None, :], batch, seq)


def kernel(x, norm_mix, norm_ffn, norm_final, ret_w_in, ret_gn, ret_w_out, ml_w_in, ml_conv_w, ml_conv_b, ml_wq, ml_wk, ml_wv, ml_w_gate, ml_b_gate, ml_gn, ml_skip, ml_w_out, rw_mu, rw_w_rkv, rw_w0, rw_w_lora_a, rw_w_lora_b, rw_a0, rw_a_lora_a, rw_a_lora_b, rw_g_lora_a, rw_g_lora_b, rw_k_k, rw_k_a, rw_r_k, rw_gn_g, rw_gn_b, rw_w_out, ffn_w_gu, ffn_w_down):
    batch, seq, d = x.shape
    depth = norm_mix.shape[0]
    h = x.reshape(batch * seq, d)
    ret_w_in16 = ret_w_in.astype(BF16)
    w_out16 = (ret_w_out.astype(BF16), ml_w_out.astype(BF16), rw_w_out.astype(BF16))
    w_gu16 = ffn_w_gu.astype(BF16)
    w_down16 = ffn_w_down.astype(BF16)
    for i in range(depth):
        kind, j = i % 3, i // 3
        if kind == 0:
            y = _retention_mixer(h, norm_mix[i], ret_w_in16, j, ret_gn[j], batch, seq)
        elif kind == 1:
            y = _mlstm_mixer(h, norm_mix[i], ml_w_in[j], ml_conv_w[j], ml_conv_b[j], ml_wq[j], ml_wk[j],
                             ml_wv[j], ml_w_gate[j], ml_b_gate[j], ml_gn[j], ml_skip[j], batch, seq)
        else:
            y = _rwkv_mixer(h, norm_mix[i], rw_mu[j], rw_w_rkv[j], rw_w0[j], rw_w_lora_a[j],
                            rw_w_lora_b[j], rw_a0[j], rw_a_lora_a[j], rw_a_lora_b[j], rw_g_lora_a[j],
                            rw_g_lora_b[j], rw_k_k[j], rw_k_a[j], rw_r_k[j], rw_gn_g[j], rw_gn_b[j],
                            batch, seq)
        h = _mix_ffn(y, w_out16[kind], j, h, norm_ffn[i][None, :], w_gu16, w_down16, i,
                     norm_final[None, :], i == depth - 1)
    return h.reshape(batch, seq, d)
```

```python
import functools
import math

import jax
import jax.numpy as jnp
from jax import lax
from jax.experimental import pallas as pl
from jax.experimental.pallas import tpu as pltpu

F32 = jnp.float32
BF16 = jnp.bfloat16

D_FF = 2816
RMS_EPS = 1e-6

RET_HEADS = 4
RET_DK = 256
RET_DV = 512
RET_EPS = 1e-6
ROPE_BASE = 10000.0

ML_INNER = 2048
ML_HEADS = 4
ML_DH = 512
ML_CONV = 4
ML_EPS = 1e-6

RW_HEADS = 16
RW_DH = 64
RW_GN_EPS = 64e-5

V7X_LANES = 128
V7X_SUBLANES = 8
V7X_VMEM_LIMIT = 56 * 1024 * 1024

TM_PROJ = 512
TM_RET_PROJ = 512
PROJ_CHUNK = 512
TM_FFN = 1024
FFN_CHUNK = 256
TM_PRE = 512
RET_CHUNK = 256
ML_CHUNK = 256
ML_BATCH_PER_STEP = 2
RW_CHUNK = 64
RW_BLOCK = 16
RW_BATCH_PER_STEP = 4


def _params(sem, vmem=V7X_VMEM_LIMIT):
    return pltpu.CompilerParams(dimension_semantics=sem, vmem_limit_bytes=vmem)


def _dot(a, b):
    return jnp.dot(a, b, preferred_element_type=F32)


def _dot_nt(a, b):
    return lax.dot_general(a, b, (((1,), (1,)), ((), ())), preferred_element_type=F32)


def _dot_tn(a, b):
    return lax.dot_general(a, b, (((0,), (0,)), ((), ())), preferred_element_type=F32)


def _dot_exact(a, b):
    hi = b.astype(BF16)
    lo = (b - hi.astype(F32)).astype(BF16)
    a16 = a.astype(BF16)
    return _dot(a16, hi) + _dot(a16, lo)


def _rms(x, g):
    ms = jnp.mean(x * x, axis=-1, keepdims=True)
    return x * lax.rsqrt(ms + RMS_EPS) * g


def _silu(x):
    return x * jax.nn.sigmoid(x)


def _const_spec(shape):
    nd = len(shape)
    return pl.BlockSpec(shape, lambda *_: (0,) * nd, pipeline_mode=pl.Buffered(1))


def _layer_spec(shape, layer):
    nd = len(shape)
    return pl.BlockSpec((None,) + tuple(shape), lambda *_: (layer,) + (0,) * nd,
                        pipeline_mode=pl.Buffered(1))


def _ret_proj_kernel(h_ref, g_ref, w_ref, cos_ref, sin_ref, o_ref):
    xn = _rms(h_ref[...], g_ref[...]).astype(BF16)
    half = RET_DK // 2
    hk = RET_HEADS * RET_DK
    hv = RET_HEADS * RET_DV
    for hd in range(2 * RET_HEADS):
        c0 = hd * RET_DK
        acc = _dot(xn, w_ref[:, c0:c0 + RET_DK])
        cos = cos_ref[hd]
        sin = sin_ref[hd]
        t1 = acc[:, :half]
        t2 = acc[:, half:]
        o_ref[:, c0:c0 + half] = (t1 * cos - t2 * sin).astype(BF16)
        o_ref[:, c0 + half:c0 + RET_DK] = (t2 * cos + t1 * sin).astype(BF16)
    for c0 in range(2 * hk, 2 * hk + hv, PROJ_CHUNK):
        o_ref[:, c0:c0 + PROJ_CHUNK] = _dot(xn, w_ref[:, c0:c0 + PROJ_CHUNK]).astype(BF16)
    for c0 in range(2 * hk + hv, 2 * hk + 2 * hv, PROJ_CHUNK):
        o_ref[:, c0:c0 + PROJ_CHUNK] = _silu(_dot(xn, w_ref[:, c0:c0 + PROJ_CHUNK])).astype(BF16)


def _ret_proj(h, g, w, layer, cos, sin, seq):
    t, d = h.shape
    n = w.shape[2]
    tm = min(TM_RET_PROJ, seq)
    spt = seq // tm
    return pl.pallas_call(
        _ret_proj_kernel,
        out_shape=jax.ShapeDtypeStruct((t, n), BF16),
        grid=(t // tm,),
        in_specs=[
            pl.BlockSpec((tm, d), lambda i: (i, 0)),
            _const_spec((1, d)),
            _layer_spec((d, n), layer),
            pl.BlockSpec((2 * RET_HEADS, tm, RET_DK // 2), lambda i: (0, i % spt, 0)),
            pl.BlockSpec((2 * RET_HEADS, tm, RET_DK // 2), lambda i: (0, i % spt, 0)),
        ],
        out_specs=pl.BlockSpec((tm, n), lambda i: (i, 0)),
        compiler_params=_params(("parallel",)),
        name="ret_proj",
    )(h, g, w, cos, sin)


def _mix_ffn_kernel(y_ref, wo_ref, h_ref, g_ref, wgu_ref, wd_ref, gf_ref, o_ref, *, final_norm):
    h = h_ref[...] + _dot(y_ref[...], wo_ref[...])
    xn = _rms(h, g_ref[...]).astype(BF16)
    acc = h
    for f in range(0, D_FF, FFN_CHUNK):
        gate = _dot(xn, wgu_ref[:, f:f + FFN_CHUNK])
        up = _dot(xn, wgu_ref[:, D_FF + f:D_FF + f + FFN_CHUNK])
        act = (_silu(gate) * up).astype(BF16)
        acc = acc + _dot(act, wd_ref[f:f + FFN_CHUNK, :])
    if final_norm:
        acc = _rms(acc, gf_ref[...])
    o_ref[...] = acc


def _mix_ffn(y, wo, wo_layer, h, g, wgu, wd, ffn_layer, gf, final_norm):
    t, d = h.shape
    k = y.shape[1]
    tm = min(TM_FFN, t)
    return pl.pallas_call(
        functools.partial(_mix_ffn_kernel, final_norm=final_norm),
        out_shape=jax.ShapeDtypeStruct((t, d), F32),
        grid=(t // tm,),
        in_specs=[
            pl.BlockSpec((tm, k), lambda i: (i, 0)),
            _layer_spec((k, d), wo_layer),
            pl.BlockSpec((tm, d), lambda i: (i, 0)),
            _const_spec((1, d)),
            _layer_spec((d, 2 * D_FF), ffn_layer),
            _layer_spec((D_FF, d), ffn_layer),
            _const_spec((1, d)),
        ],
        out_specs=pl.BlockSpec((tm, d), lambda i: (i, 0)),
        compiler_params=_params(("parallel",)),
        name="mix_ffn",
    )(y, wo, h, g, wgu, wd, gf)


def _ret_log_gamma(head):
    return math.log1p(-2.0 ** (-5.0 - head))


def _ret_chunk_kernel(q_ref, k_ref, v_ref, g_ref, gn_ref, o_ref, st_ref):
    c = pl.program_id(1)
    ln = q_ref.shape[0]

    @pl.when(c == 0)
    def _():
        st_ref[...] = jnp.zeros_like(st_ref)

    causal = (lax.broadcasted_iota(jnp.int32, (ln, ln), 1) <= lax.broadcasted_iota(jnp.int32, (ln, ln), 0))
    heads = range(RET_HEADS)
    q = [q_ref[:, hd * RET_DK:(hd + 1) * RET_DK] for hd in heads]
    k = [k_ref[:, hd * RET_DK:(hd + 1) * RET_DK] for hd in heads]
    v = [v_ref[:, hd * RET_DV:(hd + 1) * RET_DV] for hd in heads]
    st = [st_ref[hd] for hd in heads]
    qk = [_dot_nt(q[hd], k[hd]) for hd in heads]
    inter = [_dot(q[hd], st[hd].astype(BF16)) for hd in heads]
    scores = [jnp.where(causal, qk[hd], 0.0).astype(BF16) for hd in heads]
    out = [_dot(scores[hd], v[hd]) + inter[hd] for hd in heads]
    for hd in heads:
        st_ref[hd] = (st[hd] + _dot_tn(k[hd], v[hd])) * math.exp(_ret_log_gamma(hd) * ln)
    for hd in heads:
        sl = slice(hd * RET_DV, (hd + 1) * RET_DV)
        ms = jnp.mean(out[hd] * out[hd], axis=-1, keepdims=True)
        y = out[hd] * lax.rsqrt(ms + RET_EPS) * gn_ref[:, sl]
        o_ref[:, sl] = (y * g_ref[:, sl].astype(F32)).astype(BF16)


def _ret_chunk(proj, gn, batch, seq):
    t = proj.shape[0]
    ln = min(RET_CHUNK, seq)
    nc = seq // ln
    hk = RET_HEADS * RET_DK
    hv = RET_HEADS * RET_DV
    return pl.pallas_call(
        _ret_chunk_kernel,
        out_shape=jax.ShapeDtypeStruct((t, hv), BF16),
        grid=(batch, nc),
        in_specs=[
            pl.BlockSpec((ln, hk), lambda b, c: (b * nc + c, 0)),
            pl.BlockSpec((ln, hk), lambda b, c: (b * nc + c, 1)),
            pl.BlockSpec((ln, hv), lambda b, c: (b * nc + c, 1)),
            pl.BlockSpec((ln, hv), lambda b, c: (b * nc + c, 2)),
            _const_spec((1, hv)),
        ],
        out_specs=pl.BlockSpec((ln, hv), lambda b, c: (b * nc + c, 0)),
        scratch_shapes=[pltpu.VMEM((RET_HEADS, RET_DK, RET_DV), F32)],
        compiler_params=_params(("parallel", "arbitrary")),
        name="ret_chunk",
    )(proj, proj, proj, proj, gn)


def _prev_rows(cur, prev8, shift):
    ext = jnp.concatenate([prev8, cur], axis=0)
    return pltpu.roll(ext, shift, axis=0)[V7X_SUBLANES:, :]


def _log_sigmoid(x):
    return jnp.minimum(x, 0.0) - jnp.log1p(jnp.exp(-jnp.abs(x)))


def _ml_front_kernel(h_ref, g_ref, w_ref, cw_ref, cb_ref, mq_ref, mk_ref, mv_ref, wgt_ref, bg_ref,
                     c_ref, q_ref, k_ref, v_ref, z_ref, gt_ref, tail_ref, *, tiles_per_seq):
    i = pl.program_id(0)
    tm = h_ref.shape[0]

    @pl.when((i % tiles_per_seq) == 0)
    def _():
        tail_ref[...] = jnp.zeros_like(tail_ref)

    xn = _rms(h_ref[...], g_ref[...]).astype(BF16)
    blk = mq_ref.shape[1]
    groups = list(range(0, ML_INNER, PROJ_CHUNK))
    st = {"gates": jnp.zeros((tm, V7X_LANES), F32) + bg_ref[...]}

    def project(c0):
        cs = slice(c0, c0 + PROJ_CHUNK)
        st["u", c0] = _dot(xn, w_ref[:, cs])
        z_ref[:, cs] = _silu(_dot(xn, w_ref[:, ML_INNER + c0:ML_INNER + c0 + PROJ_CHUNK])).astype(BF16)

    def conv(c0):
        cs = slice(c0, c0 + PROJ_CHUNK)
        u = st.pop(("u", c0))
        prev = tail_ref[:, cs]
        tail_ref[:, cs] = u[tm - V7X_SUBLANES:, :]
        acc = u * cw_ref[ML_CONV - 1:ML_CONV, cs] + cb_ref[:, cs]
        for s in range(1, ML_CONV):
            acc = acc + _prev_rows(u, prev, s) * cw_ref[ML_CONV - 1 - s:ML_CONV - s, cs]
        c16 = _silu(acc).astype(BF16)
        c_ref[:, cs] = c16
        st["c16", c0] = c16
        st["u16", c0] = u.astype(BF16)

    def maps(c0):
        c16 = st.pop(("c16", c0))
        u16 = st.pop(("u16", c0))
        gates = st["gates"]
        for n in range(PROJ_CHUNK // blk):
            ls = slice(n * blk, (n + 1) * blk)
            sl = slice(c0 + n * blk, c0 + (n + 1) * blk)
            tile = sl.start // blk
            qn = _dot(c16[:, ls], mq_ref[tile]).astype(BF16)
            kn = _dot(c16[:, ls], mk_ref[tile]).astype(BF16)
            vn = _dot(u16[:, ls], mv_ref[tile]).astype(BF16)
            q_ref[:, sl] = qn
            k_ref[:, sl] = (kn.astype(F32) * (ML_DH ** -0.5)).astype(BF16)
            v_ref[:, sl] = vn
            gates = (gates + _dot(qn, wgt_ref[0, sl, :]) + _dot(kn, wgt_ref[1, sl, :])
                     + _dot(vn, wgt_ref[2, sl, :]))
        st["gates"] = gates

    project(groups[0])
    for gi, c0 in enumerate(groups):
        if gi + 1 < len(groups):
            project(groups[gi + 1])
        conv(c0)
        maps(c0)
    gates = st["gates"]
    lane = lax.broadcasted_iota(jnp.int32, gates.shape, 1)
    gt_ref[...] = jnp.where(jnp.logical_and(lane >= ML_HEADS, lane < 2 * ML_HEADS),
                            _log_sigmoid(gates), gates)


def _ml_front(h, g, w, conv_w, conv_b, mq, mk, mv, wgt, bg, seq):
    t, d = h.shape
    tm = min(TM_PROJ, seq)
    tps = seq // tm
    act = jax.ShapeDtypeStruct((t, ML_INNER), BF16)
    row_spec = pl.BlockSpec((tm, ML_INNER), lambda i: (i, 0))
    return pl.pallas_call(
        functools.partial(_ml_front_kernel, tiles_per_seq=tps),
        out_shape=(act, act, act, act, act, jax.ShapeDtypeStruct((t, V7X_LANES), F32)),
        grid=(t // tm,),
        in_specs=[
            pl.BlockSpec((tm, d), lambda i: (i, 0)),
            _const_spec((1, d)),
            _const_spec(w.shape),
            _const_spec((ML_CONV, ML_INNER)),
            _const_spec((1, ML_INNER)),
            _const_spec(mq.shape), _const_spec(mk.shape), _const_spec(mv.shape),
            _const_spec(wgt.shape),
            _const_spec((1, V7X_LANES)),
        ],
        out_specs=(row_spec, row_spec, row_spec, row_spec, row_spec,
                   pl.BlockSpec((tm, V7X_LANES), lambda i: (i, 0))),
        scratch_shapes=[pltpu.VMEM((V7X_SUBLANES, ML_INNER), F32)],
        compiler_params=_params(("arbitrary",)),
        name="ml_front",
    )(h, g, w, conv_w, conv_b, mq, mk, mv, wgt, bg)


def _ml_chunk_kernel(q_ref, k_ref, v_ref, gt_ref, c_ref, z_ref, gn_ref, sk_ref, o_ref,
                     cst_ref, nst_ref, mst_ref):
    ci = pl.program_id(1)
    nb, ln, _ = q_ref.shape

    @pl.when(ci == 0)
    def _():
        cst_ref[...] = jnp.zeros_like(cst_ref)
        nst_ref[...] = jnp.zeros_like(nst_ref)
        mst_ref[...] = jnp.zeros_like(mst_ref)

    row = lax.broadcasted_iota(jnp.int32, (ln, ln), 0)
    col = lax.broadcasted_iota(jnp.int32, (ln, ln), 1)
    causal = col <= row
    tril = causal.astype(F32)
    heads = range(ML_HEADS)
    sls = [slice(hd * ML_DH, (hd + 1) * ML_DH) for hd in heads]

    def row_stages(bi):
        st = {}

        def load():
            gates = gt_ref[bi]
            cum = _dot_exact(tril, gates)
            st["gates"], st["cum"], st["gates_t"], st["cum_t"] = gates, cum, gates.T, cum.T
            st["q"] = [q_ref[bi, :, sl] for sl in sls]
            st["k"] = [k_ref[bi, :, sl] for sl in sls]
            st["v"] = [v_ref[bi, :, sl] for sl in sls]
            st["cst"] = [cst_ref[bi, hd] for hd in heads]
            st["n_row"] = [nst_ref[bi, hd:hd + 1, :] for hd in heads]
            st["m_prev"] = [mst_ref[bi, hd:hd + 1, 0:1] for hd in heads]
            st["b_col"] = [cum[:, ML_HEADS + hd:ML_HEADS + hd + 1] for hd in heads]

        def scores():
            st["qk"] = [_dot_nt(st["q"][hd], st["k"][hd]) for hd in heads]
            st["inter"] = [_dot(st["q"][hd], st["cst"][hd].astype(BF16)) for hd in heads]

        def decay():
            b_col, m_prev = st["b_col"], st["m_prev"]
            log_d = [jnp.where(causal, b_col[hd] - st["cum_t"][ML_HEADS + hd:ML_HEADS + hd + 1, :]
                               + st["gates_t"][hd:hd + 1, :], -jnp.inf) for hd in heads]
            log_inter = [b_col[hd] + m_prev[hd] for hd in heads]
            st["m_row"] = [jnp.maximum(log_inter[hd], jnp.max(log_d[hd], axis=-1, keepdims=True)) for hd in heads]
            st["w_inter"] = [jnp.exp(log_inter[hd] - st["m_row"][hd]) for hd in heads]
            st["s"] = [st["qk"][hd] * jnp.exp(log_d[hd] - st["m_row"][hd]) for hd in heads]

        def readout():
            s, q = st["s"], st["q"]
            num = [_dot(s[hd].astype(BF16), st["v"][hd]) + st["w_inter"][hd] * st["inter"][hd] for hd in heads]
            den = [jnp.sum(s[hd], axis=-1, keepdims=True) + st["w_inter"][hd] * jnp.sum(
                q[hd].astype(F32) * st["n_row"][hd], axis=-1, keepdims=True) for hd in heads]
            st["hh"] = [num[hd] / jnp.maximum(jnp.abs(den[hd]), jnp.exp(-st["m_row"][hd])) for hd in heads]

        def state():
            b_col, m_prev = st["b_col"], st["m_prev"]
            for hd in heads:
                b_last = b_col[hd][ln - 1:ln, :]
                log_w = b_last - b_col[hd] + st["gates"][:, hd:hd + 1]
                m_new = jnp.maximum(b_last + m_prev[hd], jnp.max(log_w, axis=0, keepdims=True))
                kw = st["k"][hd].astype(F32) * jnp.exp(log_w - m_new)
                carry = jnp.exp(b_last + m_prev[hd] - m_new)
                cst_ref[bi, hd] = carry * st["cst"][hd] + _dot_tn(kw.astype(BF16), st["v"][hd])
                nst_ref[bi, hd:hd + 1, :] = carry * st["n_row"][hd] + jnp.sum(kw, axis=0, keepdims=True)
                mst_ref[bi, hd:hd + 1, :] = jnp.broadcast_to(m_new, (1, V7X_LANES))

        def epilogue(hd):
            sl = sls[hd]
            hh = st["hh"][hd]
            hc = hh - jnp.mean(hh, axis=-1, keepdims=True)
            y = hc * lax.rsqrt(jnp.mean(hc * hc, axis=-1, keepdims=True) + ML_EPS) * gn_ref[:, sl]
            y = (y + sk_ref[:, sl] * c_ref[bi, :, sl].astype(F32)) * z_ref[bi, :, sl].astype(F32)
            o_ref[bi, :, sl] = y.astype(BF16)

        return [load, scores, decay, readout, state], [functools.partial(epilogue, hd) for hd in heads]

    pending = []
    for bi in range(nb):
        stages, epilogues = row_stages(bi)
        for i, stage in enumerate(stages):
            stage()
            if i < len(pending):
                pending[i]()
        for piece in pending[len(stages):]:
            piece()
        pending = epilogues
    for piece in pending:
        piece()


def _ml_chunk(q, k, v, gates, c, z, gn, skip, batch, seq):
    t = q.shape[0]
    ln = min(ML_CHUNK, seq)
    nc = seq // ln
    nb = math.gcd(batch, ML_BATCH_PER_STEP)
    row_spec = pl.BlockSpec((nb, ln, ML_INNER), lambda b, ci: (b, ci, 0))
    vec = pl.BlockSpec((1, ML_INNER), lambda b, ci: (0, 0))
    acts = [x.reshape(batch, seq, ML_INNER) for x in (q, k, v)]
    out = pl.pallas_call(
        _ml_chunk_kernel,
        out_shape=jax.ShapeDtypeStruct((batch, seq, ML_INNER), BF16),
        grid=(batch // nb, nc),
        in_specs=[
            row_spec, row_spec, row_spec,
            pl.BlockSpec((nb, ln, V7X_LANES), lambda b, ci: (b, ci, 0)),
            row_spec,
            row_spec,
            vec, vec,
        ],
        out_specs=row_spec,
        scratch_shapes=[
            pltpu.VMEM((nb, ML_HEADS, ML_DH, ML_DH), F32),
            pltpu.VMEM((nb, V7X_SUBLANES, ML_DH), F32),
            pltpu.VMEM((nb, V7X_SUBLANES, V7X_LANES), F32),
        ],
        compiler_params=_params(("parallel", "arbitrary")),
        name="ml_chunk",
    )(*acts, gates.reshape(batch, seq, V7X_LANES), c.reshape(batch, seq, ML_INNER),
      z.reshape(batch, seq, ML_INNER), gn, skip)
    return out.reshape(t, ML_INNER)


def _rw_pre_kernel(h_ref, hp_ref, g_ref, mu_ref, wrkv_ref, la_ref, lbw_ref, lba_ref, lbg_ref,
                   w0_ref, a0_ref, kk_ref, ka_ref,
                   r_out, lw_out, k_out, v_out, kk_out, kb_out, g_out, *, tiles_per_seq,
                   lora_w, lora_a):
    i = pl.program_id(0)
    tm = h_ref.shape[0]
    g = g_ref[...]
    xn = _rms(h_ref[...], g)
    first = (i % tiles_per_seq) == 0
    xp8 = jnp.where(first, 0.0, _rms(hp_ref[...], g))
    dx = _prev_rows(xn, xp8, 1) - xn

    def matmuls(rows):
        def mix(n):
            return (xn[rows] + dx[rows] * mu_ref[n:n + 1, :]).astype(BF16)

        out = {"r": _dot(mix(0), wrkv_ref[0]), "k": _dot(mix(1), wrkv_ref[1]), "v": _dot(mix(2), wrkv_ref[2])}
        hw = jnp.tanh(_dot(mix(3), la_ref[:, 0:lora_w])).astype(BF16)
        ha = _dot(mix(4), la_ref[:, lora_w:lora_w + lora_a]).astype(BF16)
        hg = jax.nn.sigmoid(_dot(mix(5), la_ref[:, lora_w + lora_a:])).astype(BF16)
        out["w_pre"] = _dot(hw, lbw_ref[...])
        out["a_pre"] = _dot(ha, lba_ref[...])
        g_out[rows, :] = _dot(hg, lbg_ref[...]).astype(BF16)
        return out

    def tail(rows, m):
        lw_out[rows, :] = -math.exp(-0.5) * jax.nn.sigmoid(w0_ref[...] + m["w_pre"])
        alpha = jax.nn.sigmoid(a0_ref[...] + m["a_pre"])
        kk = m["k"] * kk_ref[...]
        r_out[rows, :] = m["r"].astype(BF16)
        k_out[rows, :] = (m["k"] * (1.0 + (alpha - 1.0) * ka_ref[...])).astype(BF16)
        v_out[rows, :] = m["v"].astype(BF16)
        kk_out[rows, :] = kk.astype(BF16)
        kb_out[rows, :] = (kk * alpha).astype(BF16)

    halves = [slice(0, tm // 2), slice(tm // 2, tm)]
    first_half = matmuls(halves[0])
    second_half = matmuls(halves[1])
    tail(halves[0], first_half)
    tail(halves[1], second_half)


def _rw_pre(h, g, mu, wrkv, la, lbw, lba, lbg, w0, a0, k_k, k_a, seq, lora_w, lora_a):
    t, d = h.shape
    tm = min(TM_PRE, seq)
    tps = seq // tm
    rb = tm // V7X_SUBLANES
    row_spec = pl.BlockSpec((tm, d), lambda i: (i, 0))
    act = jax.ShapeDtypeStruct((t, d), BF16)
    vec = _const_spec((1, d))
    return pl.pallas_call(
        functools.partial(_rw_pre_kernel, tiles_per_seq=tps, lora_w=lora_w, lora_a=lora_a),
        out_shape=(act, jax.ShapeDtypeStruct((t, d), F32), act, act, act, act, act),
        grid=(t // tm,),
        in_specs=[
            row_spec,
            pl.BlockSpec((V7X_SUBLANES, d), lambda i: (jnp.maximum(i * rb - 1, 0), 0)),
            vec,
            _const_spec(mu.shape),
            _const_spec(wrkv.shape),
            _const_spec(la.shape),
            _const_spec(lbw.shape), _const_spec(lba.shape), _const_spec(lbg.shape),
            vec, vec, vec, vec,
        ],
        out_specs=(row_spec,) * 7,
        compiler_params=_params(("parallel",)),
        name="rw_pre",
    )(h, h, g, mu, wrkv, la, lbw, lba, lbg, w0, a0, k_k, k_a)


def _rw_chunk_kernel(r_ref, lw_ref, k_ref, v_ref, kk_ref, kb_ref, g_ref, rk_ref, gng_ref, gnb_ref,
                     o_ref, st_ref):
    ci = pl.program_id(1)
    nb, ln, _ = r_ref.shape
    pair = 2 * RW_DH

    @pl.when(ci == 0)
    def _():
        st_ref[...] = jnp.zeros_like(st_ref)

    trow = lax.broadcasted_iota(jnp.int32, (ln, ln), 0)
    tcol = lax.broadcasted_iota(jnp.int32, (ln, ln), 1)
    tril = (tcol <= trow).astype(F32)
    lane = lax.broadcasted_iota(jnp.int32, (ln, pair), 1)
    head0 = lane < RW_DH
    srow = lax.broadcasted_iota(jnp.int32, (2 * ln, 2 * ln), 0)
    scol = lax.broadcasted_iota(jnp.int32, (2 * ln, 2 * ln), 1)
    strict_t = srow < scol
    incl_t = srow <= scol
    eye = (srow == scol).astype(F32)
    same_blk = (srow // RW_BLOCK) == (scol // RW_BLOCK)

    def stack(x):
        return jnp.concatenate([jnp.where(head0, x, 0.0), jnp.where(head0, 0.0, x)], axis=0).astype(BF16)

    pair_slices = [slice(p * pair, (p + 1) * pair) for p in range(RW_HEADS // 2)]

    def head_sums(x):
        s0 = jnp.sum(jnp.where(head0, x, 0.0), axis=-1, keepdims=True)
        s1 = jnp.sum(jnp.where(head0, 0.0, x), axis=-1, keepdims=True)
        return s0, s1

    def make_group(rows):
        chains = [(bi, p) for bi in rows for p in range(RW_HEADS // 2)]
        ids = range(len(chains))
        gs = {k_: [] for k_ in ("ar_s", "b_s", "k_s", "bk", "v_t", "g_last", "bonus")}

        def prologue(bi):
            lw = lw_ref[bi]
            cum = _dot_exact(tril, lw)
            cum_last = cum[ln - 1:ln, :]
            g_inv = jnp.exp(-cum)
            g_rem = jnp.exp(cum_last - cum)
            kk_raw = kk_ref[bi].astype(F32)
            r_all = r_ref[bi].astype(F32)
            kk = k_ref[bi].astype(F32)
            v_all = v_ref[bi].astype(F32)
            rkk = r_all * kk * rk_ref[...]
            inv_parts = []
            for sl in pair_slices:
                s0, s1 = head_sums(kk_raw[:, sl] * kk_raw[:, sl])
                inv_parts.append(jnp.where(head0, lax.rsqrt(jnp.maximum(s0, 1e-24)),
                                           lax.rsqrt(jnp.maximum(s1, 1e-24))))
                t0, t1 = head_sums(rkk[:, sl])
                gs["bonus"].append(jnp.where(head0, t0, t1) * v_all[:, sl])
            inv_all = jnp.concatenate(inv_parts, axis=1)
            kn = kk_raw * inv_all
            bb = kb_ref[bi].astype(F32) * inv_all
            a_all = -kn * jnp.exp(cum - lw)
            rt_all = r_all * jnp.exp(cum)
            bt_all = bb * g_inv
            kt_all = kk * g_inv
            bh_all = bb * g_rem
            kh_all = kk * g_rem
            gl = jnp.exp(cum_last)
            for sl in pair_slices:
                gs["ar_s"].append(jnp.concatenate([stack(a_all[:, sl]), stack(rt_all[:, sl])], axis=0))
                gs["b_s"].append(stack(bt_all[:, sl]))
                gs["k_s"].append(stack(kt_all[:, sl]))
                gs["bk"].append(jnp.concatenate([stack(bh_all[:, sl]), stack(kh_all[:, sl])], axis=0))
                vt = v_all[:, sl].T
                gs["v_t"].append(jnp.concatenate([vt[:RW_DH], vt[RW_DH:]], axis=1).astype(BF16))
                gs["g_last"].append(gl[:, sl])

        def products():
            gs["st"] = [st_ref[bi, p] for bi, p in chains]
            gs["m"] = [_dot_nt(jnp.concatenate([gs["b_s"][c], gs["k_s"][c], gs["st"][c].astype(BF16)], axis=0),
                               gs["ar_s"][c]) for c in ids]

        def masks():
            m = gs["m"]
            nt = [jnp.where(strict_t, m[c][:2 * ln, :2 * ln], 0.0) for c in ids]
            gs["pw"] = [jnp.where(same_blk, nt[c], 0.0) for c in ids]
            gs["noff"] = [jnp.where(same_blk, 0.0, nt[c]).astype(BF16) for c in ids]
            ak = [jnp.where(strict_t, m[c][2 * ln:4 * ln, :2 * ln], 0.0).astype(BF16) for c in ids]
            gs["rbk"] = [jnp.concatenate([jnp.where(incl_t, m[c][:2 * ln, 2 * ln:], 0.0).astype(BF16),
                                          jnp.where(incl_t, m[c][2 * ln:4 * ln, 2 * ln:], 0.0).astype(BF16)],
                                         axis=0) for c in ids]
            gs["inter_r"] = [m[c][4 * ln:, 2 * ln:] for c in ids]
            gs["u"] = [m[c][4 * ln:, :2 * ln] + _dot(gs["v_t"][c], ak[c]) for c in ids]

        def diag_first():
            p16 = [gs["pw"][c].astype(BF16) for c in ids]
            gs["e"] = [gs["pw"][c] + eye for c in ids]
            gs["pw"] = [_dot(p16[c], p16[c]) for c in ids]

        def diag_mid():
            q16 = [gs["pw"][c].astype(BF16) for c in ids]
            both = [_dot(jnp.concatenate([gs["e"][c].astype(BF16), q16[c]], axis=0), q16[c]) for c in ids]
            gs["e"] = [gs["e"][c] + both[c][:2 * ln] for c in ids]
            gs["pw"] = [both[c][2 * ln:] for c in ids]

        def diag_last():
            gs["e"] = [gs["e"][c] + _dot(gs["e"][c].astype(BF16), gs["pw"][c].astype(BF16)) for c in ids]

        def off_first():
            d16 = [gs["e"][c].astype(BF16) for c in ids]
            both = [_dot(jnp.concatenate([gs["noff"][c], gs["u"][c].astype(BF16)], axis=0), d16[c]) for c in ids]
            gs["pw"] = [both[c][:2 * ln] for c in ids]
            gs["u"] = [both[c][2 * ln:] for c in ids]

        def off_mid():
            m16 = [gs["pw"][c].astype(BF16) for c in ids]
            both = [_dot(jnp.concatenate([gs["u"][c].astype(BF16), m16[c]], axis=0), m16[c]) for c in ids]
            gs["u"] = [gs["u"][c] + both[c][:RW_DH] for c in ids]
            gs["pw"] = [both[c][RW_DH:] for c in ids]

        def off_last():
            gs["u"] = [gs["u"][c] + _dot(gs["u"][c].astype(BF16), gs["pw"][c].astype(BF16)) for c in ids]

        diag_levels = int(math.log2(RW_BLOCK))
        off_levels = max(int(math.ceil(math.log2(ln // RW_BLOCK))), 1)
        solve = ([diag_first] + [diag_mid] * (diag_levels - 2) + [diag_last]
                 + [off_first] + [off_mid] * (off_levels - 1) + [off_last])

        def readout():
            uv = [jnp.concatenate([gs["u"][c].astype(BF16), gs["v_t"][c]], axis=1) for c in ids]
            gs["y_t"] = [gs["inter_r"][c] + _dot(uv[c], gs["rbk"][c]) for c in ids]
            for c, (bi, p) in enumerate(chains):
                st_ref[bi, p] = gs["st"][c] * gs["g_last"][c] + _dot(uv[c], gs["bk"][c])

        def epilogue(c):
            bi, p = chains[c]
            sl = pair_slices[p]
            y_t = gs["y_t"][c]
            yc = y_t - jnp.mean(y_t, axis=0, keepdims=True)
            yn = (yc * lax.rsqrt(jnp.mean(yc * yc, axis=0, keepdims=True) + RW_GN_EPS)).T
            y_p = jnp.concatenate([yn[:ln], yn[ln:]], axis=1)
            out = y_p * gng_ref[:, sl] + gnb_ref[:, sl] + gs["bonus"][c]
            o_ref[bi, :, sl] = (out * g_ref[bi, :, sl].astype(F32)).astype(BF16)

        stages = [products, masks] + solve + [readout]
        return ([functools.partial(prologue, bi) for bi in rows], stages,
                [functools.partial(epilogue, c) for c in ids])

    def interleave(stages, fillers):
        per = -(-len(fillers) // max(len(stages), 1)) if fillers else 0
        fillers = list(fillers)
        for stage in stages:
            stage()
            for piece in fillers[:per]:
                piece()
            fillers = fillers[per:]
        for piece in fillers:
            piece()

    half = max(nb // 2, 1)
    groups = [make_group(range(g0, min(g0 + half, nb))) for g0 in range(0, nb, half)]
    for piece in groups[0][0]:
        piece()
    pending = []
    for gi, (_, stages, epilogues) in enumerate(groups):
        nxt = groups[gi + 1][0] if gi + 1 < len(groups) else []
        interleave(stages, list(pending) + list(nxt))
        pending = epilogues
    for piece in pending:
        piece()


def _rw_chunk(r, lw, k, v, kk, kb, g, r_k, gn_g, gn_b, batch, seq):
    t, d = r.shape
    ln = min(RW_CHUNK, seq)
    nc = seq // ln
    nb = math.gcd(batch, RW_BATCH_PER_STEP)
    row_spec = pl.BlockSpec((nb, ln, d), lambda bi, ci: (bi, ci, 0))
    vec = pl.BlockSpec((1, d), lambda bi, ci: (0, 0))
    acts = [x.reshape(batch, seq, d) for x in (r, lw, k, v, kk, kb, g)]
    out = pl.pallas_call(
        _rw_chunk_kernel,
        out_shape=jax.ShapeDtypeStruct((batch, seq, d), BF16),
        grid=(batch // nb, nc),
        in_specs=[row_spec] * 7 + [vec, vec, vec],
        out_specs=row_spec,
        scratch_shapes=[pltpu.VMEM((nb, RW_HEADS // 2, RW_DH, 2 * RW_DH), F32)],
        compiler_params=_params(("parallel", "arbitrary")),
        name="rw_chunk",
    )(*acts, r_k, gn_g, gn_b)
    return out.reshape(t, d)


def _rope_tables(seq):
    half = RET_DK // 2
    pos = jnp.arange(seq, dtype=F32)
    inv_freq = 1.0 / (ROPE_BASE ** jnp.linspace(0.0, 1.0, half, dtype=F32))
    ang = pos[:, None] * inv_freq[None, :]
    local = (jnp.arange(seq) % min(RET_CHUNK, seq)).astype(F32) + 1.0
    log_gamma = jnp.asarray([_ret_log_gamma(hd) for hd in range(RET_HEADS)], F32)
    q_scale = jnp.exp(log_gamma[:, None] * local[None, :])
    k_scale = jnp.exp(-log_gamma[:, None] * local[None, :]) * RET_DK ** -0.5
    scale = jnp.concatenate([q_scale, k_scale], axis=0)[:, :, None]
    return jnp.cos(ang)[None] * scale, jnp.sin(ang)[None] * scale


def _retention_mixer(h, norm_g, w_in16, layer, gn, batch, seq):
    cos, sin = _rope_tables(seq)
    proj = _ret_proj(h, norm_g[None, :], w_in16, layer, cos, sin, seq)
    return _ret_chunk(proj, gn[None, :].astype(F32), batch, seq)


def _block_diag_tiles(w, tile):
    nb, bs, _ = w.shape
    rows = w.reshape(nb * bs // tile, tile, bs)
    dense = jnp.tile(rows, (1, 1, tile // bs))
    idx = jnp.arange(tile) // bs
    return jnp.where(idx[:, None] == idx[None, :], dense, 0.0)


def _mlstm_mixer(h, norm_g, w_in, conv_w, conv_b, wq, wk, wv, w_gate, b_gate, gn, skip, batch, seq):
    tile = 2 * V7X_LANES
    mq = _block_diag_tiles(wq, tile).astype(BF16)
    mk = _block_diag_tiles(wk, tile).astype(BF16)
    mv = _block_diag_tiles(wv, tile).astype(BF16)
    n_gate = w_gate.shape[1]
    wgt = jnp.pad(w_gate.reshape(3, ML_INNER, n_gate), ((0, 0), (0, 0), (0, V7X_LANES - n_gate))).astype(BF16)
    bg = jnp.pad(b_gate, (0, V7X_LANES - n_gate))[None, :].astype(F32)
    c, q, k, v, z, gates = _ml_front(h, norm_g[None, :], w_in.astype(BF16), conv_w, conv_b[None, :],
                                     mq, mk, mv, wgt, bg, seq)
    return _ml_chunk(q, k, v, gates, c, z, gn[None, :], skip[None, :], batch, seq)


def _pad_to(x, axis, size):
    pad = [(0, 0)] * x.ndim
    pad[axis] = (0, size - x.shape[axis])
    return jnp.pad(x, pad)


def _rwkv_mixer(h, norm_g, mu, w_rkv, w0, w_la, w_lb, a0, a_la, a_lb, g_la, g_lb, k_k, k_a, r_k,
                gn_g, gn_b, batch, seq):
    d = h.shape[1]
    lw_ = -(-w_la.shape[1] // V7X_LANES) * V7X_LANES
    la_ = -(-a_la.shape[1] // V7X_LANES) * V7X_LANES
    lg_ = -(-g_la.shape[1] // V7X_LANES) * V7X_LANES
    la = jnp.concatenate([_pad_to(w_la, 1, lw_), _pad_to(a_la, 1, la_), _pad_to(g_la, 1, lg_)],
                         axis=1).astype(BF16)
    lbw = _pad_to(w_lb, 0, lw_).astype(BF16)
    lba = _pad_to(a_lb, 0, la_).astype(BF16)
    lbg = _pad_to(g_lb, 0, lg_).astype(BF16)
    r, lw, k, v, kk, kb, g = _rw_pre(h, norm_g[None, :], mu, w_rkv.astype(BF16), la, lbw, lba, lbg,
                                     w0[None, :], a0[None, :], k_k[None, :], k_a[None, :], seq, lw_, la_)
    return _rw_chunk(r, lw, k, v, kk, kb, g, r_k.reshape(1, d), gn_g[None, :], gn_b[None, :], batch, seq)


def kernel(x, norm_mix, norm_ffn, norm_final, ret_w_in, ret_gn, ret_w_out, ml_w_in, ml_conv_w, ml_conv_b, ml_wq, ml_wk, ml_wv, ml_w_gate, ml_b_gate, ml_gn, ml_skip, ml_w_out, rw_mu, rw_w_rkv, rw_w0, rw_w_lora_a, rw_w_lora_b, rw_a0, rw_a_lora_a, rw_a_lora_b, rw_g_lora_a, rw_g_lora_b, rw_k_k, rw_k_a, rw_r_k, rw_gn_g, rw_gn_b, rw_w_out, ffn_w_gu, ffn_w_down):
    batch, seq, d = x.shape
    depth = norm_mix.shape[0]
    h = x.reshape(batch * seq, d)
    ret_w_in16 = ret_w_in.astype(BF16)
    w_out16 = (ret_w_out.astype(BF16), ml_w_out.astype(BF16), rw_w_out.astype(BF16))
    w_gu16 = ffn_w_gu.astype(BF16)
    w_down16 = ffn_w_down.astype(BF16)
    for i in range(depth):
        kind, j = i % 3, i // 3
        if kind == 0:
            y = _retention_mixer(h, norm_mix[i], ret_w_in16, j, ret_gn[j], batch, seq)
        elif kind == 1:
            y = _mlstm_mixer(h, norm_mix[i], ml_w_in[j], ml_conv_w[j], ml_conv_b[j], ml_wq[j], ml_wk[j],
                             ml_wv[j], ml_w_gate[j], ml_b_gate[j], ml_gn[j], ml_skip[j], batch, seq)
        else:
            y = _rwkv_mixer(h, norm_mix[i], rw_mu[j], rw_w_rkv[j], rw_w0[j], rw_w_lora_a[j],
                            rw_w_lora_b[j], rw_a0[j], rw_a_lora_a[j], rw_a_lora_b[j], rw_g_lora_a[j],
                            rw_g_lora_b[j], rw_k_k[j], rw_k_a[j], rw_r_k[j], rw_gn_g[j], rw_gn_b[j],
                            batch, seq)
        h = _mix_ffn(y, w_out16[kind], j, h, norm_ffn[i][None, :], w_gu16, w_down16, i,
                     norm_final[None, :], i == depth - 1)
    return h.reshape(batch, seq, d)
```

```python
import functools
import math

import jax
import jax.numpy as jnp
from jax import lax
from jax.experimental import pallas as pl
from jax.experimental.pallas import tpu as pltpu

F32 = jnp.float32
BF16 = jnp.bfloat16

D_FF = 2816
RMS_EPS = 1e-6

RET_HEADS = 4
RET_DK = 256
RET_DV = 512
RET_EPS = 1e-6
ROPE_BASE = 10000.0

ML_INNER = 2048
ML_HEADS = 4
ML_DH = 512
ML_CONV = 4
ML_EPS = 1e-6

RW_HEADS = 16
RW_DH = 64
RW_GN_EPS = 64e-5

V7X_LANES = 128
V7X_SUBLANES = 8
V7X_VMEM_LIMIT = 56 * 1024 * 1024

TM_PROJ = 512
TM_RET_PROJ = 512
PROJ_CHUNK = 512
TM_FFN = 1024
FFN_CHUNK = 256
TM_PRE = 512
RET_CHUNK = 256
RET_BATCH_PER_STEP = 4
ML_CHUNK = 256
ML_BATCH_PER_STEP = 2
RW_CHUNK = 64
RW_BLOCK = 16
RW_BATCH_PER_STEP = 4


def _params(sem, vmem=V7X_VMEM_LIMIT):
    return pltpu.CompilerParams(dimension_semantics=sem, vmem_limit_bytes=vmem)


def _dot(a, b):
    return jnp.dot(a, b, preferred_element_type=F32)


def _dot_nt(a, b):
    return lax.dot_general(a, b, (((1,), (1,)), ((), ())), preferred_element_type=F32)


def _dot_tn(a, b):
    return lax.dot_general(a, b, (((0,), (0,)), ((), ())), preferred_element_type=F32)


def _dot_exact(a, b):
    hi = b.astype(BF16)
    lo = (b - hi.astype(F32)).astype(BF16)
    a16 = a.astype(BF16)
    return _dot(a16, hi) + _dot(a16, lo)


def _rms(x, g):
    ms = jnp.mean(x * x, axis=-1, keepdims=True)
    return x * lax.rsqrt(ms + RMS_EPS) * g


def _silu(x):
    return x * jax.nn.sigmoid(x)


def _const_spec(shape):
    nd = len(shape)
    return pl.BlockSpec(shape, lambda *_: (0,) * nd, pipeline_mode=pl.Buffered(1))


def _layer_spec(shape, layer):
    nd = len(shape)
    return pl.BlockSpec((None,) + tuple(shape), lambda *_: (layer,) + (0,) * nd,
                        pipeline_mode=pl.Buffered(1))


def _ret_proj_kernel(h_ref, g_ref, w_ref, cos_ref, sin_ref, o_ref):
    xn = _rms(h_ref[...], g_ref[...]).astype(BF16)
    half = RET_DK // 2
    hk = RET_HEADS * RET_DK
    hv = RET_HEADS * RET_DV
    for hd in range(2 * RET_HEADS):
        c0 = hd * RET_DK
        acc = _dot(xn, w_ref[:, c0:c0 + RET_DK])
        cos = cos_ref[hd]
        sin = sin_ref[hd]
        t1 = acc[:, :half]
        t2 = acc[:, half:]
        o_ref[:, c0:c0 + half] = (t1 * cos - t2 * sin).astype(BF16)
        o_ref[:, c0 + half:c0 + RET_DK] = (t2 * cos + t1 * sin).astype(BF16)
    for c0 in range(2 * hk, 2 * hk + hv, PROJ_CHUNK):
        o_ref[:, c0:c0 + PROJ_CHUNK] = _dot(xn, w_ref[:, c0:c0 + PROJ_CHUNK]).astype(BF16)
    for c0 in range(2 * hk + hv, 2 * hk + 2 * hv, PROJ_CHUNK):
        o_ref[:, c0:c0 + PROJ_CHUNK] = _silu(_dot(xn, w_ref[:, c0:c0 + PROJ_CHUNK])).astype(BF16)


def _ret_proj(h, g, w, layer, cos, sin, seq):
    t, d = h.shape
    n = w.shape[2]
    tm = min(TM_RET_PROJ, seq)
    spt = seq // tm
    return pl.pallas_call(
        _ret_proj_kernel,
        out_shape=jax.ShapeDtypeStruct((t, n), BF16),
        grid=(t // tm,),
        in_specs=[
            pl.BlockSpec((tm, d), lambda i: (i, 0)),
            _const_spec((1, d)),
            _layer_spec((d, n), layer),
            pl.BlockSpec((2 * RET_HEADS, tm, RET_DK // 2), lambda i: (0, i % spt, 0)),
            pl.BlockSpec((2 * RET_HEADS, tm, RET_DK // 2), lambda i: (0, i % spt, 0)),
        ],
        out_specs=pl.BlockSpec((tm, n), lambda i: (i, 0)),
        compiler_params=_params(("parallel",)),
        name="ret_proj",
    )(h, g, w, cos, sin)


def _mix_ffn_kernel(y_ref, wo_ref, h_ref, g_ref, wgu_ref, wd_ref, gf_ref, o_ref, *, final_norm):
    h = h_ref[...] + _dot(y_ref[...], wo_ref[...])
    xn = _rms(h, g_ref[...]).astype(BF16)
    acc = h
    for f in range(0, D_FF, FFN_CHUNK):
        gate = _dot(xn, wgu_ref[:, f:f + FFN_CHUNK])
        up = _dot(xn, wgu_ref[:, D_FF + f:D_FF + f + FFN_CHUNK])
        act = (_silu(gate) * up).astype(BF16)
        acc = acc + _dot(act, wd_ref[f:f + FFN_CHUNK, :])
    if final_norm:
        acc = _rms(acc, gf_ref[...])
    o_ref[...] = acc


def _mix_ffn(y, wo, wo_layer, h, g, wgu, wd, ffn_layer, gf, final_norm):
    t, d = h.shape
    k = y.shape[1]
    tm = min(TM_FFN, t)
    return pl.pallas_call(
        functools.partial(_mix_ffn_kernel, final_norm=final_norm),
        out_shape=jax.ShapeDtypeStruct((t, d), F32),
        grid=(t // tm,),
        in_specs=[
            pl.BlockSpec((tm, k), lambda i: (i, 0)),
            _layer_spec((k, d), wo_layer),
            pl.BlockSpec((tm, d), lambda i: (i, 0)),
            _const_spec((1, d)),
            _layer_spec((d, 2 * D_FF), ffn_layer),
            _layer_spec((D_FF, d), ffn_layer),
            _const_spec((1, d)),
        ],
        out_specs=pl.BlockSpec((tm, d), lambda i: (i, 0)),
        compiler_params=_params(("parallel",)),
        name="mix_ffn",
    )(y, wo, h, g, wgu, wd, gf)


def _ret_log_gamma(head):
    return math.log1p(-2.0 ** (-5.0 - head))


def _ret_chunk_kernel(q_ref, k_ref, v_ref, g_ref, gn_ref, o_ref, st_ref):
    c = pl.program_id(1)
    nb, ln, _ = q_ref.shape

    @pl.when(c == 0)
    def _():
        st_ref[...] = jnp.zeros_like(st_ref)

    causal = (lax.broadcasted_iota(jnp.int32, (ln, ln), 1) <= lax.broadcasted_iota(jnp.int32, (ln, ln), 0))
    heads = range(RET_HEADS)

    def row_stages(bi):
        rs = {}

        def scores():
            rs["q"] = [q_ref[bi, :, hd * RET_DK:(hd + 1) * RET_DK] for hd in heads]
            rs["k"] = [k_ref[bi, :, hd * RET_DK:(hd + 1) * RET_DK] for hd in heads]
            rs["v"] = [v_ref[bi, :, hd * RET_DV:(hd + 1) * RET_DV] for hd in heads]
            rs["st"] = [st_ref[bi, hd] for hd in heads]
            rs["qk"] = [_dot_nt(rs["q"][hd], rs["k"][hd]) for hd in heads]
            rs["inter"] = [_dot(rs["q"][hd], rs["st"][hd].astype(BF16)) for hd in heads]

        def readout():
            sc = [jnp.where(causal, rs["qk"][hd], 0.0).astype(BF16) for hd in heads]
            rs["out"] = [_dot(sc[hd], rs["v"][hd]) + rs["inter"][hd] for hd in heads]

        def state():
            for hd in heads:
                st_ref[bi, hd] = ((rs["st"][hd] + _dot_tn(rs["k"][hd], rs["v"][hd]))
                                  * math.exp(_ret_log_gamma(hd) * ln))

        def epilogue(hd):
            sl = slice(hd * RET_DV, (hd + 1) * RET_DV)
            out = rs["out"][hd]
            ms = jnp.mean(out * out, axis=-1, keepdims=True)
            y = out * lax.rsqrt(ms + RET_EPS) * gn_ref[:, sl]
            o_ref[bi, :, sl] = (y * g_ref[bi, :, sl].astype(F32)).astype(BF16)

        return [scores, readout, state], [functools.partial(epilogue, hd) for hd in heads]

    pending = []
    for bi in range(nb):
        stages, epilogues = row_stages(bi)
        per = -(-len(pending) // len(stages))
        for i, stage in enumerate(stages):
            stage()
            for piece in pending[i * per:(i + 1) * per]:
                piece()
        pending = epilogues
    for piece in pending:
        piece()


def _ret_chunk(proj, gn, batch, seq):
    t, n = proj.shape
    ln = min(RET_CHUNK, seq)
    nc = seq // ln
    nb = math.gcd(batch, RET_BATCH_PER_STEP)
    hk = RET_HEADS * RET_DK
    hv = RET_HEADS * RET_DV
    proj3 = proj.reshape(batch, seq, n)
    out = pl.pallas_call(
        _ret_chunk_kernel,
        out_shape=jax.ShapeDtypeStruct((batch, seq, hv), BF16),
        grid=(batch // nb, nc),
        in_specs=[
            pl.BlockSpec((nb, ln, hk), lambda b, c: (b, c, 0)),
            pl.BlockSpec((nb, ln, hk), lambda b, c: (b, c, 1)),
            pl.BlockSpec((nb, ln, hv), lambda b, c: (b, c, 1)),
            pl.BlockSpec((nb, ln, hv), lambda b, c: (b, c, 2)),
            _const_spec((1, hv)),
        ],
        out_specs=pl.BlockSpec((nb, ln, hv), lambda b, c: (b, c, 0)),
        scratch_shapes=[pltpu.VMEM((nb, RET_HEADS, RET_DK, RET_DV), F32)],
        compiler_params=_params(("parallel", "arbitrary")),
        name="ret_chunk",
    )(proj3, proj3, proj3, proj3, gn)
    return out.reshape(t, hv)


def _prev_rows(cur, prev8, shift):
    ext = jnp.concatenate([prev8, cur], axis=0)
    return pltpu.roll(ext, shift, axis=0)[V7X_SUBLANES:, :]


def _log_sigmoid(x):
    return jnp.minimum(x, 0.0) - jnp.log1p(jnp.exp(-jnp.abs(x)))


def _ml_front_kernel(h_ref, g_ref, w_ref, cw_ref, cb_ref, mq_ref, mk_ref, mv_ref, wgt_ref, bg_ref,
                     c_ref, q_ref, k_ref, v_ref, z_ref, gt_ref, tail_ref, *, tiles_per_seq):
    i = pl.program_id(0)
    tm = h_ref.shape[0]

    @pl.when((i % tiles_per_seq) == 0)
    def _():
        tail_ref[...] = jnp.zeros_like(tail_ref)

    xn = _rms(h_ref[...], g_ref[...]).astype(BF16)
    blk = mq_ref.shape[1]
    groups = list(range(0, ML_INNER, PROJ_CHUNK))
    st = {"gates": jnp.zeros((tm, V7X_LANES), F32) + bg_ref[...]}

    def project(c0):
        cs = slice(c0, c0 + PROJ_CHUNK)
        st["u", c0] = _dot(xn, w_ref[:, cs])
        z_ref[:, cs] = _silu(_dot(xn, w_ref[:, ML_INNER + c0:ML_INNER + c0 + PROJ_CHUNK])).astype(BF16)

    def conv(c0):
        cs = slice(c0, c0 + PROJ_CHUNK)
        u = st.pop(("u", c0))
        prev = tail_ref[:, cs]
        tail_ref[:, cs] = u[tm - V7X_SUBLANES:, :]
        acc = u * cw_ref[ML_CONV - 1:ML_CONV, cs] + cb_ref[:, cs]
        for s in range(1, ML_CONV):
            acc = acc + _prev_rows(u, prev, s) * cw_ref[ML_CONV - 1 - s:ML_CONV - s, cs]
        c16 = _silu(acc).astype(BF16)
        c_ref[:, cs] = c16
        st["c16", c0] = c16
        st["u16", c0] = u.astype(BF16)

    def maps(c0):
        c16 = st.pop(("c16", c0))
        u16 = st.pop(("u16", c0))
        gates = st["gates"]
        for n in range(PROJ_CHUNK // blk):
            ls = slice(n * blk, (n + 1) * blk)
            sl = slice(c0 + n * blk, c0 + (n + 1) * blk)
            tile = sl.start // blk
            qn = _dot(c16[:, ls], mq_ref[tile]).astype(BF16)
            kn = _dot(c16[:, ls], mk_ref[tile]).astype(BF16)
            vn = _dot(u16[:, ls], mv_ref[tile]).astype(BF16)
            q_ref[:, sl] = qn
            k_ref[:, sl] = (kn.astype(F32) * (ML_DH ** -0.5)).astype(BF16)
            v_ref[:, sl] = vn
            gates = (gates + _dot(qn, wgt_ref[0, sl, :]) + _dot(kn, wgt_ref[1, sl, :])
                     + _dot(vn, wgt_ref[2, sl, :]))
        st["gates"] = gates

    project(groups[0])
    for gi, c0 in enumerate(groups):
        if gi + 1 < len(groups):
            project(groups[gi + 1])
        conv(c0)
        maps(c0)
    gates = st["gates"]
    lane = lax.broadcasted_iota(jnp.int32, gates.shape, 1)
    gt_ref[...] = jnp.where(jnp.logical_and(lane >= ML_HEADS, lane < 2 * ML_HEADS),
                            _log_sigmoid(gates), gates)


def _ml_front(h, g, w, conv_w, conv_b, mq, mk, mv, wgt, bg, seq):
    t, d = h.shape
    tm = min(TM_PROJ, seq)
    tps = seq // tm
    act = jax.ShapeDtypeStruct((t, ML_INNER), BF16)
    row_spec = pl.BlockSpec((tm, ML_INNER), lambda i: (i, 0))
    return pl.pallas_call(
        functools.partial(_ml_front_kernel, tiles_per_seq=tps),
        out_shape=(act, act, act, act, act, jax.ShapeDtypeStruct((t, V7X_LANES), F32)),
        grid=(t // tm,),
        in_specs=[
            pl.BlockSpec((tm, d), lambda i: (i, 0)),
            _const_spec((1, d)),
            _const_spec(w.shape),
            _const_spec((ML_CONV, ML_INNER)),
            _const_spec((1, ML_INNER)),
            _const_spec(mq.shape), _const_spec(mk.shape), _const_spec(mv.shape),
            _const_spec(wgt.shape),
            _const_spec((1, V7X_LANES)),
        ],
        out_specs=(row_spec, row_spec, row_spec, row_spec, row_spec,
                   pl.BlockSpec((tm, V7X_LANES), lambda i: (i, 0))),
        scratch_shapes=[pltpu.VMEM((V7X_SUBLANES, ML_INNER), F32)],
        compiler_params=_params(("arbitrary",)),
        name="ml_front",
    )(h, g, w, conv_w, conv_b, mq, mk, mv, wgt, bg)


def _ml_chunk_kernel(q_ref, k_ref, v_ref, gt_ref, c_ref, z_ref, gn_ref, sk_ref, o_ref,
                     cst_ref, nst_ref, mst_ref):
    ci = pl.program_id(1)
    nb, ln, _ = q_ref.shape

    @pl.when(ci == 0)
    def _():
        cst_ref[...] = jnp.zeros_like(cst_ref)
        nst_ref[...] = jnp.zeros_like(nst_ref)
        mst_ref[...] = jnp.zeros_like(mst_ref)

    row = lax.broadcasted_iota(jnp.int32, (ln, ln), 0)
    col = lax.broadcasted_iota(jnp.int32, (ln, ln), 1)
    causal = col <= row
    tril = causal.astype(F32)
    heads = range(ML_HEADS)
    sls = [slice(hd * ML_DH, (hd + 1) * ML_DH) for hd in heads]

    def row_stages(bi):
        st = {}

        def load():
            gates = gt_ref[bi]
            cum = _dot_exact(tril, gates)
            st["gates"], st["cum"], st["gates_t"], st["cum_t"] = gates, cum, gates.T, cum.T
            st["q"] = [q_ref[bi, :, sl] for sl in sls]
            st["k"] = [k_ref[bi, :, sl] for sl in sls]
            st["v"] = [v_ref[bi, :, sl] for sl in sls]
            st["cst"] = [cst_ref[bi, hd] for hd in heads]
            st["n_row"] = [nst_ref[bi, hd:hd + 1, :] for hd in heads]
            st["m_prev"] = [mst_ref[bi, hd:hd + 1, 0:1] for hd in heads]
            st["b_col"] = [cum[:, ML_HEADS + hd:ML_HEADS + hd + 1] for hd in heads]

        def scores():
            st["qk"] = [_dot_nt(st["q"][hd], st["k"][hd]) for hd in heads]
            st["inter"] = [_dot(st["q"][hd], st["cst"][hd].astype(BF16)) for hd in heads]

        def decay():
            b_col, m_prev = st["b_col"], st["m_prev"]
            log_d = [jnp.where(causal, b_col[hd] - st["cum_t"][ML_HEADS + hd:ML_HEADS + hd + 1, :]
                               + st["gates_t"][hd:hd + 1, :], -jnp.inf) for hd in heads]
            log_inter = [b_col[hd] + m_prev[hd] for hd in heads]
            st["m_row"] = [jnp.maximum(log_inter[hd], jnp.max(log_d[hd], axis=-1, keepdims=True)) for hd in heads]
            st["w_inter"] = [jnp.exp(log_inter[hd] - st["m_row"][hd]) for hd in heads]
            st["s"] = [st["qk"][hd] * jnp.exp(log_d[hd] - st["m_row"][hd]) for hd in heads]

        def readout():
            s, q = st["s"], st["q"]
            num = [_dot(s[hd].astype(BF16), st["v"][hd]) + st["w_inter"][hd] * st["inter"][hd] for hd in heads]
            den = [jnp.sum(s[hd], axis=-1, keepdims=True) + st["w_inter"][hd] * jnp.sum(
                q[hd].astype(F32) * st["n_row"][hd], axis=-1, keepdims=True) for hd in heads]
            st["hh"] = [num[hd] / jnp.maximum(jnp.abs(den[hd]), jnp.exp(-st["m_row"][hd])) for hd in heads]

        def state():
            b_col, m_prev = st["b_col"], st["m_prev"]
            for hd in heads:
                b_last = b_col[hd][ln - 1:ln, :]
                log_w = b_last - b_col[hd] + st["gates"][:, hd:hd + 1]
                m_new = jnp.maximum(b_last + m_prev[hd], jnp.max(log_w, axis=0, keepdims=True))
                kw = st["k"][hd].astype(F32) * jnp.exp(log_w - m_new)
                carry = jnp.exp(b_last + m_prev[hd] - m_new)
                cst_ref[bi, hd] = carry * st["cst"][hd] + _dot_tn(kw.astype(BF16), st["v"][hd])
                nst_ref[bi, hd:hd + 1, :] = carry * st["n_row"][hd] + jnp.sum(kw, axis=0, keepdims=True)
                mst_ref[bi, hd:hd + 1, :] = jnp.broadcast_to(m_new, (1, V7X_LANES))

        def epilogue(hd):
            sl = sls[hd]
            hh = st["hh"][hd]
            hc = hh - jnp.mean(hh, axis=-1, keepdims=True)
            y = hc * lax.rsqrt(jnp.mean(hc * hc, axis=-1, keepdims=True) + ML_EPS) * gn_ref[:, sl]
            y = (y + sk_ref[:, sl] * c_ref[bi, :, sl].astype(F32)) * z_ref[bi, :, sl].astype(F32)
            o_ref[bi, :, sl] = y.astype(BF16)

        return [load, scores, decay, readout, state], [functools.partial(epilogue, hd) for hd in heads]

    pending = []
    for bi in range(nb):
        stages, epilogues = row_stages(bi)
        for i, stage in enumerate(stages):
            stage()
            if i < len(pending):
                pending[i]()
        for piece in pending[len(stages):]:
            piece()
        pending = epilogues
    for piece in pending:
        piece()


def _ml_chunk(q, k, v, gates, c, z, gn, skip, batch, seq):
    t = q.shape[0]
    ln = min(ML_CHUNK, seq)
    nc = seq // ln
    nb = math.gcd(batch, ML_BATCH_PER_STEP)
    row_spec = pl.BlockSpec((nb, ln, ML_INNER), lambda b, ci: (b, ci, 0))
    vec = pl.BlockSpec((1, ML_INNER), lambda b, ci: (0, 0))
    acts = [x.reshape(batch, seq, ML_INNER) for x in (q, k, v)]
    out = pl.pallas_call(
        _ml_chunk_kernel,
        out_shape=jax.ShapeDtypeStruct((batch, seq, ML_INNER), BF16),
        grid=(batch // nb, nc),
        in_specs=[
            row_spec, row_spec, row_spec,
            pl.BlockSpec((nb, ln, V7X_LANES), lambda b, ci: (b, ci, 0)),
            row_spec,
            row_spec,
            vec, vec,
        ],
        out_specs=row_spec,
        scratch_shapes=[
            pltpu.VMEM((nb, ML_HEADS, ML_DH, ML_DH), F32),
            pltpu.VMEM((nb, V7X_SUBLANES, ML_DH), F32),
            pltpu.VMEM((nb, V7X_SUBLANES, V7X_LANES), F32),
        ],
        compiler_params=_params(("parallel", "arbitrary")),
        name="ml_chunk",
    )(*acts, gates.reshape(batch, seq, V7X_LANES), c.reshape(batch, seq, ML_INNER),
      z.reshape(batch, seq, ML_INNER), gn, skip)
    return out.reshape(t, ML_INNER)


def _rw_pre_kernel(h_ref, hp_ref, g_ref, mu_ref, wrkv_ref, la_ref, lbw_ref, lba_ref, lbg_ref,
                   w0_ref, a0_ref, kk_ref, ka_ref,
                   r_out, lw_out, k_out, v_out, kk_out, kb_out, g_out, *, tiles_per_seq,
                   lora_w, lora_a):
    i = pl.program_id(0)
    tm = h_ref.shape[0]
    g = g_ref[...]
    xn = _rms(h_ref[...], g)
    first = (i % tiles_per_seq) == 0
    xp8 = jnp.where(first, 0.0, _rms(hp_ref[...], g))
    dx = _prev_rows(xn, xp8, 1) - xn

    def matmuls(rows):
        def mix(n):
            return (xn[rows] + dx[rows] * mu_ref[n:n + 1, :]).astype(BF16)

        out = {"r": _dot(mix(0), wrkv_ref[0]), "k": _dot(mix(1), wrkv_ref[1]), "v": _dot(mix(2), wrkv_ref[2])}
        hw = jnp.tanh(_dot(mix(3), la_ref[:, 0:lora_w])).astype(BF16)
        ha = _dot(mix(4), la_ref[:, lora_w:lora_w + lora_a]).astype(BF16)
        hg = jax.nn.sigmoid(_dot(mix(5), la_ref[:, lora_w + lora_a:])).astype(BF16)
        out["w_pre"] = _dot(hw, lbw_ref[...])
        out["a_pre"] = _dot(ha, lba_ref[...])
        g_out[rows, :] = _dot(hg, lbg_ref[...]).astype(BF16)
        return out

    def tail(rows, m):
        lw_out[rows, :] = -math.exp(-0.5) * jax.nn.sigmoid(w0_ref[...] + m["w_pre"])
        alpha = jax.nn.sigmoid(a0_ref[...] + m["a_pre"])
        kk = m["k"] * kk_ref[...]
        r_out[rows, :] = m["r"].astype(BF16)
        k_out[rows, :] = (m["k"] * (1.0 + (alpha - 1.0) * ka_ref[...])).astype(BF16)
        v_out[rows, :] = m["v"].astype(BF16)
        kk_out[rows, :] = kk.astype(BF16)
        kb_out[rows, :] = (kk * alpha).astype(BF16)

    halves = [slice(0, tm // 2), slice(tm // 2, tm)]
    first_half = matmuls(halves[0])
    second_half = matmuls(halves[1])
    tail(halves[0], first_half)
    tail(halves[1], second_half)


def _rw_pre(h, g, mu, wrkv, la, lbw, lba, lbg, w0, a0, k_k, k_a, seq, lora_w, lora_a):
    t, d = h.shape
    tm = min(TM_PRE, seq)
    tps = seq // tm
    rb = tm // V7X_SUBLANES
    row_spec = pl.BlockSpec((tm, d), lambda i: (i, 0))
    act = jax.ShapeDtypeStruct((t, d), BF16)
    vec = _const_spec((1, d))
    return pl.pallas_call(
        functools.partial(_rw_pre_kernel, tiles_per_seq=tps, lora_w=lora_w, lora_a=lora_a),
        out_shape=(act, jax.ShapeDtypeStruct((t, d), F32), act, act, act, act, act),
        grid=(t // tm,),
        in_specs=[
            row_spec,
            pl.BlockSpec((V7X_SUBLANES, d), lambda i: (jnp.maximum(i * rb - 1, 0), 0)),
            vec,
            _const_spec(mu.shape),
            _const_spec(wrkv.shape),
            _const_spec(la.shape),
            _const_spec(lbw.shape), _const_spec(lba.shape), _const_spec(lbg.shape),
            vec, vec, vec, vec,
        ],
        out_specs=(row_spec,) * 7,
        compiler_params=_params(("parallel",)),
        name="rw_pre",
    )(h, h, g, mu, wrkv, la, lbw, lba, lbg, w0, a0, k_k, k_a)


def _rw_chunk_kernel(r_ref, lw_ref, k_ref, v_ref, kk_ref, kb_ref, g_ref, rk_ref, gng_ref, gnb_ref,
                     o_ref, st_ref):
    ci = pl.program_id(1)
    nb, ln, _ = r_ref.shape
    pair = 2 * RW_DH

    @pl.when(ci == 0)
    def _():
        st_ref[...] = jnp.zeros_like(st_ref)

    trow = lax.broadcasted_iota(jnp.int32, (ln, ln), 0)
    tcol = lax.broadcasted_iota(jnp.int32, (ln, ln), 1)
    tril = (tcol <= trow).astype(F32)
    lane = lax.broadcasted_iota(jnp.int32, (ln, pair), 1)
    head0 = lane < RW_DH
    srow = lax.broadcasted_iota(jnp.int32, (2 * ln, 2 * ln), 0)
    scol = lax.broadcasted_iota(jnp.int32, (2 * ln, 2 * ln), 1)
    strict_t = srow < scol
    incl_t = srow <= scol
    eye = (srow == scol).astype(F32)
    same_blk = (srow // RW_BLOCK) == (scol // RW_BLOCK)

    def stack(x):
        return jnp.concatenate([jnp.where(head0, x, 0.0), jnp.where(head0, 0.0, x)], axis=0).astype(BF16)

    pair_slices = [slice(p * pair, (p + 1) * pair) for p in range(RW_HEADS // 2)]

    def head_sums(x):
        s0 = jnp.sum(jnp.where(head0, x, 0.0), axis=-1, keepdims=True)
        s1 = jnp.sum(jnp.where(head0, 0.0, x), axis=-1, keepdims=True)
        return s0, s1

    def make_group(rows):
        chains = [(bi, p) for bi in rows for p in range(RW_HEADS // 2)]
        ids = range(len(chains))
        gs = {k_: [] for k_ in ("ar_s", "b_s", "k_s", "bk", "v_t", "g_last", "bonus")}

        def prologue(bi):
            lw = lw_ref[bi]
            cum = _dot_exact(tril, lw)
            cum_last = cum[ln - 1:ln, :]
            g_inv = jnp.exp(-cum)
            g_rem = jnp.exp(cum_last - cum)
            kk_raw = kk_ref[bi].astype(F32)
            r_all = r_ref[bi].astype(F32)
            kk = k_ref[bi].astype(F32)
            v_all = v_ref[bi].astype(F32)
            rkk = r_all * kk * rk_ref[...]
            inv_parts = []
            for sl in pair_slices:
                s0, s1 = head_sums(kk_raw[:, sl] * kk_raw[:, sl])
                inv_parts.append(jnp.where(head0, lax.rsqrt(jnp.maximum(s0, 1e-24)),
                                           lax.rsqrt(jnp.maximum(s1, 1e-24))))
                t0, t1 = head_sums(rkk[:, sl])
                gs["bonus"].append(jnp.where(head0, t0, t1) * v_all[:, sl])
            inv_all = jnp.concatenate(inv_parts, axis=1)
            kn = kk_raw * inv_all
            bb = kb_ref[bi].astype(F32) * inv_all
            a_all = -kn * jnp.exp(cum - lw)
            rt_all = r_all * jnp.exp(cum)
            bt_all = bb * g_inv
            kt_all = kk * g_inv
            bh_all = bb * g_rem
            kh_all = kk * g_rem
            gl = jnp.exp(cum_last)
            for sl in pair_slices:
                gs["ar_s"].append(jnp.concatenate([stack(a_all[:, sl]), stack(rt_all[:, sl])], axis=0))
                gs["b_s"].append(stack(bt_all[:, sl]))
                gs["k_s"].append(stack(kt_all[:, sl]))
                gs["bk"].append(jnp.concatenate([stack(bh_all[:, sl]), stack(kh_all[:, sl])], axis=0))
                vt = v_all[:, sl].T
                gs["v_t"].append(jnp.concatenate([vt[:RW_DH], vt[RW_DH:]], axis=1).astype(BF16))
                gs["g_last"].append(gl[:, sl])

        def products():
            gs["st"] = [st_ref[bi, p] for bi, p in chains]
            gs["m"] = [_dot_nt(jnp.concatenate([gs["b_s"][c], gs["k_s"][c], gs["st"][c].astype(BF16)], axis=0),
                               gs["ar_s"][c]) for c in ids]

        def masks():
            m = gs["m"]
            nt = [jnp.where(strict_t, m[c][:2 * ln, :2 * ln], 0.0) for c in ids]
            gs["pw"] = [jnp.where(same_blk, nt[c], 0.0) for c in ids]
            gs["noff"] = [jnp.where(same_blk, 0.0, nt[c]).astype(BF16) for c in ids]
            ak = [jnp.where(strict_t, m[c][2 * ln:4 * ln, :2 * ln], 0.0).astype(BF16) for c in ids]
            gs["rbk"] = [jnp.concatenate([jnp.where(incl_t, m[c][:2 * ln, 2 * ln:], 0.0).astype(BF16),
                                          jnp.where(incl_t, m[c][2 * ln:4 * ln, 2 * ln:], 0.0).astype(BF16)],
                                         axis=0) for c in ids]
            gs["inter_r"] = [m[c][4 * ln:, 2 * ln:] for c in ids]
            gs["u"] = [m[c][4 * ln:, :2 * ln] + _dot(gs["v_t"][c], ak[c]) for c in ids]

        def diag_first():
            p16 = [gs["pw"][c].astype(BF16) for c in ids]
            gs["e"] = [gs["pw"][c] + eye for c in ids]
            gs["pw"] = [_dot(p16[c], p16[c]) for c in ids]

        def diag_mid():
            q16 = [gs["pw"][c].astype(BF16) for c in ids]
            both = [_dot(jnp.concatenate([gs["e"][c].astype(BF16), q16[c]], axis=0), q16[c]) for c in ids]
            gs["e"] = [gs["e"][c] + both[c][:2 * ln] for c in ids]
            gs["pw"] = [both[c][2 * ln:] for c in ids]

        def diag_last():
            gs["e"] = [gs["e"][c] + _dot(gs["e"][c].astype(BF16), gs["pw"][c].astype(BF16)) for c in ids]

        def off_first():
            d16 = [gs["e"][c].astype(BF16) for c in ids]
            both = [_dot(jnp.concatenate([gs["noff"][c], gs["u"][c].astype(BF16)], axis=0), d16[c]) for c in ids]
            gs["pw"] = [both[c][:2 * ln] for c in ids]
            gs["u"] = [both[c][2 * ln:] for c in ids]

        def off_mid():
            m16 = [gs["pw"][c].astype(BF16) for c in ids]
            both = [_dot(jnp.concatenate([gs["u"][c].astype(BF16), m16[c]], axis=0), m16[c]) for c in ids]
            gs["u"] = [gs["u"][c] + both[c][:RW_DH] for c in ids]
            gs["pw"] = [both[c][RW_DH:] for c in ids]

        def off_last():
            gs["u"] = [gs["u"][c] + _dot(gs["u"][c].astype(BF16), gs["pw"][c].astype(BF16)) for c in ids]

        diag_levels = int(math.log2(RW_BLOCK))
        off_levels = max(int(math.ceil(math.log2(ln // RW_BLOCK))), 1)
        solve = ([diag_first] + [diag_mid] * (diag_levels - 2) + [diag_last]
                 + [off_first] + [off_mid] * (off_levels - 1) + [off_last])

        def readout():
            uv = [jnp.concatenate([gs["u"][c].astype(BF16), gs["v_t"][c]], axis=1) for c in ids]
            gs["y_t"] = [gs["inter_r"][c] + _dot(uv[c], gs["rbk"][c]) for c in ids]
            for c, (bi, p) in enumerate(chains):
                st_ref[bi, p] = gs["st"][c] * gs["g_last"][c] + _dot(uv[c], gs["bk"][c])

        def epilogue(c):
            bi, p = chains[c]
            sl = pair_slices[p]
            y_t = gs["y_t"][c]
            yc = y_t - jnp.mean(y_t, axis=0, keepdims=True)
            yn = (yc * lax.rsqrt(jnp.mean(yc * yc, axis=0, keepdims=True) + RW_GN_EPS)).T
            y_p = jnp.concatenate([yn[:ln], yn[ln:]], axis=1)
            out = y_p * gng_ref[:, sl] + gnb_ref[:, sl] + gs["bonus"][c]
            o_ref[bi, :, sl] = (out * g_ref[bi, :, sl].astype(F32)).astype(BF16)

        stages = [products, masks] + solve + [readout]
        return ([functools.partial(prologue, bi) for bi in rows], stages,
                [functools.partial(epilogue, c) for c in ids])

    def interleave(stages, fillers):
        per = -(-len(fillers) // max(len(stages), 1)) if fillers else 0
        fillers = list(fillers)
        for stage in stages:
            stage()
            for piece in fillers[:per]:
                piece()
            fillers = fillers[per:]
        for piece in fillers:
            piece()

    half = max(nb // 2, 1)
    groups = [make_group(range(g0, min(g0 + half, nb))) for g0 in range(0, nb, half)]
    for piece in groups[0][0]:
        piece()
    pending = []
    for gi, (_, stages, epilogues) in enumerate(groups):
        nxt = groups[gi + 1][0] if gi + 1 < len(groups) else []
        interleave(stages, list(pending) + list(nxt))
        pending = epilogues
    for piece in pending:
        piece()


def _rw_chunk(r, lw, k, v, kk, kb, g, r_k, gn_g, gn_b, batch, seq):
    t, d = r.shape
    ln = min(RW_CHUNK, seq)
    nc = seq // ln
    nb = math.gcd(batch, RW_BATCH_PER_STEP)
    row_spec = pl.BlockSpec((nb, ln, d), lambda bi, ci: (bi, ci, 0))
    vec = pl.BlockSpec((1, d), lambda bi, ci: (0, 0))
    acts = [x.reshape(batch, seq, d) for x in (r, lw, k, v, kk, kb, g)]
    out = pl.pallas_call(
        _rw_chunk_kernel,
        out_shape=jax.ShapeDtypeStruct((batch, seq, d), BF16),
        grid=(batch // nb, nc),
        in_specs=[row_spec] * 7 + [vec, vec, vec],
        out_specs=row_spec,
        scratch_shapes=[pltpu.VMEM((nb, RW_HEADS // 2, RW_DH, 2 * RW_DH), F32)],
        compiler_params=_params(("parallel", "arbitrary")),
        name="rw_chunk",
    )(*acts, r_k, gn_g, gn_b)
    return out.reshape(t, d)


def _rope_tables(seq):
    half = RET_DK // 2
    pos = jnp.arange(seq, dtype=F32)
    inv_freq = 1.0 / (ROPE_BASE ** jnp.linspace(0.0, 1.0, half, dtype=F32))
    ang = pos[:, None] * inv_freq[None, :]
    local = (jnp.arange(seq) % min(RET_CHUNK, seq)).astype(F32) + 1.0
    log_gamma = jnp.asarray([_ret_log_gamma(hd) for hd in range(RET_HEADS)], F32)
    q_scale = jnp.exp(log_gamma[:, None] * local[None, :])
    k_scale = jnp.exp(-log_gamma[:, None] * local[None, :]) * RET_DK ** -0.5
    scale = jnp.concatenate([q_scale, k_scale], axis=0)[:, :, None]
    return jnp.cos(ang)[None] * scale, jnp.sin(ang)[None] * scale


def _retention_mixer(h, norm_g, w_in16, layer, gn, batch, seq):
    cos, sin = _rope_tables(seq)
    proj = _ret_proj(h, norm_g[None, :], w_in16, layer, cos, sin, seq)
    return _ret_chunk(proj, gn[None, :].astype(F32), batch, seq)


def _block_diag_tiles(w, tile):
    nb, bs, _ = w.shape
    rows = w.reshape(nb * bs // tile, tile, bs)
    dense = jnp.tile(rows, (1, 1, tile // bs))
    idx = jnp.arange(tile) // bs
    return jnp.where(idx[:, None] == idx[None, :], dense, 0.0)


def _mlstm_mixer(h, norm_g, w_in, conv_w, conv_b, wq, wk, wv, w_gate, b_gate, gn, skip, batch, seq):
    tile = 2 * V7X_LANES
    mq = _block_diag_tiles(wq, tile).astype(BF16)
    mk = _block_diag_tiles(wk, tile).astype(BF16)
    mv = _block_diag_tiles(wv, tile).astype(BF16)
    n_gate = w_gate.shape[1]
    wgt = jnp.pad(w_gate.reshape(3, ML_INNER, n_gate), ((0, 0), (0, 0), (0, V7X_LANES - n_gate))).astype(BF16)
    bg = jnp.pad(b_gate, (0, V7X_LANES - n_gate))[None, :].astype(F32)
    c, q, k, v, z, gates = _ml_front(h, norm_g[None, :], w_in.astype(BF16), conv_w, conv_b[None, :],
                                     mq, mk, mv, wgt, bg, seq)
    return _ml_chunk(q, k, v, gates, c, z, gn[None, :], skip[None, :], batch, seq)


def _pad_to(x, axis, size):
    pad = [(0, 0)] * x.ndim
    pad[axis] = (0, size - x.shape[axis])
    return jnp.pad(x, pad)


def _rwkv_mixer(h, norm_g, mu, w_rkv, w0, w_la, w_lb, a0, a_la, a_lb, g_la, g_lb, k_k, k_a, r_k,
                gn_g, gn_b, batch, seq):
    d = h.shape[1]
    lw_ = -(-w_la.shape[1] // V7X_LANES) * V7X_LANES
    la_ = -(-a_la.shape[1] // V7X_LANES) * V7X_LANES
    lg_ = -(-g_la.shape[1] // V7X_LANES) * V7X_LANES
    la = jnp.concatenate([_pad_to(w_la, 1, lw_), _pad_to(a_la, 1, la_), _pad_to(g_la, 1, lg_)],
                         axis=1).astype(BF16)
    lbw = _pad_to(w_lb, 0, lw_).astype(BF16)
    lba = _pad_to(a_lb, 0, la_).astype(BF16)
    lbg = _pad_to(g_lb, 0, lg_).astype(BF16)
    r, lw, k, v, kk, kb, g = _rw_pre(h, norm_g[None, :], mu, w_rkv.astype(BF16), la, lbw, lba, lbg,
                                     w0[None, :], a0[None, :], k_k[None, :], k_a[None, :], seq, lw_, la_)
    return _rw_chunk(r, lw, k, v, kk, kb, g, r_k.reshape(1, d), gn_g[None, :], gn_b[None, :], batch, seq)


def kernel(x, norm_mix, norm_ffn, norm_final, ret_w_in, ret_gn, ret_w_out, ml_w_in, ml_conv_w, ml_conv_b, ml_wq, ml_wk, ml_wv, ml_w_gate, ml_b_gate, ml_gn, ml_skip, ml_w_out, rw_mu, rw_w_rkv, rw_w0, rw_w_lora_a, rw_w_lora_b, rw_a0, rw_a_lora_a, rw_a_lora_b, rw_g_lora_a, rw_g_lora_b, rw_k_k, rw_k_a, rw_r_k, rw_gn_g, rw_gn_b, rw_w_out, ffn_w_gu, ffn_w_down):
    batch, seq, d = x.shape
    depth = norm_mix.shape[0]
    h = x.reshape(batch * seq, d)
    ret_w_in16 = ret_w_in.astype(BF16)
    w_out16 = (ret_w_out.astype(BF16), ml_w_out.astype(BF16), rw_w_out.astype(BF16))
    w_gu16 = ffn_w_gu.astype(BF16)
    w_down16 = ffn_w_down.astype(BF16)
    for i in range(depth):
        kind, j = i % 3, i // 3
        if kind == 0:
            y = _retention_mixer(h, norm_mix[i], ret_w_in16, j, ret_gn[j], batch, seq)
        elif kind == 1:
            y = _mlstm_mixer(h, norm_mix[i], ml_w_in[j], ml_conv_w[j], ml_conv_b[j], ml_wq[j], ml_wk[j],
                             ml_wv[j], ml_w_gate[j], ml_b_gate[j], ml_gn[j], ml_skip[j], batch, seq)
        else:
            y = _rwkv_mixer(h, norm_mix[i], rw_mu[j], rw_w_rkv[j], rw_w0[j], rw_w_lora_a[j],
                            rw_w_lora_b[j], rw_a0[j], rw_a_lora_a[j], rw_a_lora_b[j], rw_g_lora_a[j],
                            rw_g_lora_b[j], rw_k_k[j], rw_k_a[j], rw_r_k[j], rw_gn_g[j], rw_gn_b[j],
                            batch, seq)
        h = _mix_ffn(y, w_out16[kind], j, h, norm_ffn[i][None, :], w_gu16, w_down16, i,
                     norm_final[None, :], i == depth - 1)
    return h.reshape(batch, seq, d)
```
